```python
import jax, jax.numpy as jnp
from jax import lax
import numpy as np

D_MODEL = 1024
BATCH = 4
SEQ = 4096
DEPTH = 2
DEC_BATCH = 128
DEC_SEQ = 4
PAST_LEN = 2048
PAGE_SIZE = 128

N_MIXERS = 2
N_A_LAYERS = (DEPTH + 1) // 2
N_B_LAYERS = DEPTH // 2
N_HEADS = 8
HEAD_DIM = D_MODEL // N_HEADS
N_KV_HEADS = 2
GROUP = N_HEADS // N_KV_HEADS
IDX_HEADS = 8
IDX_DIM = 64
TOPK_MAX = 256
Q_BLOCK = 128
POOL_WINDOWS = (2, 4, 8, 16)
N_POOL_GROUPS = len(POOL_WINDOWS)
POOL_GROUP_DIM = D_MODEL // N_POOL_GROUPS
POOL_STATE = max(POOL_WINDOWS) - 1
D_FF = 4 * D_MODEL
RMS_EPS = 1e-6
Q_W = N_HEADS * HEAD_DIM
KV_W = N_KV_HEADS * HEAD_DIM
QI_W = IDX_HEADS * IDX_DIM
PROJ_SPLITS = (Q_W, Q_W + KV_W, Q_W + 2 * KV_W, Q_W + 2 * KV_W + QI_W, Q_W + 2 * KV_W + QI_W + IDX_DIM)
D_PROJ = Q_W + 2 * KV_W + QI_W + IDX_DIM + IDX_HEADS

kernel_name = 'dsa_pool_hybrid_step'


def rmsnorm(x, g):
    xf = x.astype(jnp.float32)
    y = xf * lax.rsqrt(jnp.mean(xf * xf, axis=-1, keepdims=True) + RMS_EPS)
    return (y * g.astype(jnp.float32)).astype(x.dtype)


def modulate(h, shift, scale):
    return h * (1 + scale[:, None, :]) + shift[:, None, :]


def dsa_block(qb, qib, wib, posb, keys, vals, kidx, k_sel):
    n, qblk = qb.shape[0], qb.shape[1]
    L = keys.shape[1]
    logits = jnp.einsum('nqhd,nsd->nqhs', qib.astype(jnp.float32), kidx.astype(jnp.float32)) * (IDX_DIM ** -0.5)
    score = jnp.einsum('nqhs,nqh->nqs', jax.nn.relu(logits), wib.astype(jnp.float32)) * (IDX_HEADS ** -0.5)
    causal = jnp.arange(L)[None, :] <= posb[:, None]
    score = jnp.where(causal[None], score, -jnp.inf)
    _, idx = lax.top_k(score, k_sel)
    valid = idx <= posb[None, :, None]
    k_g = jax.vmap(lambda a, i: a[i])(keys, idx)
    v_g = jax.vmap(lambda a, i: a[i])(vals, idx)
    s = jnp.einsum('nqgrd,nqkgd->nqgrk', qb.astype(jnp.float32), k_g.astype(jnp.float32)) * (HEAD_DIM ** -0.5)
    s = jnp.where(valid[:, :, None, None, :], s, -jnp.inf)
    p = jax.nn.softmax(s, axis=-1)
    o = jnp.einsum('nqgrk,nqkgd->nqgrd', p, v_g.astype(jnp.float32))
    return o.reshape(n, qblk, N_HEADS * HEAD_DIM).astype(qb.dtype)


def dsa_mixer(h, w_in, w_o, past_k, past_v, past_kidx, pos0):
    n, t, _ = h.shape
    proj = h @ w_in
    q, k, v, qi, ki, wi = jnp.split(proj, PROJ_SPLITS, axis=-1)
    q = q.reshape(n, t, N_KV_HEADS, GROUP, HEAD_DIM)
    k = k.reshape(n, t, N_KV_HEADS, HEAD_DIM)
    v = v.reshape(n, t, N_KV_HEADS, HEAD_DIM)
    qi = qi.reshape(n, t, IDX_HEADS, IDX_DIM)
    if past_k is None:
        keys, vals, kidx = k, v, ki
    else:
        keys = jnp.concatenate([past_k.astype(k.dtype), k], axis=1)
        vals = jnp.concatenate([past_v.astype(v.dtype), v], axis=1)
        kidx = jnp.concatenate([past_kidx.astype(ki.dtype), ki], axis=1)
    L = keys.shape[1]
    k_sel = min(TOPK_MAX, L // 4)
    qb = min(Q_BLOCK, t)
    nblk = t // qb

    def to_blocks(a):
        return jnp.moveaxis(a.reshape((n, nblk, qb) + a.shape[2:]), 1, 0)

    pos = pos0 + jnp.arange(t)
    xs = (to_blocks(q), to_blocks(qi), to_blocks(wi), pos.reshape(nblk, qb))
    o = lax.map(lambda a: dsa_block(a[0], a[1], a[2], a[3], keys, vals, kidx, k_sel), xs)
    o = jnp.moveaxis(o, 0, 1).reshape(n, t, N_HEADS * HEAD_DIM)
    return o @ w_o, (k, v, ki)


def pool_mixer(h, prev, pos0, w_grp, scale):
    n, t, d = h.shape
    ext = jnp.concatenate([prev.astype(h.dtype), h], axis=1)
    cs = jnp.pad(jnp.cumsum(ext.astype(jnp.float32), axis=1), ((0, 0), (1, 0), (0, 0)))
    P = POOL_STATE
    end = cs[:, P + 1:P + 1 + t]
    pos = pos0 + jnp.arange(t)
    parts = []
    for g, w in enumerate(POOL_WINDOWS):
        lo, hi = g * POOL_GROUP_DIM, (g + 1) * POOL_GROUP_DIM
        win = end[..., lo:hi] - cs[:, P + 1 - w:P + 1 - w + t, lo:hi]
        cnt = jnp.minimum(w, pos + 1).astype(jnp.float32)[None, :, None]
        parts.append(win / cnt - h[..., lo:hi].astype(jnp.float32))
    pooled = jnp.stack(parts, axis=2)
    out = jnp.einsum('ntgc,gcd->ntgd', pooled, w_grp.astype(jnp.float32)).reshape(n, t, d)
    out = out * scale.astype(jnp.float32)
    return out.astype(h.dtype), ext[:, -P:]


def sq_relu_mlp(h, w_up, w_down):
    return jnp.square(jax.nn.relu(h @ w_up)) @ w_down


def setup_inputs(seed: int = 0) -> dict:
    key = jax.random.key(seed)
    ks = jax.random.split(key, 24)
    f32 = jnp.float32
    n_pages = PAST_LEN // PAGE_SIZE
    n_used = DEC_BATCH * n_pages
    n_pool = n_used + n_used // 4

    def nrm(k, shape, s):
        return jax.random.normal(k, shape, f32) * s

    page_table = jax.random.permutation(ks[6], n_pool)[:n_used].reshape(DEC_BATCH, n_pages).astype(jnp.int32)
    return {
        'x_prompt': nrm(ks[0], (BATCH, SEQ, D_MODEL), 1.0),
        'x_sample': nrm(ks[1], (DEC_BATCH, DEC_SEQ, D_MODEL), 1.0),
        'cache_k': nrm(ks[2], (N_A_LAYERS, n_pool, PAGE_SIZE, N_KV_HEADS, HEAD_DIM), 1.0),
        'cache_v': nrm(ks[3], (N_A_LAYERS, n_pool, PAGE_SIZE, N_KV_HEADS, HEAD_DIM), 1.0),
        'cache_kidx': nrm(ks[4], (N_A_LAYERS, n_pool, PAGE_SIZE, IDX_DIM), 1.0),
        'state_pool': nrm(ks[5], (N_B_LAYERS, DEC_BATCH, POOL_STATE, D_MODEL), 1.0),
        'page_table': page_table,
        'c_prompt': nrm(ks[7], (BATCH, D_MODEL), 1.0),
        'c_sample': nrm(ks[8], (DEC_BATCH, D_MODEL), 1.0),
        'norm_mix_g': 1.0 + nrm(ks[9], (DEPTH, D_MODEL), 0.05),
        'norm_ffn_g': 1.0 + nrm(ks[10], (DEPTH, D_MODEL), 0.05),
        'w_mod': nrm(ks[11], (DEPTH, D_MODEL, 6 * D_MODEL), 0.5 * D_MODEL ** -0.5),
        'b_mod': nrm(ks[12], (DEPTH, 6 * D_MODEL), 0.02),
        'dsa_w_in': nrm(ks[13], (N_A_LAYERS, D_MODEL, D_PROJ), D_MODEL ** -0.5),
        'dsa_w_o': nrm(ks[14], (N_A_LAYERS, N_HEADS * HEAD_DIM, D_MODEL), (N_HEADS * HEAD_DIM) ** -0.5),
        'pool_w': nrm(ks[15], (N_B_LAYERS, N_POOL_GROUPS, POOL_GROUP_DIM, POOL_GROUP_DIM), POOL_GROUP_DIM ** -0.5),
        'pool_scale': 1.0 + nrm(ks[16], (N_B_LAYERS, D_MODEL), 0.1),
        'w_up': nrm(ks[17], (DEPTH, D_MODEL, D_FF), D_MODEL ** -0.5),
        'w_down': nrm(ks[18], (DEPTH, D_FF, D_MODEL), D_FF ** -0.5),
        'final_norm_g': 1.0 + nrm(ks[19], (D_MODEL,), 0.05),
    }


def reference(x_prompt, x_sample, cache_k, cache_v, cache_kidx, state_pool, page_table, c_prompt, c_sample,
              norm_mix_g, norm_ffn_g, w_mod, b_mod, dsa_w_in, dsa_w_o, pool_w, pool_scale, w_up, w_down,
              final_norm_g):
    def run(x, c, is_sample):
        n = x.shape[0]
        pos0 = PAST_LEN if is_sample else 0
        mod_in = jax.nn.silu(c)
        new_k, new_v, new_ki, new_pool = [], [], [], []
        for i in range(DEPTH):
            mod = mod_in @ w_mod[i] + b_mod[i]
            sh1, sc1, g1, sh2, sc2, g2 = jnp.split(mod, 6, axis=-1)
            h = modulate(rmsnorm(x, norm_mix_g[i]), sh1, sc1)
            j = i // N_MIXERS
            if i % N_MIXERS == 0:
                if is_sample:
                    pk = cache_k[j][page_table].reshape(n, -1, N_KV_HEADS, HEAD_DIM)
                    pv = cache_v[j][page_table].reshape(n, -1, N_KV_HEADS, HEAD_DIM)
                    pki = cache_kidx[j][page_table].reshape(n, -1, IDX_DIM)
                else:
                    pk, pv, pki = None, None, None
                m, (k_rows, v_rows, ki_rows) = dsa_mixer(h, dsa_w_in[j], dsa_w_o[j], pk, pv, pki, pos0)
                new_k.append(k_rows)
                new_v.append(v_rows)
                new_ki.append(ki_rows)
            else:
                prev = state_pool[j] if is_sample else jnp.zeros((n, POOL_STATE, D_MODEL), h.dtype)
                m, st = pool_mixer(h, prev, pos0, pool_w[j], pool_scale[j])
                new_pool.append(st)
            x = x + g1[:, None, :] * m
            h = modulate(rmsnorm(x, norm_ffn_g[i]), sh2, sc2)
            x = x + g2[:, None, :] * sq_relu_mlp(h, w_up[i], w_down[i])
        y = rmsnorm(x, final_norm_g)
        return y, jnp.stack(new_k), jnp.stack(new_v), jnp.stack(new_ki), jnp.stack(new_pool)

    y_prompt, k_p, v_p, ki_p, pool_p = run(x_prompt, c_prompt, False)
    y_sample, k_s, v_s, ki_s, pool_s = run(x_sample, c_sample, True)
    return (y_prompt, y_sample, k_p, v_p, ki_p, pool_p, k_s, v_s, ki_s, pool_s)
```

```python
import functools

import jax
import jax.numpy as jnp
from jax import lax
from jax.experimental import pallas as pl
from jax.experimental.pallas import tpu as pltpu

F32 = jnp.float32
BF16 = jnp.bfloat16
I32 = jnp.int32

D_MODEL = 1024
DEPTH = 2
PAST_LEN = 2048
PAGE_SIZE = 128
N_PAGES = PAST_LEN // PAGE_SIZE
N_HEADS = 8
HEAD_DIM = 128
N_KV_HEADS = 2
GROUP = N_HEADS // N_KV_HEADS
IDX_HEADS = 8
IDX_DIM = 64
TOPK = 256
POOL_WINDOWS = (2, 4, 8, 16)
POOL_GROUP_DIM = D_MODEL // len(POOL_WINDOWS)
POOL_STATE = max(POOL_WINDOWS) - 1
D_FF = 4 * D_MODEL
RMS_EPS = 1e-6
Q_W = N_HEADS * HEAD_DIM
KV_W = N_KV_HEADS * HEAD_DIM
QI_W = IDX_HEADS * IDX_DIM
D_PROJ = Q_W + 2 * KV_W + QI_W + IDX_DIM + IDX_HEADS

LANES = 128
SUBLANES = 8
VMEM_LIMIT_BYTES = 56 * 1024 * 1024

KIW_W = LANES
D_PROJ_PAD = Q_W + 2 * KV_W + QI_W + KIW_W
OFF_K = Q_W
OFF_V = OFF_K + KV_W
OFF_QI = OFF_V + KV_W
OFF_KIW = OFF_QI + QI_W

ATT_SCALE = HEAD_DIM ** -0.5
IDX_SCALE = IDX_DIM ** -0.5
IDX_HEAD_SCALE = IDX_HEADS ** -0.5

INT_MIN = -(2 ** 31)
INT_MAX = 2 ** 31 - 1
NEG_BIG = -1e30

TQ = 128
TK = 512
FF_CHUNK = 512
HALO = 16
SEL_ROWS = 128
L_SAMPLE = PAST_LEN + PAGE_SIZE
T_NEW = 4
Q8 = SUBLANES


def _params(sem):
    return pltpu.CompilerParams(dimension_semantics=sem, vmem_limit_bytes=VMEM_LIMIT_BYTES)


def _dot(a, b):
    return jnp.dot(a, b, preferred_element_type=F32)


def _dot_nt(a, b):
    return lax.dot_general(a, b, (((1,), (1,)), ((), ())), preferred_element_type=F32)


def _rms_mod(x, g, shift, scale):
    ms = jnp.mean(x * x, axis=-1, keepdims=True)
    y = x * lax.rsqrt(ms + RMS_EPS) * g
    return y * (1.0 + scale) + shift


def _rmsnorm(x, g):
    ms = jnp.mean(x * x, axis=-1, keepdims=True)
    return x * lax.rsqrt(ms + RMS_EPS) * g


def _mod_kernel(c_ref, w_ref, b_ref, o_ref):
    c = c_ref[...]
    a = c * (1.0 / (1.0 + jnp.exp(-c)))
    w = w_ref[0]
    a_hi = a.astype(BF16)
    a_lo = (a - a_hi.astype(F32)).astype(BF16)
    w_hi = w.astype(BF16)
    w_lo = (w - w_hi.astype(F32)).astype(BF16)
    acc = _dot(a_hi, w_hi) + _dot(a_lo, w_hi) + _dot(a_hi, w_lo)
    o_ref[0] = acc + b_ref[0]


def _mod_call(c_all, w_mod, b_mod):
    rows = c_all.shape[0]
    tn = 1536
    return pl.pallas_call(
        _mod_kernel,
        grid=(DEPTH, 6 * D_MODEL // tn),
        in_specs=[
            pl.BlockSpec((rows, D_MODEL), lambda i, j: (0, 0)),
            pl.BlockSpec((1, D_MODEL, tn), lambda i, j: (i, 0, j)),
            pl.BlockSpec((1, 1, tn), lambda i, j: (i, 0, j)),
        ],
        out_specs=pl.BlockSpec((1, rows, tn), lambda i, j: (i, 0, j)),
        out_shape=jax.ShapeDtypeStruct((DEPTH, rows, 6 * D_MODEL), F32),
        compiler_params=_params(("arbitrary", "arbitrary")),
        name="mod_vectors",
    )(c_all, w_mod, b_mod.reshape(DEPTH, 1, 6 * D_MODEL))


def _proj_kernel(x_ref, sh_ref, sc_ref, g_ref, w_ref,
                 k_ref, v_ref, ki_ref, q_ref, qi_ref, kiw_ref, kb_ref, vb_ref, kib_ref):
    h = _rms_mod(x_ref[0], g_ref[...], sh_ref[0], sc_ref[0]).astype(BF16)
    q_ref[0] = (_dot(h, w_ref[:, 0:Q_W]) * ATT_SCALE).astype(BF16)
    k = _dot(h, w_ref[:, OFF_K:OFF_K + KV_W])
    k_ref[0] = k
    kb_ref[0] = k.astype(BF16)
    v = _dot(h, w_ref[:, OFF_V:OFF_V + KV_W])
    v_ref[0] = v
    vb_ref[0] = v.astype(BF16)
    qi_ref[0] = (_dot(h, w_ref[:, OFF_QI:OFF_QI + QI_W]) * IDX_SCALE).astype(BF16)
    kiw = _dot(h, w_ref[:, OFF_KIW:OFF_KIW + KIW_W])
    kiw_ref[0] = kiw
    ki = kiw[:, 0:IDX_DIM]
    ki_ref[0] = ki
    kib_ref[0] = ki.astype(BF16)


def _mod_spec(arr, tm):
    if arr.shape[1] == 1:
        return pl.BlockSpec((1, 1, D_MODEL), lambda b, i: (b, 0, 0))
    return pl.BlockSpec((1, tm, D_MODEL), lambda b, i: (b, i, 0))


def _proj_call(x, shift, scale, g, w_in_b, tm):
    nb, t, _ = x.shape
    tok = lambda w: pl.BlockSpec((1, tm, w), lambda b, i: (b, i, 0))
    shp = lambda w, dt: jax.ShapeDtypeStruct((nb, t, w), dt)
    return pl.pallas_call(
        _proj_kernel,
        grid=(nb, t // tm),
        in_specs=[
            tok(D_MODEL), _mod_spec(shift, tm), _mod_spec(scale, tm),
            pl.BlockSpec((1, D_MODEL), lambda b, i: (0, 0)),
            pl.BlockSpec((D_MODEL, D_PROJ_PAD), lambda b, i: (0, 0)),
        ],
        out_specs=[tok(KV_W), tok(KV_W), tok(IDX_DIM), tok(Q_W), tok(QI_W), tok(KIW_W),
                   tok(KV_W), tok(KV_W), tok(IDX_DIM)],
        out_shape=[shp(KV_W, F32), shp(KV_W, F32), shp(IDX_DIM, F32), shp(Q_W, BF16), shp(QI_W, BF16),
                   shp(KIW_W, F32), shp(KV_W, BF16), shp(KV_W, BF16), shp(IDX_DIM, BF16)],
        compiler_params=_params(("arbitrary", "arbitrary")),
        name="dsa_proj",
    )(x, shift, scale, g.reshape(1, D_MODEL), w_in_b)


def _score_keys(score, valid):
    b = lax.bitcast_convert_type(score, I32)
    key = jnp.where(b < 0, (b ^ INT_MAX) + 1, b)
    return jnp.where(valid, key, INT_MIN)


def _select_bias(keys_ref, bias_ref, ithr_ref, nblk, blk, idx_bits, row_ok):
    rows = keys_ref.shape[0]
    nsub = blk // LANES
    lane = lax.broadcasted_iota(I32, (rows, LANES), 1)

    def count(pred):
        def body(c, acc):
            for j in range(nsub):
                col0 = pl.multiple_of(c * blk + j * LANES, LANES)
                acc = acc + jnp.where(pred(keys_ref[:, pl.ds(col0, LANES)], col0), 1, 0)
            return acc
        acc = lax.fori_loop(0, nblk, body, jnp.zeros((rows, LANES), I32))
        return jnp.sum(acc, axis=1, keepdims=True)

    def bcast(v):
        return jnp.broadcast_to(v, (rows, LANES))

    n_nonneg = count(lambda kb, col0: kb >= 0)
    prefix = jnp.where(n_nonneg >= TOPK, 0, INT_MIN).astype(I32)

    def bit_body(t, prefix):
        cand = prefix | lax.shift_left(jnp.int32(1), 30 - t)
        cand_b = bcast(cand)
        n = count(lambda kb, col0: kb >= cand_b)
        return jnp.where(n >= TOPK, cand, prefix)

    thr = lax.fori_loop(0, 31, bit_body, prefix)
    thr_b = bcast(thr)
    n_gt = count(lambda kb, col0: kb > thr_b)
    n_ge = count(lambda kb, col0: kb >= thr_b)
    has_valid_thr = thr > INT_MIN
    tie = (n_ge > TOPK) & has_valid_thr & row_ok
    ithr_ref[...] = bcast(jnp.where(has_valid_thr, INT_MAX, -1).astype(I32))

    @pl.when(jnp.max(tie.astype(I32)) > 0)
    def _():
        need = TOPK - n_gt

        def idx_body(t, ithr):
            cand = ithr | lax.shift_left(jnp.int32(1), idx_bits - 1 - t)
            cand_b = bcast(cand)
            n = count(lambda kb, col0: (kb == thr_b) & ((col0 + lane) < cand_b))
            return jnp.where(n < need, cand, ithr)

        ithr = lax.fori_loop(0, idx_bits, idx_body, jnp.zeros((rows, 1), I32))
        ithr_ref[...] = jnp.where(bcast(tie), bcast(ithr), ithr_ref[...])

    ithr_b = ithr_ref[...]

    def bias_body(c, carry):
        for j in range(nsub):
            col0 = pl.multiple_of(c * blk + j * LANES, LANES)
            kb = keys_ref[:, pl.ds(col0, LANES)]
            sel = (kb > thr_b) | ((kb == thr_b) & ((col0 + lane) <= ithr_b))
            bias_ref[:, pl.ds(col0, LANES)] = jnp.where(sel, 0.0, NEG_BIG).astype(F32)
        return carry

    lax.fori_loop(0, nblk, bias_body, 0)


def _dsa_prompt_kernel(q_ref, qi_ref, kiw_ref, kb_ref, vb_ref, kib_ref, o_ref,
                       keys_ref, bias_ref, ithr_ref, wb_ref, m_ref, l_ref, acc_ref):
    i = pl.program_id(1)
    nchunk = (i * TQ + TQ + TK - 1) // TK
    lane = lax.broadcasted_iota(I32, (TQ, LANES), 1)
    pos = i * TQ + lax.broadcasted_iota(I32, (TQ, LANES), 0)

    kiw = kiw_ref[0]
    for h in range(IDX_HEADS):
        w_h = kiw[:, IDX_DIM + h:IDX_DIM + h + 1] * IDX_HEAD_SCALE
        wb_ref[h] = jnp.broadcast_to(w_h, (TQ, LANES))

    def idx_body(c, carry):
        c0 = pl.multiple_of(c * TK, TK)
        kc = kib_ref[0, pl.ds(c0, TK), :]
        score = [jnp.zeros((TQ, LANES), F32) for _ in range(TK // LANES)]
        for h in range(IDX_HEADS):
            logit = _dot_nt(qi_ref[0, :, h * IDX_DIM:(h + 1) * IDX_DIM], kc)
            w_h = wb_ref[h]
            for j in range(TK // LANES):
                score[j] = score[j] + jnp.maximum(logit[:, j * LANES:(j + 1) * LANES], 0.0) * w_h
        for j in range(TK // LANES):
            col = c0 + j * LANES + lane
            keys_ref[:, pl.ds(pl.multiple_of(c0 + j * LANES, LANES), LANES)] = _score_keys(score[j], col <= pos)
        return carry

    lax.fori_loop(0, nchunk, idx_body, 0)

    _select_bias(keys_ref, bias_ref, ithr_ref, nchunk, TK, 12, jnp.full((TQ, 1), True))

    m_ref[...] = jnp.full(m_ref.shape, NEG_BIG, F32)
    l_ref[...] = jnp.zeros(l_ref.shape, F32)
    acc_ref[...] = jnp.zeros(acc_ref.shape, F32)

    def att_body(c, carry):
        c0 = pl.multiple_of(c * TK, TK)
        bias_c = bias_ref[:, pl.ds(c0, TK)]
        bias4 = jnp.concatenate([bias_c] * GROUP, axis=0)
        for g in range(N_KV_HEADS):
            qg = jnp.concatenate(
                [q_ref[0, :, (g * GROUP + r) * HEAD_DIM:(g * GROUP + r + 1) * HEAD_DIM] for r in range(GROUP)],
                axis=0)
            kc = kb_ref[0, pl.ds(c0, TK), g * HEAD_DIM:(g + 1) * HEAD_DIM]
            vc = vb_ref[0, pl.ds(c0, TK), g * HEAD_DIM:(g + 1) * HEAD_DIM]
            s = _dot_nt(qg, kc) + bias4
            m_old = m_ref[g]
            m_new = jnp.maximum(m_old, jnp.max(s, axis=1, keepdims=True))
            alpha = jnp.exp(m_old - m_new)
            p = jnp.exp(s - m_new)
            l_ref[g] = alpha * l_ref[g] + jnp.sum(p, axis=1, keepdims=True)
            acc_ref[g] = alpha * acc_ref[g] + _dot(p.astype(BF16), vc)
            m_ref[g] = m_new
        return carry

    lax.fori_loop(0, nchunk, att_body, 0)

    for g in range(N_KV_HEADS):
        og = acc_ref[g] * (1.0 / l_ref[g])
        for r in range(GROUP):
            col = (g * GROUP + r) * HEAD_DIM
            o_ref[0, :, col:col + HEAD_DIM] = og[r * TQ:(r + 1) * TQ].astype(BF16)


def _dsa_prompt_call(q, qi, kiw, kb, vb, kib):
    nb, s, _ = q.shape
    tile = lambda w: pl.BlockSpec((1, TQ, w), lambda b, i: (b, i, 0))
    full = lambda w: pl.BlockSpec((1, s, w), lambda b, i: (b, 0, 0))
    rows = GROUP * TQ
    return pl.pallas_call(
        _dsa_prompt_kernel,
        grid=(nb, s // TQ),
        in_specs=[tile(Q_W), tile(QI_W), tile(KIW_W), full(KV_W), full(KV_W), full(IDX_DIM)],
        out_specs=tile(Q_W),
        out_shape=jax.ShapeDtypeStruct((nb, s, Q_W), BF16),
        scratch_shapes=[
            pltpu.VMEM((TQ, s), I32),
            pltpu.VMEM((TQ, s), F32),
            pltpu.VMEM((TQ, LANES), I32),
            pltpu.VMEM((IDX_HEADS, TQ, LANES), F32),
            pltpu.VMEM((N_KV_HEADS, rows, 1), F32),
            pltpu.VMEM((N_KV_HEADS, rows, 1), F32),
            pltpu.VMEM((N_KV_HEADS, rows, HEAD_DIM), F32),
        ],
        compiler_params=_params(("arbitrary", "arbitrary")),
        name="dsa_prompt",
    )(q, qi, kiw, kb, vb, kib)


def _page_specs(width):
    return [pl.BlockSpec((1, PAGE_SIZE, width), functools.partial(lambda n, pt, p: (pt[n, p], 0, 0), p=p))
            for p in range(N_PAGES)]


def _pad_new_page(new8):
    return jnp.concatenate([new8, jnp.zeros((PAGE_SIZE - Q8, new8.shape[1]), F32)], axis=0).astype(BF16)


def _idx_sample_kernel(pt_ref, qis_ref, ws_ref, kin_ref, *rest):
    pages = rest[:N_PAGES]
    o_ref = rest[N_PAGES]
    del pt_ref
    qs = qis_ref[0]
    w = ws_ref[0]

    def page_score(kp):
        r = jnp.maximum(_dot_nt(qs, kp), 0.0) * w
        s = r[0:Q8]
        for h in range(1, IDX_HEADS):
            s = s + r[h * Q8:(h + 1) * Q8]
        return s

    for p in range(N_PAGES):
        o_ref[0, :, p * PAGE_SIZE:(p + 1) * PAGE_SIZE] = page_score(pages[p][0].astype(BF16))
    o_ref[0, :, PAST_LEN:L_SAMPLE] = page_score(_pad_new_page(kin_ref[0]))


def _idx_sample_call(page_table, qis, ws, kinew8, cache_ki):
    n = qis.shape[0]
    rows = IDX_HEADS * Q8
    grid_spec = pltpu.PrefetchScalarGridSpec(
        num_scalar_prefetch=1,
        grid=(n,),
        in_specs=[
            pl.BlockSpec((1, rows, IDX_DIM), lambda i, pt: (i, 0, 0)),
            pl.BlockSpec((1, rows, 1), lambda i, pt: (i, 0, 0)),
            pl.BlockSpec((1, Q8, IDX_DIM), lambda i, pt: (i, 0, 0)),
        ] + _page_specs(IDX_DIM),
        out_specs=pl.BlockSpec((1, Q8, L_SAMPLE), lambda i, pt: (i, 0, 0)),
    )
    return pl.pallas_call(
        _idx_sample_kernel,
        grid_spec=grid_spec,
        out_shape=jax.ShapeDtypeStruct((n, Q8, L_SAMPLE), F32),
        compiler_params=_params(("arbitrary",)),
        name="dsa_sample_scores",
    )(page_table, qis, ws, kinew8, *([cache_ki] * N_PAGES))


def _select_sample_kernel(s_ref, bias_ref, keys_ref, ithr_ref):
    row = lax.broadcasted_iota(I32, (SEL_ROWS, LANES), 0)
    lane = lax.broadcasted_iota(I32, (SEL_ROWS, LANES), 1)
    qrow = row % Q8
    real = qrow < T_NEW
    for j in range(L_SAMPLE // LANES):
        col = j * LANES + lane
        valid = (col < PAST_LEN) | ((col - PAST_LEN) <= qrow)
        valid = valid & real
        keys_ref[:, j * LANES:(j + 1) * LANES] = _score_keys(s_ref[:, j * LANES:(j + 1) * LANES], valid)
    row_ok = (lax.broadcasted_iota(I32, (SEL_ROWS, 1), 0) % Q8) < T_NEW
    _select_bias(keys_ref, bias_ref, ithr_ref, L_SAMPLE // LANES, LANES, 12, row_ok)


def _select_sample_call(scores):
    rows = scores.shape[0]
    return pl.pallas_call(
        _select_sample_kernel,
        grid=(rows // SEL_ROWS,),
        in_specs=[pl.BlockSpec((SEL_ROWS, L_SAMPLE), lambda i: (i, 0))],
        out_specs=pl.BlockSpec((SEL_ROWS, L_SAMPLE), lambda i: (i, 0)),
        out_shape=jax.ShapeDtypeStruct((rows, L_SAMPLE), F32),
        scratch_shapes=[pltpu.VMEM((SEL_ROWS, L_SAMPLE), I32), pltpu.VMEM((SEL_ROWS, LANES), I32)],
        compiler_params=_params(("arbitrary",)),
        name="dsa_sample_select",
    )(scores)


def _attn_sample_kernel(pt_ref, qb_ref, bias_ref, kn_ref, vn_ref, *rest):
    kpages = rest[:N_PAGES]
    vpages = rest[N_PAGES:2 * N_PAGES]
    o_ref = rest[2 * N_PAGES]
    s_ref = rest[2 * N_PAGES + 1]
    del pt_ref
    qb = qb_ref[0]
    reps = qb.shape[0] // Q8

    def page_logits(p, kp):
        bias8 = bias_ref[0, :, p * PAGE_SIZE:(p + 1) * PAGE_SIZE]
        s_ref[:, p * PAGE_SIZE:(p + 1) * PAGE_SIZE] = _dot_nt(qb, kp) + jnp.concatenate([bias8] * reps, axis=0)

    for p in range(N_PAGES):
        page_logits(p, kpages[p][0].astype(BF16))
    page_logits(N_PAGES, _pad_new_page(kn_ref[0]))

    s = s_ref[...]
    m = jnp.max(s, axis=1, keepdims=True)
    e = jnp.exp(s - m)
    l = jnp.sum(e, axis=1, keepdims=True)
    eb = e.astype(BF16)
    acc = _dot(eb[:, PAST_LEN:L_SAMPLE], _pad_new_page(vn_ref[0]))
    for p in range(N_PAGES):
        acc = acc + _dot(eb[:, p * PAGE_SIZE:(p + 1) * PAGE_SIZE], vpages[p][0].astype(BF16))
    o_ref[0] = acc * (1.0 / l)


def _attn_sample_call(page_table, qblk, bias, knew8, vnew8, cache_k, cache_v):
    n, rows, _ = qblk.shape
    grid_spec = pltpu.PrefetchScalarGridSpec(
        num_scalar_prefetch=1,
        grid=(n,),
        in_specs=[
            pl.BlockSpec((1, rows, KV_W), lambda i, pt: (i, 0, 0)),
            pl.BlockSpec((1, Q8, L_SAMPLE), lambda i, pt: (i, 0, 0)),
            pl.BlockSpec((1, Q8, KV_W), lambda i, pt: (i, 0, 0)),
            pl.BlockSpec((1, Q8, KV_W), lambda i, pt: (i, 0, 0)),
        ] + _page_specs(KV_W) + _page_specs(KV_W),
        out_specs=pl.BlockSpec((1, rows, KV_W), lambda i, pt: (i, 0, 0)),
        scratch_shapes=[pltpu.VMEM((rows, L_SAMPLE), F32)],
    )
    return pl.pallas_call(
        _attn_sample_kernel,
        grid_spec=grid_spec,
        out_shape=jax.ShapeDtypeStruct((n, rows, KV_W), F32),
        compiler_params=_params(("arbitrary",)),
        name="dsa_sample_attn",
    )(page_table, qblk, bias, knew8, vnew8, *([cache_k] * N_PAGES), *([cache_v] * N_PAGES))


def _mlp_resid(x1, g_ffn, shift, scale, gate, wup_ref, wdn_ref, acc_ref):
    h = _rms_mod(x1, g_ffn, shift, scale).astype(BF16)
    for c in range(D_FF // FF_CHUNK):
        u = _dot(h, wup_ref[:, c * FF_CHUNK:(c + 1) * FF_CHUNK])
        u = jnp.square(jnp.maximum(u, 0.0)).astype(BF16)
        d = _dot(u, wdn_ref[c * FF_CHUNK:(c + 1) * FF_CHUNK, :])
        if c == 0:
            acc_ref[...] = d
        else:
            acc_ref[...] += d
    return x1 + gate * acc_ref[...]


def _post0_kernel(x_ref, o_ref, g1_ref, sh2_ref, sc2_ref, g2_ref, gffn_ref, wo_ref, wup_ref, wdn_ref,
                  out_ref, acc_ref):
    x1 = x_ref[0] + g1_ref[0] * _dot(o_ref[0], wo_ref[...])
    out_ref[0] = _mlp_resid(x1, gffn_ref[...], sh2_ref[0], sc2_ref[0], g2_ref[0], wup_ref, wdn_ref, acc_ref)


def _const_spec(shape):
    nd = len(shape)
    return pl.BlockSpec(shape, lambda b, i: (0,) * nd)


def _post0_call(x, o, g1, sh2, sc2, g2, g_ffn, w_o, w_up, w_dn, tm):
    nb, t, _ = x.shape
    tok = lambda w: pl.BlockSpec((1, tm, w), lambda b, i: (b, i, 0))
    return pl.pallas_call(
        _post0_kernel,
        grid=(nb, t // tm),
        in_specs=[tok(D_MODEL), tok(Q_W), _mod_spec(g1, tm), _mod_spec(sh2, tm), _mod_spec(sc2, tm),
                  _mod_spec(g2, tm), _const_spec((1, D_MODEL)), _const_spec((Q_W, D_MODEL)),
                  _const_spec((D_MODEL, D_FF)), _const_spec((D_FF, D_MODEL))],
        out_specs=tok(D_MODEL),
        out_shape=jax.ShapeDtypeStruct((nb, t, D_MODEL), F32),
        scratch_shapes=[pltpu.VMEM((tm, D_MODEL), F32)],
        compiler_params=_params(("arbitrary", "arbitrary")),
        name="layer0_out",
    )(x, o, g1, sh2, sc2, g2, g_ffn.reshape(1, D_MODEL), w_o, w_up, w_dn)


def _pool_project(pooled, wp_ref, ps_ref):
    parts = []
    for g in range(len(POOL_WINDOWS)):
        lo = g * POOL_GROUP_DIM
        parts.append(_dot(pooled[:, lo:lo + POOL_GROUP_DIM].astype(BF16), wp_ref[g]))
    return jnp.concatenate(parts, axis=1) * ps_ref[...]


def _layer1_prompt_kernel(x_ref, xh_ref, sh1_ref, sc1_ref, g1_ref, sh2_ref, sc2_ref, g2_ref,
                          gmix_ref, gffn_ref, gfin_ref, wp_ref, ps_ref, wup_ref, wdn_ref,
                          y_ref, hlast_ref, acc_ref):
    i = pl.program_id(1)
    tm = x_ref.shape[1]
    x = x_ref[0]
    sh1, sc1 = sh1_ref[0], sc1_ref[0]
    h = _rms_mod(x, gmix_ref[...], sh1, sc1)
    h_halo = _rms_mod(xh_ref[0], gmix_ref[...], sh1, sc1)
    h_halo = jnp.where(i > 0, h_halo, 0.0)
    ext = jnp.concatenate([h_halo, h], axis=0)
    pos = i * tm + lax.broadcasted_iota(I32, (tm, 1), 0)
    parts = []
    for g, w in enumerate(POOL_WINDOWS):
        lo = g * POOL_GROUP_DIM
        s = ext[:, lo:lo + POOL_GROUP_DIM]
        step = 1
        while step < w:
            s = s + pltpu.roll(s, step, axis=0)
            step *= 2
        cnt = jnp.minimum(w, pos + 1).astype(F32)
        parts.append(s[HALO:, :] / cnt - h[:, lo:lo + POOL_GROUP_DIM])
    pooled = jnp.concatenate(parts, axis=1)
    x1 = x + g1_ref[0] * _pool_project(pooled, wp_ref, ps_ref)
    x2 = _mlp_resid(x1, gffn_ref[...], sh2_ref[0], sc2_ref[0], g2_ref[0], wup_ref, wdn_ref, acc_ref)
    y_ref[0] = _rmsnorm(x2, gfin_ref[...])

    @pl.when(i == pl.num_programs(1) - 1)
    def _():
        hlast_ref[0] = h[tm - HALO:, :]


def _layer1_prompt_call(x, mods, g_mix, g_ffn, g_fin, pool_w, pool_scale, w_up, w_dn, tm):
    nb, t, _ = x.shape
    sh1, sc1, g1, sh2, sc2, g2 = mods
    tok = pl.BlockSpec((1, tm, D_MODEL), lambda b, i: (b, i, 0))
    halo = pl.BlockSpec((1, HALO, D_MODEL), lambda b, i: (b, jnp.maximum(i * (tm // HALO) - 1, 0), 0))
    vec = _const_spec((1, D_MODEL))
    return pl.pallas_call(
        _layer1_prompt_kernel,
        grid=(nb, t // tm),
        in_specs=[tok, halo] + [_mod_spec(m, tm) for m in mods] + [vec, vec, vec,
                  _const_spec(pool_w.shape), vec, _const_spec((D_MODEL, D_FF)), _const_spec((D_FF, D_MODEL))],
        out_specs=[tok, pl.BlockSpec((1, HALO, D_MODEL), lambda b, i: (b, 0, 0))],
        out_shape=[jax.ShapeDtypeStruct((nb, t, D_MODEL), F32), jax.ShapeDtypeStruct((nb, HALO, D_MODEL), F32)],
        scratch_shapes=[pltpu.VMEM((tm, D_MODEL), F32)],
        compiler_params=_params(("arbitrary", "arbitrary")),
        name="layer1_prompt",
    )(x, x, sh1, sc1, g1, sh2, sc2, g2, g_mix.reshape(1, D_MODEL), g_ffn.reshape(1, D_MODEL),
      g_fin.reshape(1, D_MODEL), pool_w, pool_scale.reshape(1, D_MODEL), w_up, w_dn)


def _layer1_sample_kernel(x_ref, prev_ref, sh1_ref, sc1_ref, g1_ref, sh2_ref, sc2_ref, g2_ref,
                          gmix_ref, gffn_ref, gfin_ref, wp_ref, ps_ref, wup_ref, wdn_ref,
                          y_ref, h_ref, hs_ref, pooled_ref, acc_ref):
    tm = x_ref.shape[1]
    t_new = tm // prev_ref.shape[1]
    nseq = prev_ref.shape[1]
    x = x_ref[0]
    h = _rms_mod(x, gmix_ref[...], sh1_ref[0], sc1_ref[0])
    h_ref[0] = h
    ncol = D_MODEL // LANES
    for c in range(ncol):
        hs_ref[c] = h[:, c * LANES:(c + 1) * LANES]
    ext = [prev_ref[j] for j in range(POOL_STATE)]
    ext += [jnp.concatenate([hs_ref[c, pl.ds(t, nseq, stride=t_new), :] for c in range(ncol)], axis=1)
            for t in range(t_new)]
    for t in range(t_new):
        parts = []
        for g, w in enumerate(POOL_WINDOWS):
            lo = g * POOL_GROUP_DIM
            s = ext[POOL_STATE + t][:, lo:lo + POOL_GROUP_DIM]
            for j in range(1, w):
                s = s + ext[POOL_STATE + t - j][:, lo:lo + POOL_GROUP_DIM]
            parts.append(s / float(w) - ext[POOL_STATE + t][:, lo:lo + POOL_GROUP_DIM])
        pooled_t = jnp.concatenate(parts, axis=1)
        for c in range(ncol):
            pooled_ref[c, pl.ds(t, nseq, stride=t_new), :] = pooled_t[:, c * LANES:(c + 1) * LANES]
    pooled = jnp.concatenate([pooled_ref[c] for c in range(ncol)], axis=1)
    x1 = x + g1_ref[0] * _pool_project(pooled, wp_ref, ps_ref)
    x2 = _mlp_resid(x1, gffn_ref[...], sh2_ref[0], sc2_ref[0], g2_ref[0], wup_ref, wdn_ref, acc_ref)
    y_ref[0] = _rmsnorm(x2, gfin_ref[...])


def _layer1_sample_call(x, prev_t, mods, g_mix, g_ffn, g_fin, pool_w, pool_scale, w_up, w_dn, tm, t_new):
    nb, t, _ = x.shape
    tok = pl.BlockSpec((1, tm, D_MODEL), lambda b, i: (b, i, 0))
    prev = pl.BlockSpec((POOL_STATE, tm // t_new, D_MODEL), lambda b, i: (0, i, 0))
    vec = _const_spec((1, D_MODEL))
    return pl.pallas_call(
        _layer1_sample_kernel,
        grid=(nb, t // tm),
        in_specs=[tok, prev] + [_mod_spec(m, tm) for m in mods] + [vec, vec, vec,
                  _const_spec(pool_w.shape), vec, _const_spec((D_MODEL, D_FF)), _const_spec((D_FF, D_MODEL))],
        out_specs=[tok, tok],
        out_shape=[jax.ShapeDtypeStruct((nb, t, D_MODEL), F32), jax.ShapeDtypeStruct((nb, t, D_MODEL), F32)],
        scratch_shapes=[pltpu.VMEM((D_MODEL // LANES, tm, LANES), F32),
                        pltpu.VMEM((D_MODEL // LANES, tm, LANES), F32),
                        pltpu.VMEM((tm, D_MODEL), F32)],
        compiler_params=_params(("arbitrary", "arbitrary")),
        name="layer1_sample",
    )(x, prev_t, *mods, g_mix.reshape(1, D_MODEL), g_ffn.reshape(1, D_MODEL),
      g_fin.reshape(1, D_MODEL), pool_w, pool_scale.reshape(1, D_MODEL), w_up, w_dn)


def kernel(x_prompt, x_sample, cache_k, cache_v, cache_kidx, state_pool, page_table, c_prompt, c_sample,
           norm_mix_g, norm_ffn_g, w_mod, b_mod, dsa_w_in, dsa_w_o, pool_w, pool_scale, w_up, w_down,
           final_norm_g):
    nbp, seq, _ = x_prompt.shape
    nbs, t_new, _ = x_sample.shape
    n_tok_s = nbs * t_new

    w_in_b = jnp.pad(dsa_w_in[0], ((0, 0), (0, D_PROJ_PAD - D_PROJ))).astype(BF16)
    w_o_b = dsa_w_o[0].astype(BF16)
    w_up_b = w_up.astype(BF16)
    w_dn_b = w_down.astype(BF16)
    pool_w_b = pool_w[0].astype(BF16)

    n_c = nbp + nbs
    c_rows = -(-n_c // SUBLANES) * SUBLANES
    c_all = jnp.pad(jnp.concatenate([c_prompt, c_sample], axis=0), ((0, c_rows - n_c), (0, 0)))
    mod = _mod_call(c_all, w_mod, b_mod)

    def mods_prompt(layer):
        m = mod[layer, :nbp].reshape(nbp, 1, 6, D_MODEL)
        return [m[:, :, j] for j in range(6)]

    def mods_sample(layer):
        m = jnp.repeat(mod[layer, nbp:n_c].reshape(nbs, 6, D_MODEL), t_new, axis=0)
        return [m[:, j].reshape(1, n_tok_s, D_MODEL) for j in range(6)]

    sh1, sc1, g1, sh2, sc2, g2 = mods_prompt(0)
    k_p, v_p, ki_p, q, qi, kiw, kb, vb, kib = _proj_call(x_prompt, sh1, sc1, norm_mix_g[0], w_in_b, 512)
    o = _dsa_prompt_call(q, qi, kiw, kb, vb, kib)
    x1 = _post0_call(x_prompt, o, g1, sh2, sc2, g2, norm_ffn_g[0], w_o_b, w_up_b[0], w_dn_b[0], 512)
    y_prompt, hlast = _layer1_prompt_call(x1, mods_prompt(1), norm_mix_g[1], norm_ffn_g[1], final_norm_g,
                                          pool_w_b, pool_scale[0], w_up_b[1], w_dn_b[1], 512)
    pool_p = hlast[:, HALO - POOL_STATE:]

    sh1, sc1, g1, sh2, sc2, g2 = mods_sample(0)
    xs = x_sample.reshape(1, n_tok_s, D_MODEL)
    tms = 256
    k_s, v_s, ki_s, q, qi, kiw, _, _, _ = _proj_call(xs, sh1, sc1, norm_mix_g[0], w_in_b, tms)

    pad_q = lambda a: jnp.pad(a, [(0, 0)] * (a.ndim - 2) + [(0, Q8 - t_new), (0, 0)])
    qis = pad_q(qi.reshape(nbs, t_new, IDX_HEADS, IDX_DIM).transpose(0, 2, 1, 3)).reshape(nbs, IDX_HEADS * Q8, IDX_DIM)
    wi = kiw.reshape(nbs, t_new, KIW_W)[:, :, IDX_DIM:IDX_DIM + IDX_HEADS] * IDX_HEAD_SCALE
    ws = pad_q(wi.transpose(0, 2, 1)[..., None]).reshape(nbs, IDX_HEADS * Q8, 1)
    knew8 = pad_q(k_s.reshape(nbs, t_new, KV_W))
    vnew8 = pad_q(v_s.reshape(nbs, t_new, KV_W))
    kinew8 = pad_q(ki_s.reshape(nbs, t_new, IDX_DIM))
    n_pool = cache_k.shape[1]
    scores = _idx_sample_call(page_table, qis, ws, kinew8, cache_kidx[0].reshape(n_pool, PAGE_SIZE, IDX_DIM))
    bias = _select_sample_call(scores.reshape(nbs * Q8, L_SAMPLE)).reshape(nbs, Q8, L_SAMPLE)
    qg = pad_q(q.reshape(nbs, t_new, N_KV_HEADS, GROUP, HEAD_DIM).transpose(0, 2, 3, 1, 4))
    eye = jnp.eye(N_KV_HEADS, dtype=BF16)
    qblk = (qg[:, :, :, :, None, :] * eye[None, :, None, None, :, None]).reshape(nbs, N_KV_HEADS * GROUP * Q8, KV_W)
    o_blk = _attn_sample_call(page_table, qblk, bias, knew8, vnew8,
                              cache_k[0].reshape(n_pool, PAGE_SIZE, KV_W), cache_v[0].reshape(n_pool, PAGE_SIZE, KV_W))
    o_blk = o_blk.reshape(nbs, N_KV_HEADS, GROUP, Q8, N_KV_HEADS, HEAD_DIM)
    o_s = jnp.stack([o_blk[:, g, :, :t_new, g] for g in range(N_KV_HEADS)], axis=1)
    o_s = o_s.transpose(0, 3, 1, 2, 4).reshape(1, n_tok_s, Q_W).astype(BF16)
    x1s = _post0_call(xs, o_s, g1, sh2, sc2, g2, norm_ffn_g[0], w_o_b, w_up_b[0], w_dn_b[0], tms)
    prev_t = state_pool[0].transpose(1, 0, 2)
    y_s, h1s = _layer1_sample_call(x1s, prev_t, mods_sample(1), norm_mix_g[1], norm_ffn_g[1], final_norm_g,
                                   pool_w_b, pool_scale[0], w_up_b[1], w_dn_b[1], tms, t_new)
    y_sample = y_s.reshape(nbs, t_new, D_MODEL)
    pool_s = jnp.concatenate([state_pool[0][:, t_new:], h1s.reshape(nbs, t_new, D_MODEL)], axis=1)

    return (
        y_prompt, y_sample,
        k_p.reshape(1, nbp, seq, N_KV_HEADS, HEAD_DIM), v_p.reshape(1, nbp, seq, N_KV_HEADS, HEAD_DIM),
        ki_p.reshape(1, nbp, seq, IDX_DIM), pool_p[None],
        k_s.reshape(1, nbs, t_new, N_KV_HEADS, HEAD_DIM), v_s.reshape(1, nbs, t_new, N_KV_HEADS, HEAD_DIM),
        ki_s.reshape(1, nbs, t_new, IDX_DIM), pool_s[None],
    )
```

```python
import functools

import jax
import jax.numpy as jnp
from jax import lax
from jax.experimental import pallas as pl
from jax.experimental.pallas import tpu as pltpu

F32 = jnp.float32
BF16 = jnp.bfloat16
I32 = jnp.int32

D_MODEL = 1024
DEPTH = 2
PAST_LEN = 2048
PAGE_SIZE = 128
N_PAGES = PAST_LEN // PAGE_SIZE
N_HEADS = 8
HEAD_DIM = 128
N_KV_HEADS = 2
GROUP = N_HEADS // N_KV_HEADS
IDX_HEADS = 8
IDX_DIM = 64
TOPK = 256
POOL_WINDOWS = (2, 4, 8, 16)
POOL_GROUP_DIM = D_MODEL // len(POOL_WINDOWS)
POOL_STATE = max(POOL_WINDOWS) - 1
D_FF = 4 * D_MODEL
RMS_EPS = 1e-6
Q_W = N_HEADS * HEAD_DIM
KV_W = N_KV_HEADS * HEAD_DIM
QI_W = IDX_HEADS * IDX_DIM
D_PROJ = Q_W + 2 * KV_W + QI_W + IDX_DIM + IDX_HEADS

LANES = 128
SUBLANES = 8
VMEM_LIMIT_BYTES = 56 * 1024 * 1024

KIW_W = LANES
D_PROJ_PAD = Q_W + 2 * KV_W + QI_W + KIW_W
OFF_K = Q_W
OFF_V = OFF_K + KV_W
OFF_QI = OFF_V + KV_W
OFF_KIW = OFF_QI + QI_W

LOG2_E = 1.4426950408889634
ATT_SCALE = HEAD_DIM ** -0.5 * LOG2_E
IDX_SCALE = IDX_DIM ** -0.5
IDX_HEAD_SCALE = IDX_HEADS ** -0.5

INT_MIN = -(2 ** 31)
INT_MAX = 2 ** 31 - 1
NEG_BIG = -1e30

TQ = 128
TK = 512
FF_CHUNK = 512
HALO = 16
SEL_ROWS = 128
L_SAMPLE = PAST_LEN + PAGE_SIZE
T_NEW = 4
Q8 = SUBLANES


def _params(sem):
    return pltpu.CompilerParams(dimension_semantics=sem, vmem_limit_bytes=VMEM_LIMIT_BYTES)


def _dot(a, b):
    return jnp.dot(a, b, preferred_element_type=F32)


def _dot_nt(a, b):
    return lax.dot_general(a, b, (((1,), (1,)), ((), ())), preferred_element_type=F32)


def _rms_mod(x, g, shift, scale):
    ms = jnp.mean(x * x, axis=-1, keepdims=True)
    y = x * lax.rsqrt(ms + RMS_EPS) * g
    return y * (1.0 + scale) + shift


def _rmsnorm(x, g):
    ms = jnp.mean(x * x, axis=-1, keepdims=True)
    return x * lax.rsqrt(ms + RMS_EPS) * g


def _mod_kernel(c_ref, w_ref, b_ref, o_ref):
    c = c_ref[...]
    a = c * (1.0 / (1.0 + jnp.exp(-c)))
    w = w_ref[0]
    a_hi = a.astype(BF16)
    a_lo = (a - a_hi.astype(F32)).astype(BF16)
    w_hi = w.astype(BF16)
    w_lo = (w - w_hi.astype(F32)).astype(BF16)
    acc = _dot(a_hi, w_hi) + _dot(a_lo, w_hi) + _dot(a_hi, w_lo)
    o_ref[0] = acc + b_ref[0]


def _mod_call(c_all, w_mod, b_mod):
    rows = c_all.shape[0]
    tn = 1536
    return pl.pallas_call(
        _mod_kernel,
        grid=(DEPTH, 6 * D_MODEL // tn),
        in_specs=[
            pl.BlockSpec((rows, D_MODEL), lambda i, j: (0, 0)),
            pl.BlockSpec((1, D_MODEL, tn), lambda i, j: (i, 0, j)),
            pl.BlockSpec((1, 1, tn), lambda i, j: (i, 0, j)),
        ],
        out_specs=pl.BlockSpec((1, rows, tn), lambda i, j: (i, 0, j)),
        out_shape=jax.ShapeDtypeStruct((DEPTH, rows, 6 * D_MODEL), F32),
        compiler_params=_params(("arbitrary", "arbitrary")),
        name="mod_vectors",
    )(c_all, w_mod, b_mod.reshape(DEPTH, 1, 6 * D_MODEL))


def _proj_kernel(x_ref, sh_ref, sc_ref, g_ref, w_ref,
                 k_ref, v_ref, ki_ref, q_ref, qi_ref, kiw_ref, kb_ref, vb_ref, kib_ref):
    h = _rms_mod(x_ref[0], g_ref[...], sh_ref[0], sc_ref[0]).astype(BF16)
    q_ref[0] = (_dot(h, w_ref[:, 0:Q_W]) * ATT_SCALE).astype(BF16)
    k = _dot(h, w_ref[:, OFF_K:OFF_K + KV_W])
    v = _dot(h, w_ref[:, OFF_V:OFF_V + KV_W])
    for g in range(N_KV_HEADS):
        k_ref[0, :, g, :] = k[:, g * HEAD_DIM:(g + 1) * HEAD_DIM]
        v_ref[0, :, g, :] = v[:, g * HEAD_DIM:(g + 1) * HEAD_DIM]
    kb_ref[0] = k.astype(BF16)
    vb_ref[0] = v.astype(BF16)
    qi_ref[0] = (_dot(h, w_ref[:, OFF_QI:OFF_QI + QI_W]) * IDX_SCALE).astype(BF16)
    kiw = _dot(h, w_ref[:, OFF_KIW:OFF_KIW + KIW_W])
    kiw_ref[0] = kiw
    ki = kiw[:, 0:IDX_DIM]
    ki_ref[0] = ki
    kib_ref[0] = ki.astype(BF16)


def _mod_spec(arr, tm):
    if arr.shape[1] == 1:
        return pl.BlockSpec((1, 1, D_MODEL), lambda b, i: (b, 0, 0))
    return pl.BlockSpec((1, tm, D_MODEL), lambda b, i: (b, i, 0))


def _proj_call(x, shift, scale, g, w_in_b, tm):
    nb, t, _ = x.shape
    tok = lambda w: pl.BlockSpec((1, tm, w), lambda b, i: (b, i, 0))
    shp = lambda w, dt: jax.ShapeDtypeStruct((nb, t, w), dt)
    heads = pl.BlockSpec((1, tm, N_KV_HEADS, HEAD_DIM), lambda b, i: (b, i, 0, 0))
    heads_shape = jax.ShapeDtypeStruct((nb, t, N_KV_HEADS, HEAD_DIM), F32)
    return pl.pallas_call(
        _proj_kernel,
        grid=(nb, t // tm),
        in_specs=[
            tok(D_MODEL), _mod_spec(shift, tm), _mod_spec(scale, tm),
            pl.BlockSpec((1, D_MODEL), lambda b, i: (0, 0)),
            pl.BlockSpec((D_MODEL, D_PROJ_PAD), lambda b, i: (0, 0)),
        ],
        out_specs=[heads, heads, tok(IDX_DIM), tok(Q_W), tok(QI_W), tok(KIW_W),
                   tok(KV_W), tok(KV_W), tok(IDX_DIM)],
        out_shape=[heads_shape, heads_shape, shp(IDX_DIM, F32), shp(Q_W, BF16), shp(QI_W, BF16),
                   shp(KIW_W, F32), shp(KV_W, BF16), shp(KV_W, BF16), shp(IDX_DIM, BF16)],
        compiler_params=_params(("arbitrary", "arbitrary")),
        name="dsa_proj",
    )(x, shift, scale, g.reshape(1, D_MODEL), w_in_b)


def _score_keys(score, valid):
    b = lax.bitcast_convert_type(score, I32)
    key = jnp.where(b < 0, (b ^ INT_MAX) + 1, b)
    return jnp.where(valid, key, INT_MIN)


def _select_bias(keys_ref, bias_ref, ithr_ref, nblk, blk, idx_bits, row_ok):
    rows = keys_ref.shape[0]
    nsub = blk // LANES
    lane = lax.broadcasted_iota(I32, (rows, LANES), 1)

    def count(pred):
        def body(c, acc):
            for j in range(nsub):
                col0 = pl.multiple_of(c * blk + j * LANES, LANES)
                acc = acc + jnp.where(pred(keys_ref[:, pl.ds(col0, LANES)], col0), 1, 0)
            return acc
        acc = lax.fori_loop(0, nblk, body, jnp.zeros((rows, LANES), I32))
        return jnp.sum(acc, axis=1, keepdims=True)

    def bcast(v):
        return jnp.broadcast_to(v, (rows, LANES))

    n_nonneg = count(lambda kb, col0: kb >= 0)
    prefix = jnp.where(n_nonneg >= TOPK, 0, INT_MIN).astype(I32)

    def bit_body(t, prefix):
        cand = prefix | lax.shift_left(jnp.int32(1), 30 - t)
        cand_b = bcast(cand)
        n = count(lambda kb, col0: kb >= cand_b)
        return jnp.where(n >= TOPK, cand, prefix)

    thr = lax.fori_loop(0, 31, bit_body, prefix)
    thr_b = bcast(thr)
    n_gt = count(lambda kb, col0: kb > thr_b)
    n_ge = count(lambda kb, col0: kb >= thr_b)
    has_valid_thr = thr > INT_MIN
    tie = (n_ge > TOPK) & has_valid_thr & row_ok
    ithr_ref[...] = bcast(jnp.where(has_valid_thr, INT_MAX, -1).astype(I32))

    @pl.when(jnp.max(tie.astype(I32)) > 0)
    def _():
        need = TOPK - n_gt

        def idx_body(t, ithr):
            cand = ithr | lax.shift_left(jnp.int32(1), idx_bits - 1 - t)
            cand_b = bcast(cand)
            n = count(lambda kb, col0: (kb == thr_b) & ((col0 + lane) < cand_b))
            return jnp.where(n < need, cand, ithr)

        ithr = lax.fori_loop(0, idx_bits, idx_body, jnp.zeros((rows, 1), I32))
        ithr_ref[...] = jnp.where(bcast(tie), bcast(ithr), ithr_ref[...])

    ithr_b = ithr_ref[...]

    def bias_body(c, carry):
        for j in range(nsub):
            col0 = pl.multiple_of(c * blk + j * LANES, LANES)
            kb = keys_ref[:, pl.ds(col0, LANES)]
            sel = (kb > thr_b) | ((kb == thr_b) & ((col0 + lane) <= ithr_b))
            bias_ref[:, pl.ds(col0, LANES)] = jnp.where(sel, 0.0, NEG_BIG).astype(F32)
        return carry

    lax.fori_loop(0, nblk, bias_body, 0)


def _select_bias_t(keys_ref, bias_ref, ithr_ref, nblk):
    nsub = TK // SUBLANES
    nacc = 4
    row8 = lax.broadcasted_iota(I32, (SUBLANES, LANES), 0)

    def count(pred):
        def body(c, accs):
            accs = list(accs)
            for j in range(nsub):
                row0 = pl.multiple_of(c * TK + j * SUBLANES, SUBLANES)
                hit = pred(keys_ref[pl.ds(row0, SUBLANES), :], row0)
                accs[j % nacc] = accs[j % nacc] + jnp.where(hit, 1, 0)
            return tuple(accs)
        accs = lax.fori_loop(0, nblk, body, tuple(jnp.zeros((SUBLANES, LANES), I32) for _ in range(nacc)))
        tot = (accs[0] + accs[1]) + (accs[2] + accs[3])
        return jnp.sum(tot, axis=0, keepdims=True)

    def bcast(v):
        return jnp.broadcast_to(v, (SUBLANES, LANES))

    n_nonneg = count(lambda kb, row0: kb >= 0)
    prefix = jnp.where(n_nonneg >= TOPK, 0, INT_MIN).astype(I32)

    def bit_body(t, prefix):
        cand = prefix | lax.shift_left(jnp.int32(1), 30 - t)
        cand_b = bcast(cand)
        n = count(lambda kb, row0: kb >= cand_b)
        return jnp.where(n >= TOPK, cand, prefix)

    thr = lax.fori_loop(0, 31, bit_body, prefix)
    thr_b = bcast(thr)
    n_gt = count(lambda kb, row0: kb > thr_b)
    n_ge = count(lambda kb, row0: kb >= thr_b)
    has_valid_thr = thr > INT_MIN
    tie = (n_ge > TOPK) & has_valid_thr
    ithr_ref[...] = bcast(jnp.where(has_valid_thr, INT_MAX, -1).astype(I32))

    @pl.when(jnp.max(tie.astype(I32)) > 0)
    def _():
        need = TOPK - n_gt

        def idx_body(t, ithr):
            cand = ithr | lax.shift_left(jnp.int32(1), 11 - t)
            cand_b = bcast(cand)
            n = count(lambda kb, row0: (kb == thr_b) & ((row0 + row8) < cand_b))
            return jnp.where(n < need, cand, ithr)

        ithr = lax.fori_loop(0, 12, idx_body, jnp.zeros((1, LANES), I32))
        ithr_ref[...] = jnp.where(bcast(tie), bcast(ithr), ithr_ref[...])

    thr_q = jnp.broadcast_to(thr, (LANES, LANES))
    ithr_q = jnp.broadcast_to(ithr_ref[0:1, :], (LANES, LANES))
    rowq = lax.broadcasted_iota(I32, (LANES, LANES), 0)

    def bias_body(c, carry):
        for j in range(TK // LANES):
            row0 = pl.multiple_of(c * TK + j * LANES, LANES)
            kb = keys_ref[pl.ds(row0, LANES), :]
            sel = (kb > thr_q) | ((kb == thr_q) & ((row0 + rowq) <= ithr_q))
            bias_ref[:, pl.ds(row0, LANES)] = jnp.where(sel, 0.0, NEG_BIG).astype(F32).T
        return carry

    lax.fori_loop(0, nblk, bias_body, 0)


def _dsa_prompt_kernel(q_ref, qi_ref, kiw_ref, kb_ref, vb_ref, kib_ref, o_ref,
                       keys_ref, bias_ref, ithr_ref, qir_ref, s_ref, mlane_ref, llane_ref, acc_ref):
    i = pl.program_id(1)
    nchunk = (i * TQ + TQ + TK - 1) // TK
    qpos = i * TQ + lax.broadcasted_iota(I32, (TK, LANES), 1)
    krow = lax.broadcasted_iota(I32, (TK, LANES), 0)

    for h in range(IDX_HEADS):
        qir_ref[h * TQ:(h + 1) * TQ, :] = qi_ref[0, :, h * IDX_DIM:(h + 1) * IDX_DIM]
    kiw_t = kiw_ref[0].T
    w_rows = [kiw_t[IDX_DIM + h:IDX_DIM + h + 1, :] * IDX_HEAD_SCALE for h in range(IDX_HEADS)]

    def idx_body(c, carry):
        c0 = pl.multiple_of(c * TK, TK)
        logit = _dot_nt(kib_ref[0, pl.ds(c0, TK), :], qir_ref[...])
        score = jnp.zeros((TK, LANES), F32)
        for h in range(IDX_HEADS):
            score = score + jnp.maximum(logit[:, h * TQ:(h + 1) * TQ], 0.0) * w_rows[h]
        keys_ref[pl.ds(c0, TK), :] = _score_keys(score, (c0 + krow) <= qpos)
        return carry

    lax.fori_loop(0, nchunk, idx_body, 0)

    _select_bias_t(keys_ref, bias_ref, ithr_ref, nchunk)

    def q_group(g):
        return jnp.concatenate(
            [q_ref[0, :, (g * GROUP + r) * HEAD_DIM:(g * GROUP + r + 1) * HEAD_DIM] for r in range(GROUP)], axis=0)

    mlane_ref[...] = jnp.full(mlane_ref.shape, NEG_BIG, F32)

    def logits_body(c, carry):
        c0 = pl.multiple_of(c * TK, TK)
        bias4 = jnp.concatenate([bias_ref[:, pl.ds(c0, TK)]] * GROUP, axis=0)
        for g in range(N_KV_HEADS):
            s = _dot_nt(q_group(g), kb_ref[0, pl.ds(c0, TK), g * HEAD_DIM:(g + 1) * HEAD_DIM]) + bias4
            s_ref[g, :, pl.ds(c0, TK)] = s
            m = mlane_ref[g]
            for j in range(TK // LANES):
                m = jnp.maximum(m, s[:, j * LANES:(j + 1) * LANES])
            mlane_ref[g] = m
        return carry

    lax.fori_loop(0, nchunk, logits_body, 0)

    for g in range(N_KV_HEADS):
        mlane_ref[g] = jnp.broadcast_to(jnp.max(mlane_ref[g], axis=1, keepdims=True), mlane_ref.shape[1:])
    llane_ref[...] = jnp.zeros(llane_ref.shape, F32)
    acc_ref[...] = jnp.zeros(acc_ref.shape, F32)

    def pv_body(c, carry):
        c0 = pl.multiple_of(c * TK, TK)
        for g in range(N_KV_HEADS):
            m = mlane_ref[g]
            l = llane_ref[g]
            ps = []
            for j in range(TK // LANES):
                pj = jnp.exp2(s_ref[g, :, pl.ds(pl.multiple_of(c0 + j * LANES, LANES), LANES)] - m)
                l = l + pj
                ps.append(pj.astype(BF16))
            llane_ref[g] = l
            p = jnp.concatenate(ps, axis=1)
            acc_ref[g] += _dot(p, vb_ref[0, pl.ds(c0, TK), g * HEAD_DIM:(g + 1) * HEAD_DIM])
        return carry

    lax.fori_loop(0, nchunk, pv_body, 0)

    for g in range(N_KV_HEADS):
        og = acc_ref[g] * (1.0 / jnp.sum(llane_ref[g], axis=1, keepdims=True))
        for r in range(GROUP):
            col = (g * GROUP + r) * HEAD_DIM
            o_ref[0, :, col:col + HEAD_DIM] = og[r * TQ:(r + 1) * TQ].astype(BF16)


def _dsa_prompt_call(q, qi, kiw, kb, vb, kib):
    nb, s, _ = q.shape
    tile = lambda w: pl.BlockSpec((1, TQ, w), lambda b, i: (b, i, 0))
    full = lambda w: pl.BlockSpec((1, s, w), lambda b, i: (b, 0, 0))
    rows = GROUP * TQ
    return pl.pallas_call(
        _dsa_prompt_kernel,
        grid=(nb, s // TQ),
        in_specs=[tile(Q_W), tile(QI_W), tile(KIW_W), full(KV_W), full(KV_W), full(IDX_DIM)],
        out_specs=tile(Q_W),
        out_shape=jax.ShapeDtypeStruct((nb, s, Q_W), BF16),
        scratch_shapes=[
            pltpu.VMEM((s, TQ), I32),
            pltpu.VMEM((TQ, s), F32),
            pltpu.VMEM((SUBLANES, LANES), I32),
            pltpu.VMEM((IDX_HEADS * TQ, IDX_DIM), BF16),
            pltpu.VMEM((N_KV_HEADS, rows, s), F32),
            pltpu.VMEM((N_KV_HEADS, rows, LANES), F32),
            pltpu.VMEM((N_KV_HEADS, rows, LANES), F32),
            pltpu.VMEM((N_KV_HEADS, rows, HEAD_DIM), F32),
        ],
        compiler_params=_params(("arbitrary", "arbitrary")),
        name="dsa_prompt",
    )(q, qi, kiw, kb, vb, kib)


def _page_specs(page_shape):
    zeros = (0,) * len(page_shape)
    return [pl.BlockSpec((1,) + page_shape, functools.partial(lambda n, pt, p: (pt[n, p],) + zeros, p=p))
            for p in range(N_PAGES)]


def _kv_page(page_ref):
    return jnp.concatenate([page_ref[0, :, g, :] for g in range(N_KV_HEADS)], axis=1).astype(BF16)


def _pad_new_page(new8):
    return jnp.concatenate([new8, jnp.zeros((PAGE_SIZE - Q8, new8.shape[1]), F32)], axis=0).astype(BF16)


def _idx_sample_kernel(pt_ref, qis_ref, ws_ref, kin_ref, *rest):
    pages = rest[:N_PAGES]
    o_ref = rest[N_PAGES]
    del pt_ref
    qs = qis_ref[0]
    w = ws_ref[0]

    def page_score(kt):
        r = jnp.maximum(_dot(qs, kt.astype(BF16)), 0.0) * w
        s = r[0:Q8]
        for h in range(1, IDX_HEADS):
            s = s + r[h * Q8:(h + 1) * Q8]
        return s

    for p in range(N_PAGES):
        o_ref[0, :, p * PAGE_SIZE:(p + 1) * PAGE_SIZE] = page_score(pages[p][0])
    o_ref[0, :, PAST_LEN:L_SAMPLE] = page_score(kin_ref[0])


def _idx_sample_call(page_table, qis, ws, kinew_t, cache_ki_t):
    n = qis.shape[0]
    rows = IDX_HEADS * Q8
    grid_spec = pltpu.PrefetchScalarGridSpec(
        num_scalar_prefetch=1,
        grid=(n,),
        in_specs=[
            pl.BlockSpec((1, rows, IDX_DIM), lambda i, pt: (i, 0, 0)),
            pl.BlockSpec((1, rows, 1), lambda i, pt: (i, 0, 0)),
            pl.BlockSpec((1, IDX_DIM, PAGE_SIZE), lambda i, pt: (i, 0, 0)),
        ] + _page_specs((IDX_DIM, PAGE_SIZE)),
        out_specs=pl.BlockSpec((1, Q8, L_SAMPLE), lambda i, pt: (i, 0, 0)),
    )
    return pl.pallas_call(
        _idx_sample_kernel,
        grid_spec=grid_spec,
        out_shape=jax.ShapeDtypeStruct((n, Q8, L_SAMPLE), F32),
        compiler_params=_params(("arbitrary",)),
        name="dsa_sample_scores",
    )(page_table, qis, ws, kinew_t, *([cache_ki_t] * N_PAGES))


def _select_sample_kernel(s_ref, bias_ref, keys_ref, ithr_ref):
    row = lax.broadcasted_iota(I32, (SEL_ROWS, LANES), 0)
    lane = lax.broadcasted_iota(I32, (SEL_ROWS, LANES), 1)
    qrow = row % Q8
    real = qrow < T_NEW
    for j in range(L_SAMPLE // LANES):
        col = j * LANES + lane
        valid = (col < PAST_LEN) | ((col - PAST_LEN) <= qrow)
        valid = valid & real
        keys_ref[:, j * LANES:(j + 1) * LANES] = _score_keys(s_ref[:, j * LANES:(j + 1) * LANES], valid)
    row_ok = (lax.broadcasted_iota(I32, (SEL_ROWS, 1), 0) % Q8) < T_NEW
    _select_bias(keys_ref, bias_ref, ithr_ref, L_SAMPLE // LANES, LANES, 12, row_ok)


def _select_sample_call(scores):
    rows = scores.shape[0]
    return pl.pallas_call(
        _select_sample_kernel,
        grid=(rows // SEL_ROWS,),
        in_specs=[pl.BlockSpec((SEL_ROWS, L_SAMPLE), lambda i: (i, 0))],
        out_specs=pl.BlockSpec((SEL_ROWS, L_SAMPLE), lambda i: (i, 0)),
        out_shape=jax.ShapeDtypeStruct((rows, L_SAMPLE), F32),
        scratch_shapes=[pltpu.VMEM((SEL_ROWS, L_SAMPLE), I32), pltpu.VMEM((SEL_ROWS, LANES), I32)],
        compiler_params=_params(("arbitrary",)),
        name="dsa_sample_select",
    )(scores)


def _attn_sample_kernel(pt_ref, qb_ref, bias_ref, kn_ref, vn_ref, *rest):
    kpages = rest[:N_PAGES]
    vpages = rest[N_PAGES:2 * N_PAGES]
    o_ref = rest[2 * N_PAGES]
    s_ref = rest[2 * N_PAGES + 1]
    del pt_ref
    qb = qb_ref[0]
    reps = qb.shape[0] // Q8

    def page_logits(p, kp):
        bias8 = bias_ref[0, :, p * PAGE_SIZE:(p + 1) * PAGE_SIZE]
        s_ref[:, p * PAGE_SIZE:(p + 1) * PAGE_SIZE] = _dot_nt(qb, kp) + jnp.concatenate([bias8] * reps, axis=0)

    for p in range(N_PAGES):
        page_logits(p, _kv_page(kpages[p]))
    page_logits(N_PAGES, _pad_new_page(kn_ref[0]))

    s = s_ref[...]
    m = jnp.max(s, axis=1, keepdims=True)
    e = jnp.exp2(s - m)
    l = jnp.sum(e, axis=1, keepdims=True)
    eb = e.astype(BF16)
    acc = _dot(eb[:, PAST_LEN:L_SAMPLE], _pad_new_page(vn_ref[0]))
    for p in range(N_PAGES):
        acc = acc + _dot(eb[:, p * PAGE_SIZE:(p + 1) * PAGE_SIZE], _kv_page(vpages[p]))
    o_ref[0] = acc * (1.0 / l)


def _attn_sample_call(page_table, qblk, bias, knew8, vnew8, cache_k, cache_v):
    n, rows, _ = qblk.shape
    grid_spec = pltpu.PrefetchScalarGridSpec(
        num_scalar_prefetch=1,
        grid=(n,),
        in_specs=[
            pl.BlockSpec((1, rows, KV_W), lambda i, pt: (i, 0, 0)),
            pl.BlockSpec((1, Q8, L_SAMPLE), lambda i, pt: (i, 0, 0)),
            pl.BlockSpec((1, Q8, KV_W), lambda i, pt: (i, 0, 0)),
            pl.BlockSpec((1, Q8, KV_W), lambda i, pt: (i, 0, 0)),
        ] + _page_specs((PAGE_SIZE, N_KV_HEADS, HEAD_DIM)) + _page_specs((PAGE_SIZE, N_KV_HEADS, HEAD_DIM)),
        out_specs=pl.BlockSpec((1, rows, KV_W), lambda i, pt: (i, 0, 0)),
        scratch_shapes=[pltpu.VMEM((rows, L_SAMPLE), F32)],
    )
    return pl.pallas_call(
        _attn_sample_kernel,
        grid_spec=grid_spec,
        out_shape=jax.ShapeDtypeStruct((n, rows, KV_W), F32),
        compiler_params=_params(("arbitrary",)),
        name="dsa_sample_attn",
    )(page_table, qblk, bias, knew8, vnew8, *([cache_k] * N_PAGES), *([cache_v] * N_PAGES))


def _mlp_resid(x1, g_ffn, shift, scale, gate, wup_ref, wdn_ref, acc_ref):
    h = _rms_mod(x1, g_ffn, shift, scale).astype(BF16)
    for c in range(D_FF // FF_CHUNK):
        u = _dot(h, wup_ref[:, c * FF_CHUNK:(c + 1) * FF_CHUNK])
        u = jnp.square(jnp.maximum(u, 0.0)).astype(BF16)
        d = _dot(u, wdn_ref[c * FF_CHUNK:(c + 1) * FF_CHUNK, :])
        if c == 0:
            acc_ref[...] = d
        else:
            acc_ref[...] += d
    return x1 + gate * acc_ref[...]


def _post0_kernel(x_ref, o_ref, g1_ref, sh2_ref, sc2_ref, g2_ref, gffn_ref, wo_ref, wup_ref, wdn_ref,
                  out_ref, acc_ref):
    x1 = x_ref[0] + g1_ref[0] * _dot(o_ref[0], wo_ref[...])
    out_ref[0] = _mlp_resid(x1, gffn_ref[...], sh2_ref[0], sc2_ref[0], g2_ref[0], wup_ref, wdn_ref, acc_ref)


def _const_spec(shape):
    nd = len(shape)
    return pl.BlockSpec(shape, lambda b, i: (0,) * nd)


def _post0_call(x, o, g1, sh2, sc2, g2, g_ffn, w_o, w_up, w_dn, tm):
    nb, t, _ = x.shape
    tok = lambda w: pl.BlockSpec((1, tm, w), lambda b, i: (b, i, 0))
    return pl.pallas_call(
        _post0_kernel,
        grid=(nb, t // tm),
        in_specs=[tok(D_MODEL), tok(Q_W), _mod_spec(g1, tm), _mod_spec(sh2, tm), _mod_spec(sc2, tm),
                  _mod_spec(g2, tm), _const_spec((1, D_MODEL)), _const_spec((Q_W, D_MODEL)),
                  _const_spec((D_MODEL, D_FF)), _const_spec((D_FF, D_MODEL))],
        out_specs=tok(D_MODEL),
        out_shape=jax.ShapeDtypeStruct((nb, t, D_MODEL), F32),
        scratch_shapes=[pltpu.VMEM((tm, D_MODEL), F32)],
        compiler_params=_params(("arbitrary", "arbitrary")),
        name="layer0_out",
    )(x, o, g1, sh2, sc2, g2, g_ffn.reshape(1, D_MODEL), w_o, w_up, w_dn)


def _pool_project(pooled, wp_ref, ps_ref):
    parts = []
    for g in range(len(POOL_WINDOWS)):
        lo = g * POOL_GROUP_DIM
        parts.append(_dot(pooled[:, lo:lo + POOL_GROUP_DIM].astype(BF16), wp_ref[g]))
    return jnp.concatenate(parts, axis=1) * ps_ref[...]


def _layer1_prompt_kernel(x_ref, xh_ref, sh1_ref, sc1_ref, g1_ref, sh2_ref, sc2_ref, g2_ref,
                          gmix_ref, gffn_ref, gfin_ref, wp_ref, ps_ref, wup_ref, wdn_ref,
                          y_ref, hlast_ref, acc_ref):
    i = pl.program_id(1)
    tm = x_ref.shape[1]
    x = x_ref[0]
    sh1, sc1 = sh1_ref[0], sc1_ref[0]
    h = _rms_mod(x, gmix_ref[...], sh1, sc1)
    h_halo = _rms_mod(xh_ref[0], gmix_ref[...], sh1, sc1)
    h_halo = jnp.where(i > 0, h_halo, 0.0)
    ext = jnp.concatenate([h_halo, h], axis=0)
    pos = i * tm + lax.broadcasted_iota(I32, (tm, 1), 0)
    parts = []
    for g, w in enumerate(POOL_WINDOWS):
        lo = g * POOL_GROUP_DIM
        s = ext[:, lo:lo + POOL_GROUP_DIM]
        step = 1
        while step < w:
            s = s + pltpu.roll(s, step, axis=0)
            step *= 2
        cnt = jnp.minimum(w, pos + 1).astype(F32)
        parts.append(s[HALO:, :] / cnt - h[:, lo:lo + POOL_GROUP_DIM])
    pooled = jnp.concatenate(parts, axis=1)
    x1 = x + g1_ref[0] * _pool_project(pooled, wp_ref, ps_ref)
    x2 = _mlp_resid(x1, gffn_ref[...], sh2_ref[0], sc2_ref[0], g2_ref[0], wup_ref, wdn_ref, acc_ref)
    y_ref[0] = _rmsnorm(x2, gfin_ref[...])

    @pl.when(i == pl.num_programs(1) - 1)
    def _():
        hlast_ref[0] = h[tm - HALO:, :]


def _layer1_prompt_call(x, mods, g_mix, g_ffn, g_fin, pool_w, pool_scale, w_up, w_dn, tm):
    nb, t, _ = x.shape
    sh1, sc1, g1, sh2, sc2, g2 = mods
    tok = pl.BlockSpec((1, tm, D_MODEL), lambda b, i: (b, i, 0))
    halo = pl.BlockSpec((1, HALO, D_MODEL), lambda b, i: (b, jnp.maximum(i * (tm // HALO) - 1, 0), 0))
    vec = _const_spec((1, D_MODEL))
    return pl.pallas_call(
        _layer1_prompt_kernel,
        grid=(nb, t // tm),
        in_specs=[tok, halo] + [_mod_spec(m, tm) for m in mods] + [vec, vec, vec,
                  _const_spec(pool_w.shape), vec, _const_spec((D_MODEL, D_FF)), _const_spec((D_FF, D_MODEL))],
        out_specs=[tok, pl.BlockSpec((1, HALO, D_MODEL), lambda b, i: (b, 0, 0))],
        out_shape=[jax.ShapeDtypeStruct((nb, t, D_MODEL), F32), jax.ShapeDtypeStruct((nb, HALO, D_MODEL), F32)],
        scratch_shapes=[pltpu.VMEM((tm, D_MODEL), F32)],
        compiler_params=_params(("arbitrary", "arbitrary")),
        name="layer1_prompt",
    )(x, x, sh1, sc1, g1, sh2, sc2, g2, g_mix.reshape(1, D_MODEL), g_ffn.reshape(1, D_MODEL),
      g_fin.reshape(1, D_MODEL), pool_w, pool_scale.reshape(1, D_MODEL), w_up, w_dn)


def _layer1_sample_kernel(x_ref, prev_ref, sh1_ref, sc1_ref, g1_ref, sh2_ref, sc2_ref, g2_ref,
                          gmix_ref, gffn_ref, gfin_ref, wp_ref, ps_ref, wup_ref, wdn_ref,
                          y_ref, h_ref, hs_ref, pooled_ref, acc_ref):
    tm = x_ref.shape[1]
    t_new = tm // prev_ref.shape[1]
    nseq = prev_ref.shape[1]
    x = x_ref[0]
    h = _rms_mod(x, gmix_ref[...], sh1_ref[0], sc1_ref[0])
    h_ref[0] = h
    ncol = D_MODEL // LANES
    for c in range(ncol):
        hs_ref[c] = h[:, c * LANES:(c + 1) * LANES]
    ext = [prev_ref[j] for j in range(POOL_STATE)]
    ext += [jnp.concatenate([hs_ref[c, pl.ds(t, nseq, stride=t_new), :] for c in range(ncol)], axis=1)
            for t in range(t_new)]
    for t in range(t_new):
        parts = []
        for g, w in enumerate(POOL_WINDOWS):
            lo = g * POOL_GROUP_DIM
            s = ext[POOL_STATE + t][:, lo:lo + POOL_GROUP_DIM]
            for j in range(1, w):
                s = s + ext[POOL_STATE + t - j][:, lo:lo + POOL_GROUP_DIM]
            parts.append(s / float(w) - ext[POOL_STATE + t][:, lo:lo + POOL_GROUP_DIM])
        pooled_t = jnp.concatenate(parts, axis=1)
        for c in range(ncol):
            pooled_ref[c, pl.ds(t, nseq, stride=t_new), :] = pooled_t[:, c * LANES:(c + 1) * LANES]
    pooled = jnp.concatenate([pooled_ref[c] for c in range(ncol)], axis=1)
    x1 = x + g1_ref[0] * _pool_project(pooled, wp_ref, ps_ref)
    x2 = _mlp_resid(x1, gffn_ref[...], sh2_ref[0], sc2_ref[0], g2_ref[0], wup_ref, wdn_ref, acc_ref)
    y_ref[0] = _rmsnorm(x2, gfin_ref[...])


def _layer1_sample_call(x, prev_t, mods, g_mix, g_ffn, g_fin, pool_w, pool_scale, w_up, w_dn, tm, t_new):
    nb, t, _ = x.shape
    tok = pl.BlockSpec((1, tm, D_MODEL), lambda b, i: (b, i, 0))
    prev = pl.BlockSpec((POOL_STATE, tm // t_new, D_MODEL), lambda b, i: (0, i, 0))
    vec = _const_spec((1, D_MODEL))
    return pl.pallas_call(
        _layer1_sample_kernel,
        grid=(nb, t // tm),
        in_specs=[tok, prev] + [_mod_spec(m, tm) for m in mods] + [vec, vec, vec,
                  _const_spec(pool_w.shape), vec, _const_spec((D_MODEL, D_FF)), _const_spec((D_FF, D_MODEL))],
        out_specs=[tok, tok],
        out_shape=[jax.ShapeDtypeStruct((nb, t, D_MODEL), F32), jax.ShapeDtypeStruct((nb, t, D_MODEL), F32)],
        scratch_shapes=[pltpu.VMEM((D_MODEL // LANES, tm, LANES), F32),
                        pltpu.VMEM((D_MODEL // LANES, tm, LANES), F32),
                        pltpu.VMEM((tm, D_MODEL), F32)],
        compiler_params=_params(("arbitrary", "arbitrary")),
        name="layer1_sample",
    )(x, prev_t, *mods, g_mix.reshape(1, D_MODEL), g_ffn.reshape(1, D_MODEL),
      g_fin.reshape(1, D_MODEL), pool_w, pool_scale.reshape(1, D_MODEL), w_up, w_dn)


def kernel(x_prompt, x_sample, cache_k, cache_v, cache_kidx, state_pool, page_table, c_prompt, c_sample,
           norm_mix_g, norm_ffn_g, w_mod, b_mod, dsa_w_in, dsa_w_o, pool_w, pool_scale, w_up, w_down,
           final_norm_g):
    nbp, seq, _ = x_prompt.shape
    nbs, t_new, _ = x_sample.shape
    n_tok_s = nbs * t_new

    w_in_b = jnp.pad(dsa_w_in[0], ((0, 0), (0, D_PROJ_PAD - D_PROJ))).astype(BF16)
    w_o_b = dsa_w_o[0].astype(BF16)
    w_up_b = w_up.astype(BF16)
    w_dn_b = w_down.astype(BF16)
    pool_w_b = pool_w[0].astype(BF16)

    n_c = nbp + nbs
    c_rows = -(-n_c // SUBLANES) * SUBLANES
    c_all = jnp.pad(jnp.concatenate([c_prompt, c_sample], axis=0), ((0, c_rows - n_c), (0, 0)))
    mod = _mod_call(c_all, w_mod, b_mod)

    def mods_prompt(layer):
        m = mod[layer, :nbp].reshape(nbp, 1, 6, D_MODEL)
        return [m[:, :, j] for j in range(6)]

    def mods_sample(layer):
        m = jnp.repeat(mod[layer, nbp:n_c].reshape(nbs, 6, D_MODEL), t_new, axis=0)
        return [m[:, j].reshape(1, n_tok_s, D_MODEL) for j in range(6)]

    sh1, sc1, g1, sh2, sc2, g2 = mods_prompt(0)
    k_p, v_p, ki_p, q, qi, kiw, kb, vb, kib = _proj_call(x_prompt, sh1, sc1, norm_mix_g[0], w_in_b, 512)
    o = _dsa_prompt_call(q, qi, kiw, kb, vb, kib)
    x1 = _post0_call(x_prompt, o, g1, sh2, sc2, g2, norm_ffn_g[0], w_o_b, w_up_b[0], w_dn_b[0], 512)
    y_prompt, hlast = _layer1_prompt_call(x1, mods_prompt(1), norm_mix_g[1], norm_ffn_g[1], final_norm_g,
                                          pool_w_b, pool_scale[0], w_up_b[1], w_dn_b[1], 512)
    pool_p = hlast[:, HALO - POOL_STATE:]

    sh1, sc1, g1, sh2, sc2, g2 = mods_sample(0)
    xs = x_sample.reshape(1, n_tok_s, D_MODEL)
    tms = 256
    k_s, v_s, ki_s, q, qi, kiw, _, _, _ = _proj_call(xs, sh1, sc1, norm_mix_g[0], w_in_b, tms)

    pad_q = lambda a: jnp.pad(a, [(0, 0)] * (a.ndim - 2) + [(0, Q8 - t_new), (0, 0)])
    qis = pad_q(qi.reshape(nbs, t_new, IDX_HEADS, IDX_DIM).transpose(0, 2, 1, 3)).reshape(nbs, IDX_HEADS * Q8, IDX_DIM)
    wi = kiw.reshape(nbs, t_new, KIW_W)[:, :, IDX_DIM:IDX_DIM + IDX_HEADS] * IDX_HEAD_SCALE
    ws = pad_q(wi.transpose(0, 2, 1)[..., None]).reshape(nbs, IDX_HEADS * Q8, 1)
    knew8 = pad_q(k_s.reshape(nbs, t_new, KV_W))
    vnew8 = pad_q(v_s.reshape(nbs, t_new, KV_W))
    kinew_t = jnp.pad(ki_s.reshape(nbs, t_new, IDX_DIM).transpose(0, 2, 1), ((0, 0), (0, 0), (0, PAGE_SIZE - t_new)))
    scores = _idx_sample_call(page_table, qis, ws, kinew_t, cache_kidx[0].transpose(0, 2, 1))
    bias = _select_sample_call(scores.reshape(nbs * Q8, L_SAMPLE)).reshape(nbs, Q8, L_SAMPLE)
    qg = pad_q(q.reshape(nbs, t_new, N_KV_HEADS, GROUP, HEAD_DIM).transpose(0, 2, 3, 1, 4))
    eye = jnp.eye(N_KV_HEADS, dtype=BF16)
    qblk = (qg[:, :, :, :, None, :] * eye[None, :, None, None, :, None]).reshape(nbs, N_KV_HEADS * GROUP * Q8, KV_W)
    o_blk = _attn_sample_call(page_table, qblk, bias, knew8, vnew8, cache_k[0], cache_v[0])
    o_blk = o_blk.reshape(nbs, N_KV_HEADS, GROUP, Q8, N_KV_HEADS, HEAD_DIM)
    o_s = jnp.stack([o_blk[:, g, :, :t_new, g] for g in range(N_KV_HEADS)], axis=1)
    o_s = o_s.transpose(0, 3, 1, 2, 4).reshape(1, n_tok_s, Q_W).astype(BF16)
    x1s = _post0_call(xs, o_s, g1, sh2, sc2, g2, norm_ffn_g[0], w_o_b, w_up_b[0], w_dn_b[0], tms)
    prev_t = state_pool[0].transpose(1, 0, 2)
    y_s, h1s = _layer1_sample_call(x1s, prev_t, mods_sample(1), norm_mix_g[1], norm_ffn_g[1], final_norm_g,
                                   pool_w_b, pool_scale[0], w_up_b[1], w_dn_b[1], tms, t_new)
    y_sample = y_s.reshape(nbs, t_new, D_MODEL)
    pool_s = jnp.concatenate([state_pool[0][:, t_new:], h1s.reshape(nbs, t_new, D_MODEL)], axis=1)

    return (
        y_prompt, y_sample,
        k_p[None], v_p[None], ki_p[None], pool_p[None],
        k_s.reshape(1, nbs, t_new, N_KV_HEADS, HEAD_DIM), v_s.reshape(1, nbs, t_new, N_KV_HEADS, HEAD_DIM),
        ki_s.reshape(1, nbs, t_new, IDX_DIM), pool_s[None],
    )
```

```python
import functools
from typing import Callable, NamedTuple

import jax
import jax.numpy as jnp
from jax import lax
from jax.experimental import pallas as pl
from jax.experimental.pallas import tpu as pltpu

F32 = jnp.float32
BF16 = jnp.bfloat16
I32 = jnp.int32

D_MODEL = 1024
DEPTH = 2
PAST_LEN = 2048
PAGE_SIZE = 128
N_PAGES = PAST_LEN // PAGE_SIZE
N_HEADS = 8
HEAD_DIM = 128
N_KV_HEADS = 2
GROUP = N_HEADS // N_KV_HEADS
IDX_HEADS = 8
IDX_DIM = 64
TOPK = 256
POOL_WINDOWS = (2, 4, 8, 16)
POOL_GROUP_DIM = D_MODEL // len(POOL_WINDOWS)
POOL_STATE = max(POOL_WINDOWS) - 1
D_FF = 4 * D_MODEL
RMS_EPS = 1e-6
Q_W = N_HEADS * HEAD_DIM
KV_W = N_KV_HEADS * HEAD_DIM
QI_W = IDX_HEADS * IDX_DIM
D_PROJ = Q_W + 2 * KV_W + QI_W + IDX_DIM + IDX_HEADS

LANES = 128
SUBLANES = 8
VMEM_LIMIT_BYTES = 56 * 1024 * 1024

KIW_W = LANES
D_PROJ_PAD = Q_W + 2 * KV_W + QI_W + KIW_W
OFF_K = Q_W
OFF_V = OFF_K + KV_W
OFF_QI = OFF_V + KV_W
OFF_KIW = OFF_QI + QI_W

LOG2_E = 1.4426950408889634
ATT_SCALE = HEAD_DIM ** -0.5 * LOG2_E
IDX_SCALE = IDX_DIM ** -0.5
IDX_HEAD_SCALE = IDX_HEADS ** -0.5

INT_MIN = -(2 ** 31)
INT_MAX = 2 ** 31 - 1
NEG_BIG = -1e30

TQ = 128
TK = 512
FF_CHUNK = 512
HALO = 16
SEL_ROWS = 128
L_SAMPLE = PAST_LEN + PAGE_SIZE
T_NEW = 4
Q8 = SUBLANES


def _params(sem):
    return pltpu.CompilerParams(dimension_semantics=sem, vmem_limit_bytes=VMEM_LIMIT_BYTES)


def _dot(a, b):
    return jnp.dot(a, b, preferred_element_type=F32)


def _dot_nt(a, b):
    return lax.dot_general(a, b, (((1,), (1,)), ((), ())), preferred_element_type=F32)


def _rms_mod(x, g, shift, scale):
    ms = jnp.mean(x * x, axis=-1, keepdims=True)
    y = x * lax.rsqrt(ms + RMS_EPS) * g
    return y * (1.0 + scale) + shift


def _rmsnorm(x, g):
    ms = jnp.mean(x * x, axis=-1, keepdims=True)
    return x * lax.rsqrt(ms + RMS_EPS) * g


def _split3(x):
    hi = x.astype(BF16)
    r1 = x - hi.astype(F32)
    mid = r1.astype(BF16)
    lo = (r1 - mid.astype(F32)).astype(BF16)
    return hi, mid, lo


def _mod_kernel(c_ref, w_ref, b_ref, o_ref, tok_ref, *, n_prompt, t_new):
    c = c_ref[...]
    a = c * (1.0 / (1.0 + jnp.exp(-c)))
    w = w_ref[0]
    a_hi = a.astype(BF16)
    a_lo = (a - a_hi.astype(F32)).astype(BF16)
    w_hi = w.astype(BF16)
    w_lo = (w - w_hi.astype(F32)).astype(BF16)
    mod = _dot(a_hi, w_hi) + _dot(a_lo, w_hi) + _dot(a_hi, w_lo) + b_ref[0]
    o_ref[0] = mod
    n_tok, rows = tok_ref.shape[1], mod.shape[0]
    r = lax.broadcasted_iota(I32, (n_tok, rows), 0)
    s = (lax.broadcasted_iota(I32, (n_tok, rows), 1) - n_prompt) * t_new
    pick = jnp.where((r >= s) & (r < s + t_new), 1.0, 0.0).astype(BF16)
    hi, mid, lo = _split3(mod)
    tok_ref[0] = (_dot(pick, hi) + _dot(pick, mid)) + _dot(pick, lo)


def _mod_call(c_all, w_mod, b_mod, n_prompt, n_sample, t_new):
    rows = c_all.shape[0]
    tn = 1536
    return pl.pallas_call(
        functools.partial(_mod_kernel, n_prompt=n_prompt, t_new=t_new),
        grid=(DEPTH, 6 * D_MODEL // tn),
        in_specs=[
            pl.BlockSpec((rows, D_MODEL), lambda i, j: (0, 0)),
            pl.BlockSpec((1, D_MODEL, tn), lambda i, j: (i, 0, j)),
            pl.BlockSpec((1, 1, tn), lambda i, j: (i, 0, j)),
        ],
        out_specs=[pl.BlockSpec((1, rows, tn), lambda i, j: (i, 0, j)),
                   pl.BlockSpec((1, n_sample * t_new, tn), lambda i, j: (i, 0, j))],
        out_shape=[jax.ShapeDtypeStruct((DEPTH, rows, 6 * D_MODEL), F32),
                   jax.ShapeDtypeStruct((DEPTH, n_sample * t_new, 6 * D_MODEL), F32)],
        compiler_params=_params(("arbitrary", "arbitrary")),
        name="mod_vectors",
    )(c_all, w_mod, b_mod.reshape(DEPTH, 1, 6 * D_MODEL))


def _proj_kernel(x_ref, sh_ref, sc_ref, g_ref, w_ref,
                 k_ref, v_ref, ki_ref, q_ref, qi_ref, kiw_ref, kb_ref, vb_ref, kib_ref):
    h = _rms_mod(x_ref[0], g_ref[...], sh_ref[0], sc_ref[0]).astype(BF16)
    q_ref[0] = (_dot(h, w_ref[:, 0:Q_W]) * ATT_SCALE).astype(BF16)
    k = _dot(h, w_ref[:, OFF_K:OFF_K + KV_W])
    v = _dot(h, w_ref[:, OFF_V:OFF_V + KV_W])
    for g in range(N_KV_HEADS):
        k_ref[0, :, g, :] = k[:, g * HEAD_DIM:(g + 1) * HEAD_DIM]
        v_ref[0, :, g, :] = v[:, g * HEAD_DIM:(g + 1) * HEAD_DIM]
    kb_ref[0] = k.astype(BF16)
    vb_ref[0] = v.astype(BF16)
    qi_ref[0] = (_dot(h, w_ref[:, OFF_QI:OFF_QI + QI_W]) * IDX_SCALE).astype(BF16)
    kiw = _dot(h, w_ref[:, OFF_KIW:OFF_KIW + KIW_W])
    kiw_ref[0] = kiw
    ki = kiw[:, 0:IDX_DIM]
    ki_ref[0] = ki
    kib_ref[0] = ki.astype(BF16)


class _Mod(NamedTuple):
    arr: jax.Array
    spec: Callable[[int], pl.BlockSpec]


def _mod_per_sequence(arr):
    return _Mod(arr, lambda tm: pl.BlockSpec((1, 1, D_MODEL), lambda b, i: (b, 0, 0)))


def _mod_per_token(mod_tok, layer, j):
    return _Mod(mod_tok, lambda tm: pl.BlockSpec((1, tm, D_MODEL), lambda b, i: (layer, i, j)))


def _proj_call(x, shift, scale, g, w_in_b, tm):
    nb, t, _ = x.shape
    tok = lambda w: pl.BlockSpec((1, tm, w), lambda b, i: (b, i, 0))
    shp = lambda w, dt: jax.ShapeDtypeStruct((nb, t, w), dt)
    heads = pl.BlockSpec((1, tm, N_KV_HEADS, HEAD_DIM), lambda b, i: (b, i, 0, 0))
    heads_shape = jax.ShapeDtypeStruct((nb, t, N_KV_HEADS, HEAD_DIM), F32)
    return pl.pallas_call(
        _proj_kernel,
        grid=(nb, t // tm),
        in_specs=[
            tok(D_MODEL), shift.spec(tm), scale.spec(tm),
            pl.BlockSpec((1, D_MODEL), lambda b, i: (0, 0)),
            pl.BlockSpec((D_MODEL, D_PROJ_PAD), lambda b, i: (0, 0)),
        ],
        out_specs=[heads, heads, tok(IDX_DIM), tok(Q_W), tok(QI_W), tok(KIW_W),
                   tok(KV_W), tok(KV_W), tok(IDX_DIM)],
        out_shape=[heads_shape, heads_shape, shp(IDX_DIM, F32), shp(Q_W, BF16), shp(QI_W, BF16),
                   shp(KIW_W, F32), shp(KV_W, BF16), shp(KV_W, BF16), shp(IDX_DIM, BF16)],
        compiler_params=_params(("arbitrary", "arbitrary")),
        name="dsa_proj",
    )(x, shift.arr, scale.arr, g.reshape(1, D_MODEL), w_in_b)


def _score_keys(score, valid):
    b = lax.bitcast_convert_type(score, I32)
    key = jnp.where(b < 0, (b ^ INT_MAX) + 1, b)
    return jnp.where(valid, key, INT_MIN)


def _select_bias_t(keys_ref, bias_ref, ithr_ref, nblk, blk):
    nsub = blk // SUBLANES
    nacc = 4
    row8 = lax.broadcasted_iota(I32, (SUBLANES, LANES), 0)

    def count(pred):
        def body(c, accs):
            accs = list(accs)
            for j in range(nsub):
                row0 = pl.multiple_of(c * blk + j * SUBLANES, SUBLANES)
                hit = pred(keys_ref[pl.ds(row0, SUBLANES), :], row0)
                accs[j % nacc] = accs[j % nacc] + jnp.where(hit, 1, 0)
            return tuple(accs)
        accs = lax.fori_loop(0, nblk, body, tuple(jnp.zeros((SUBLANES, LANES), I32) for _ in range(nacc)))
        tot = (accs[0] + accs[1]) + (accs[2] + accs[3])
        return jnp.sum(tot, axis=0, keepdims=True)

    def bcast(v):
        return jnp.broadcast_to(v, (SUBLANES, LANES))

    n_nonneg = count(lambda kb, row0: kb >= 0)
    prefix = jnp.where(n_nonneg >= TOPK, 0, INT_MIN).astype(I32)

    def bit_body(t, prefix):
        cand = prefix | lax.shift_left(jnp.int32(1), 30 - t)
        cand_b = bcast(cand)
        n = count(lambda kb, row0: kb >= cand_b)
        return jnp.where(n >= TOPK, cand, prefix)

    thr = lax.fori_loop(0, 31, bit_body, prefix)
    thr_b = bcast(thr)
    n_gt = count(lambda kb, row0: kb > thr_b)
    n_ge = count(lambda kb, row0: kb >= thr_b)
    has_valid_thr = thr > INT_MIN
    tie = (n_ge > TOPK) & has_valid_thr
    ithr_ref[...] = bcast(jnp.where(has_valid_thr, INT_MAX, -1).astype(I32))

    @pl.when(jnp.max(tie.astype(I32)) > 0)
    def _():
        need = TOPK - n_gt

        def idx_body(t, ithr):
            cand = ithr | lax.shift_left(jnp.int32(1), 11 - t)
            cand_b = bcast(cand)
            n = count(lambda kb, row0: (kb == thr_b) & ((row0 + row8) < cand_b))
            return jnp.where(n < need, cand, ithr)

        ithr = lax.fori_loop(0, 12, idx_body, jnp.zeros((1, LANES), I32))
        ithr_ref[...] = jnp.where(bcast(tie), bcast(ithr), ithr_ref[...])

    thr_q = jnp.broadcast_to(thr, (LANES, LANES))
    ithr_q = jnp.broadcast_to(ithr_ref[0:1, :], (LANES, LANES))
    rowq = lax.broadcasted_iota(I32, (LANES, LANES), 0)

    def bias_body(c, carry):
        for j in range(blk // LANES):
            row0 = pl.multiple_of(c * blk + j * LANES, LANES)
            kb = keys_ref[pl.ds(row0, LANES), :]
            sel = (kb > thr_q) | ((kb == thr_q) & ((row0 + rowq) <= ithr_q))
            bias_ref[:, pl.ds(row0, LANES)] = jnp.where(sel, 0.0, NEG_BIG).astype(F32).T
        return carry

    lax.fori_loop(0, nblk, bias_body, 0)


def _dsa_prompt_kernel(q_ref, qi_ref, kiw_ref, kb_ref, vb_ref, kib_ref, o_ref,
                       keys_ref, bias_ref, ithr_ref, qir_ref, s_ref, mlane_ref, llane_ref, acc_ref):
    i = pl.program_id(1)
    nchunk = (i * TQ + TQ + TK - 1) // TK
    qpos = i * TQ + lax.broadcasted_iota(I32, (TK, LANES), 1)
    krow = lax.broadcasted_iota(I32, (TK, LANES), 0)

    for h in range(IDX_HEADS):
        qir_ref[h * TQ:(h + 1) * TQ, :] = qi_ref[0, :, h * IDX_DIM:(h + 1) * IDX_DIM]
    kiw_t = kiw_ref[0].T
    w_rows = [kiw_t[IDX_DIM + h:IDX_DIM + h + 1, :] * IDX_HEAD_SCALE for h in range(IDX_HEADS)]

    def idx_body(c, carry):
        c0 = pl.multiple_of(c * TK, TK)
        logit = _dot_nt(kib_ref[0, pl.ds(c0, TK), :], qir_ref[...])
        score = jnp.zeros((TK, LANES), F32)
        for h in range(IDX_HEADS):
            score = score + jnp.maximum(logit[:, h * TQ:(h + 1) * TQ], 0.0) * w_rows[h]
        keys_ref[pl.ds(c0, TK), :] = _score_keys(score, (c0 + krow) <= qpos)
        return carry

    lax.fori_loop(0, nchunk, idx_body, 0)

    _select_bias_t(keys_ref, bias_ref, ithr_ref, nchunk, TK)

    def q_group(g):
        return jnp.concatenate(
            [q_ref[0, :, (g * GROUP + r) * HEAD_DIM:(g * GROUP + r + 1) * HEAD_DIM] for r in range(GROUP)], axis=0)

    mlane_ref[...] = jnp.full(mlane_ref.shape, NEG_BIG, F32)

    def logits_body(c, carry):
        c0 = pl.multiple_of(c * TK, TK)
        bias4 = jnp.concatenate([bias_ref[:, pl.ds(c0, TK)]] * GROUP, axis=0)
        for g in range(N_KV_HEADS):
            s = _dot_nt(q_group(g), kb_ref[0, pl.ds(c0, TK), g * HEAD_DIM:(g + 1) * HEAD_DIM]) + bias4
            s_ref[g, :, pl.ds(c0, TK)] = s
            m = mlane_ref[g]
            for j in range(TK // LANES):
                m = jnp.maximum(m, s[:, j * LANES:(j + 1) * LANES])
            mlane_ref[g] = m
        return carry

    lax.fori_loop(0, nchunk, logits_body, 0)

    for g in range(N_KV_HEADS):
        mlane_ref[g] = jnp.broadcast_to(jnp.max(mlane_ref[g], axis=1, keepdims=True), mlane_ref.shape[1:])
    llane_ref[...] = jnp.zeros(llane_ref.shape, F32)
    acc_ref[...] = jnp.zeros(acc_ref.shape, F32)

    def pv_body(c, carry):
        c0 = pl.multiple_of(c * TK, TK)
        for g in range(N_KV_HEADS):
            m = mlane_ref[g]
            l = llane_ref[g]
            ps = []
            for j in range(TK // LANES):
                pj = jnp.exp2(s_ref[g, :, pl.ds(pl.multiple_of(c0 + j * LANES, LANES), LANES)] - m)
                l = l + pj
                ps.append(pj.astype(BF16))
            llane_ref[g] = l
            p = jnp.concatenate(ps, axis=1)
            acc_ref[g] += _dot(p, vb_ref[0, pl.ds(c0, TK), g * HEAD_DIM:(g + 1) * HEAD_DIM])
        return carry

    lax.fori_loop(0, nchunk, pv_body, 0)

    for g in range(N_KV_HEADS):
        og = acc_ref[g] * (1.0 / jnp.sum(llane_ref[g], axis=1, keepdims=True))
        for r in range(GROUP):
            col = (g * GROUP + r) * HEAD_DIM
            o_ref[0, :, col:col + HEAD_DIM] = og[r * TQ:(r + 1) * TQ].astype(BF16)


def _dsa_prompt_call(q, qi, kiw, kb, vb, kib):
    nb, s, _ = q.shape
    tile = lambda w: pl.BlockSpec((1, TQ, w), lambda b, i: (b, i, 0))
    full = lambda w: pl.BlockSpec((1, s, w), lambda b, i: (b, 0, 0))
    rows = GROUP * TQ
    return pl.pallas_call(
        _dsa_prompt_kernel,
        grid=(nb, s // TQ),
        in_specs=[tile(Q_W), tile(QI_W), tile(KIW_W), full(KV_W), full(KV_W), full(IDX_DIM)],
        out_specs=tile(Q_W),
        out_shape=jax.ShapeDtypeStruct((nb, s, Q_W), BF16),
        scratch_shapes=[
            pltpu.VMEM((s, TQ), I32),
            pltpu.VMEM((TQ, s), F32),
            pltpu.VMEM((SUBLANES, LANES), I32),
            pltpu.VMEM((IDX_HEADS * TQ, IDX_DIM), BF16),
            pltpu.VMEM((N_KV_HEADS, rows, s), F32),
            pltpu.VMEM((N_KV_HEADS, rows, LANES), F32),
            pltpu.VMEM((N_KV_HEADS, rows, LANES), F32),
            pltpu.VMEM((N_KV_HEADS, rows, HEAD_DIM), F32),
        ],
        compiler_params=_params(("arbitrary", "arbitrary")),
        name="dsa_prompt",
    )(q, qi, kiw, kb, vb, kib)


def _page_specs(page_shape):
    zeros = (0,) * len(page_shape)
    return [pl.BlockSpec((1,) + page_shape, functools.partial(lambda n, pt, p: (pt[n, p],) + zeros, p=p))
            for p in range(N_PAGES)]


def _kv_page(page_ref):
    heads = [page_ref[0, pl.ds(g, PAGE_SIZE, stride=N_KV_HEADS), :] for g in range(N_KV_HEADS)]
    return jnp.concatenate(heads, axis=1).astype(BF16)


def _pad_new_page(new8):
    return jnp.concatenate([new8, jnp.zeros((PAGE_SIZE - Q8, new8.shape[1]), F32)], axis=0).astype(BF16)


def _idx_sample_kernel(pt_ref, qis_ref, ws_ref, kin_ref, *rest):
    pages = rest[:N_PAGES]
    o_ref = rest[N_PAGES]
    del pt_ref
    qs = qis_ref[0]
    w = ws_ref[0]

    def page_score(kt):
        r = jnp.maximum(_dot(qs, kt.astype(BF16)), 0.0) * w
        s = r[0:Q8]
        for h in range(1, IDX_HEADS):
            s = s + r[h * Q8:(h + 1) * Q8]
        return s

    for p in range(N_PAGES):
        o_ref[0, :, p * PAGE_SIZE:(p + 1) * PAGE_SIZE] = page_score(pages[p][0])
    o_ref[0, :, PAST_LEN:L_SAMPLE] = page_score(kin_ref[0])


def _idx_sample_call(page_table, qis, ws, kinew_t, cache_ki_t):
    n = qis.shape[0]
    rows = IDX_HEADS * Q8
    grid_spec = pltpu.PrefetchScalarGridSpec(
        num_scalar_prefetch=1,
        grid=(n,),
        in_specs=[
            pl.BlockSpec((1, rows, IDX_DIM), lambda i, pt: (i, 0, 0)),
            pl.BlockSpec((1, rows, 1), lambda i, pt: (i, 0, 0)),
            pl.BlockSpec((1, IDX_DIM, PAGE_SIZE), lambda i, pt: (i, 0, 0)),
        ] + _page_specs((IDX_DIM, PAGE_SIZE)),
        out_specs=pl.BlockSpec((1, Q8, L_SAMPLE), lambda i, pt: (i, 0, 0)),
    )
    return pl.pallas_call(
        _idx_sample_kernel,
        grid_spec=grid_spec,
        out_shape=jax.ShapeDtypeStruct((n, Q8, L_SAMPLE), F32),
        compiler_params=_params(("arbitrary",)),
        name="dsa_sample_scores",
    )(page_table, qis, ws, kinew_t, *([cache_ki_t] * N_PAGES))


def _select_sample_kernel(s_ref, bias_ref, keys_ref, ithr_ref):
    krow = lax.broadcasted_iota(I32, (LANES, SEL_ROWS), 0)
    qrow = lax.broadcasted_iota(I32, (LANES, SEL_ROWS), 1) % Q8
    real = qrow < T_NEW
    for j in range(L_SAMPLE // LANES):
        kidx = j * LANES + krow
        valid = ((kidx < PAST_LEN) | ((kidx - PAST_LEN) <= qrow)) & real
        keys_ref[j * LANES:(j + 1) * LANES, :] = _score_keys(s_ref[:, j * LANES:(j + 1) * LANES].T, valid)
    _select_bias_t(keys_ref, bias_ref, ithr_ref, L_SAMPLE // LANES, LANES)


def _select_sample_call(scores):
    rows = scores.shape[0]
    return pl.pallas_call(
        _select_sample_kernel,
        grid=(rows // SEL_ROWS,),
        in_specs=[pl.BlockSpec((SEL_ROWS, L_SAMPLE), lambda i: (i, 0))],
        out_specs=pl.BlockSpec((SEL_ROWS, L_SAMPLE), lambda i: (i, 0)),
        out_shape=jax.ShapeDtypeStruct((rows, L_SAMPLE), F32),
        scratch_shapes=[pltpu.VMEM((L_SAMPLE, SEL_ROWS), I32), pltpu.VMEM((SUBLANES, LANES), I32)],
        compiler_params=_params(("arbitrary",)),
        name="dsa_sample_select",
    )(scores)


def _attn_sample_kernel(pt_ref, qb_ref, bias_ref, kn_ref, vn_ref, *rest):
    kpages = rest[:N_PAGES]
    vpages = rest[N_PAGES:2 * N_PAGES]
    o_ref = rest[2 * N_PAGES]
    s_ref = rest[2 * N_PAGES + 1]
    del pt_ref
    qb = qb_ref[0]
    reps = qb.shape[0] // Q8

    def page_logits(p, kp):
        bias8 = bias_ref[0, :, p * PAGE_SIZE:(p + 1) * PAGE_SIZE]
        s_ref[:, p * PAGE_SIZE:(p + 1) * PAGE_SIZE] = _dot_nt(qb, kp) + jnp.concatenate([bias8] * reps, axis=0)

    for p in range(N_PAGES):
        page_logits(p, _kv_page(kpages[p]))
    page_logits(N_PAGES, _pad_new_page(kn_ref[0]))

    s = s_ref[...]
    m = jnp.max(s, axis=1, keepdims=True)
    e = jnp.exp2(s - m)
    l = jnp.sum(e, axis=1, keepdims=True)
    eb = e.astype(BF16)
    acc = _dot(eb[:, PAST_LEN:L_SAMPLE], _pad_new_page(vn_ref[0]))
    for p in range(N_PAGES):
        acc = acc + _dot(eb[:, p * PAGE_SIZE:(p + 1) * PAGE_SIZE], _kv_page(vpages[p]))
    o_ref[0] = acc * (1.0 / l)


def _attn_sample_call(page_table, qblk, bias, knew8, vnew8, cache_k, cache_v):
    n, rows, _ = qblk.shape
    grid_spec = pltpu.PrefetchScalarGridSpec(
        num_scalar_prefetch=1,
        grid=(n,),
        in_specs=[
            pl.BlockSpec((1, rows, KV_W), lambda i, pt: (i, 0, 0)),
            pl.BlockSpec((1, Q8, L_SAMPLE), lambda i, pt: (i, 0, 0)),
            pl.BlockSpec((1, Q8, KV_W), lambda i, pt: (i, 0, 0)),
            pl.BlockSpec((1, Q8, KV_W), lambda i, pt: (i, 0, 0)),
        ] + _page_specs((PAGE_SIZE * N_KV_HEADS, HEAD_DIM)) + _page_specs((PAGE_SIZE * N_KV_HEADS, HEAD_DIM)),
        out_specs=pl.BlockSpec((1, rows, KV_W), lambda i, pt: (i, 0, 0)),
        scratch_shapes=[pltpu.VMEM((rows, L_SAMPLE), F32)],
    )
    return pl.pallas_call(
        _attn_sample_kernel,
        grid_spec=grid_spec,
        out_shape=jax.ShapeDtypeStruct((n, rows, KV_W), F32),
        compiler_params=_params(("arbitrary",)),
        name="dsa_sample_attn",
    )(page_table, qblk, bias, knew8, vnew8, *([cache_k] * N_PAGES), *([cache_v] * N_PAGES))


def _mlp_resid(x1, g_ffn, shift, scale, gate, wup_ref, wdn_ref, acc_ref):
    h = _rms_mod(x1, g_ffn, shift, scale).astype(BF16)
    for c in range(D_FF // FF_CHUNK):
        u = _dot(h, wup_ref[:, c * FF_CHUNK:(c + 1) * FF_CHUNK])
        u = jnp.square(jnp.maximum(u, 0.0)).astype(BF16)
        d = _dot(u, wdn_ref[c * FF_CHUNK:(c + 1) * FF_CHUNK, :])
        if c == 0:
            acc_ref[...] = d
        else:
            acc_ref[...] += d
    return x1 + gate * acc_ref[...]


def _post0_kernel(x_ref, o_ref, g1_ref, sh2_ref, sc2_ref, g2_ref, gffn_ref, wo_ref, wup_ref, wdn_ref,
                  out_ref, acc_ref):
    x1 = x_ref[0] + g1_ref[0] * _dot(o_ref[0], wo_ref[...])
    out_ref[0] = _mlp_resid(x1, gffn_ref[...], sh2_ref[0], sc2_ref[0], g2_ref[0], wup_ref, wdn_ref, acc_ref)


def _const_spec(shape):
    nd = len(shape)
    return pl.BlockSpec(shape, lambda b, i: (0,) * nd)


def _post0_call(x, o, g1, sh2, sc2, g2, g_ffn, w_o, w_up, w_dn, tm):
    nb, t, _ = x.shape
    tok = lambda w: pl.BlockSpec((1, tm, w), lambda b, i: (b, i, 0))
    return pl.pallas_call(
        _post0_kernel,
        grid=(nb, t // tm),
        in_specs=[tok(D_MODEL), tok(Q_W), g1.spec(tm), sh2.spec(tm), sc2.spec(tm),
                  g2.spec(tm), _const_spec((1, D_MODEL)), _const_spec((Q_W, D_MODEL)),
                  _const_spec((D_MODEL, D_FF)), _const_spec((D_FF, D_MODEL))],
        out_specs=tok(D_MODEL),
        out_shape=jax.ShapeDtypeStruct((nb, t, D_MODEL), F32),
        scratch_shapes=[pltpu.VMEM((tm, D_MODEL), F32)],
        compiler_params=_params(("arbitrary", "arbitrary")),
        name="layer0_out",
    )(x, o, g1.arr, sh2.arr, sc2.arr, g2.arr, g_ffn.reshape(1, D_MODEL), w_o, w_up, w_dn)


def _pool_project(pooled, wp_ref, ps_ref):
    parts = []
    for g in range(len(POOL_WINDOWS)):
        lo = g * POOL_GROUP_DIM
        parts.append(_dot(pooled[:, lo:lo + POOL_GROUP_DIM].astype(BF16), wp_ref[g]))
    return jnp.concatenate(parts, axis=1) * ps_ref[...]


def _layer1_prompt_kernel(x_ref, xh_ref, sh1_ref, sc1_ref, g1_ref, sh2_ref, sc2_ref, g2_ref,
                          gmix_ref, gffn_ref, gfin_ref, wp_ref, ps_ref, wup_ref, wdn_ref,
                          y_ref, hlast_ref, acc_ref):
    i = pl.program_id(1)
    tm = x_ref.shape[1]
    x = x_ref[0]
    sh1, sc1 = sh1_ref[0], sc1_ref[0]
    h = _rms_mod(x, gmix_ref[...], sh1, sc1)
    h_halo = _rms_mod(xh_ref[0], gmix_ref[...], sh1, sc1)
    h_halo = jnp.where(i > 0, h_halo, 0.0)
    ext = jnp.concatenate([h_halo, h], axis=0)
    pos = i * tm + lax.broadcasted_iota(I32, (tm, 1), 0)
    parts = []
    for g, w in enumerate(POOL_WINDOWS):
        lo = g * POOL_GROUP_DIM
        s = ext[:, lo:lo + POOL_GROUP_DIM]
        step = 1
        while step < w:
            s = s + pltpu.roll(s, step, axis=0)
            step *= 2
        cnt = jnp.minimum(w, pos + 1).astype(F32)
        parts.append(s[HALO:, :] / cnt - h[:, lo:lo + POOL_GROUP_DIM])
    pooled = jnp.concatenate(parts, axis=1)
    x1 = x + g1_ref[0] * _pool_project(pooled, wp_ref, ps_ref)
    x2 = _mlp_resid(x1, gffn_ref[...], sh2_ref[0], sc2_ref[0], g2_ref[0], wup_ref, wdn_ref, acc_ref)
    y_ref[0] = _rmsnorm(x2, gfin_ref[...])

    @pl.when(i == pl.num_programs(1) - 1)
    def _():
        hlast_ref[0] = h[tm - HALO:, :]


def _layer1_prompt_call(x, mods, g_mix, g_ffn, g_fin, pool_w, pool_scale, w_up, w_dn, tm):
    nb, t, _ = x.shape
    tok = pl.BlockSpec((1, tm, D_MODEL), lambda b, i: (b, i, 0))
    halo = pl.BlockSpec((1, HALO, D_MODEL), lambda b, i: (b, jnp.maximum(i * (tm // HALO) - 1, 0), 0))
    vec = _const_spec((1, D_MODEL))
    return pl.pallas_call(
        _layer1_prompt_kernel,
        grid=(nb, t // tm),
        in_specs=[tok, halo] + [m.spec(tm) for m in mods] + [vec, vec, vec,
                  _const_spec(pool_w.shape), vec, _const_spec((D_MODEL, D_FF)), _const_spec((D_FF, D_MODEL))],
        out_specs=[tok, pl.BlockSpec((1, HALO, D_MODEL), lambda b, i: (b, 0, 0))],
        out_shape=[jax.ShapeDtypeStruct((nb, t, D_MODEL), F32), jax.ShapeDtypeStruct((nb, HALO, D_MODEL), F32)],
        scratch_shapes=[pltpu.VMEM((tm, D_MODEL), F32)],
        compiler_params=_params(("arbitrary", "arbitrary")),
        name="layer1_prompt",
    )(x, x, *[m.arr for m in mods], g_mix.reshape(1, D_MODEL), g_ffn.reshape(1, D_MODEL),
      g_fin.reshape(1, D_MODEL), pool_w, pool_scale.reshape(1, D_MODEL), w_up, w_dn)


def _layer1_sample_kernel(x_ref, prev_ref, sh1_ref, sc1_ref, g1_ref, sh2_ref, sc2_ref, g2_ref,
                          gmix_ref, gffn_ref, gfin_ref, wp_ref, ps_ref, wup_ref, wdn_ref,
                          y_ref, h_ref, hs_ref, pooled_ref, acc_ref):
    tm = x_ref.shape[1]
    t_new = tm // prev_ref.shape[1]
    nseq = prev_ref.shape[1]
    x = x_ref[0]
    h = _rms_mod(x, gmix_ref[...], sh1_ref[0], sc1_ref[0])
    h_ref[0] = h
    ncol = D_MODEL // LANES
    for c in range(ncol):
        hs_ref[c] = h[:, c * LANES:(c + 1) * LANES]
    ext = [prev_ref[j] for j in range(POOL_STATE)]
    ext += [jnp.concatenate([hs_ref[c, pl.ds(t, nseq, stride=t_new), :] for c in range(ncol)], axis=1)
            for t in range(t_new)]
    for t in range(t_new):
        parts = []
        for g, w in enumerate(POOL_WINDOWS):
            lo = g * POOL_GROUP_DIM
            s = ext[POOL_STATE + t][:, lo:lo + POOL_GROUP_DIM]
            for j in range(1, w):
                s = s + ext[POOL_STATE + t - j][:, lo:lo + POOL_GROUP_DIM]
            parts.append(s / float(w) - ext[POOL_STATE + t][:, lo:lo + POOL_GROUP_DIM])
        pooled_t = jnp.concatenate(parts, axis=1)
        for c in range(ncol):
            pooled_ref[c, pl.ds(t, nseq, stride=t_new), :] = pooled_t[:, c * LANES:(c + 1) * LANES]
    pooled = jnp.concatenate([pooled_ref[c] for c in range(ncol)], axis=1)
    x1 = x + g1_ref[0] * _pool_project(pooled, wp_ref, ps_ref)
    x2 = _mlp_resid(x1, gffn_ref[...], sh2_ref[0], sc2_ref[0], g2_ref[0], wup_ref, wdn_ref, acc_ref)
    y_ref[0] = _rmsnorm(x2, gfin_ref[...])


def _layer1_sample_call(x, prev_t, mods, g_mix, g_ffn, g_fin, pool_w, pool_scale, w_up, w_dn, tm, t_new):
    nb, t, _ = x.shape
    tok = pl.BlockSpec((1, tm, D_MODEL), lambda b, i: (b, i, 0))
    prev = pl.BlockSpec((POOL_STATE, tm // t_new, D_MODEL), lambda b, i: (0, i, 0))
    vec = _const_spec((1, D_MODEL))
    return pl.pallas_call(
        _layer1_sample_kernel,
        grid=(nb, t // tm),
        in_specs=[tok, prev] + [m.spec(tm) for m in mods] + [vec, vec, vec,
                  _const_spec(pool_w.shape), vec, _const_spec((D_MODEL, D_FF)), _const_spec((D_FF, D_MODEL))],
        out_specs=[tok, tok],
        out_shape=[jax.ShapeDtypeStruct((nb, t, D_MODEL), F32), jax.ShapeDtypeStruct((nb, t, D_MODEL), F32)],
        scratch_shapes=[pltpu.VMEM((D_MODEL // LANES, tm, LANES), F32),
                        pltpu.VMEM((D_MODEL // LANES, tm, LANES), F32),
                        pltpu.VMEM((tm, D_MODEL), F32)],
        compiler_params=_params(("arbitrary", "arbitrary")),
        name="layer1_sample",
    )(x, prev_t, *[m.arr for m in mods], g_mix.reshape(1, D_MODEL), g_ffn.reshape(1, D_MODEL),
      g_fin.reshape(1, D_MODEL), pool_w, pool_scale.reshape(1, D_MODEL), w_up, w_dn)


def kernel(x_prompt, x_sample, cache_k, cache_v, cache_kidx, state_pool, page_table, c_prompt, c_sample,
           norm_mix_g, norm_ffn_g, w_mod, b_mod, dsa_w_in, dsa_w_o, pool_w, pool_scale, w_up, w_down,
           final_norm_g):
    nbp, seq, _ = x_prompt.shape
    nbs, t_new, _ = x_sample.shape
    n_tok_s = nbs * t_new

    w_in_b = jnp.pad(dsa_w_in[0], ((0, 0), (0, D_PROJ_PAD - D_PROJ))).astype(BF16)
    w_o_b = dsa_w_o[0].astype(BF16)
    w_up_b = w_up.astype(BF16)
    w_dn_b = w_down.astype(BF16)
    pool_w_b = pool_w[0].astype(BF16)

    n_c = nbp + nbs
    c_rows = -(-n_c // SUBLANES) * SUBLANES
    c_all = jnp.pad(jnp.concatenate([c_prompt, c_sample], axis=0), ((0, c_rows - n_c), (0, 0)))
    mod, mod_tok = _mod_call(c_all, w_mod, b_mod, nbp, nbs, t_new)

    def mods_prompt(layer):
        m = mod[layer, :nbp].reshape(nbp, 1, 6, D_MODEL)
        return [_mod_per_sequence(m[:, :, j]) for j in range(6)]

    def mods_sample(layer):
        return [_mod_per_token(mod_tok, layer, j) for j in range(6)]

    sh1, sc1, g1, sh2, sc2, g2 = mods_prompt(0)
    k_p, v_p, ki_p, q, qi, kiw, kb, vb, kib = _proj_call(x_prompt, sh1, sc1, norm_mix_g[0], w_in_b, 512)
    o = _dsa_prompt_call(q, qi, kiw, kb, vb, kib)
    x1 = _post0_call(x_prompt, o, g1, sh2, sc2, g2, norm_ffn_g[0], w_o_b, w_up_b[0], w_dn_b[0], 512)
    y_prompt, hlast = _layer1_prompt_call(x1, mods_prompt(1), norm_mix_g[1], norm_ffn_g[1], final_norm_g,
                                          pool_w_b, pool_scale[0], w_up_b[1], w_dn_b[1], 512)
    pool_p = hlast[:, HALO - POOL_STATE:]

    sh1, sc1, g1, sh2, sc2, g2 = mods_sample(0)
    xs = x_sample.reshape(1, n_tok_s, D_MODEL)
    tms = 256
    k_s, v_s, ki_s, q, qi, kiw, _, _, _ = _proj_call(xs, sh1, sc1, norm_mix_g[0], w_in_b, tms)

    pad_q = lambda a: jnp.pad(a, [(0, 0)] * (a.ndim - 2) + [(0, Q8 - t_new), (0, 0)])
    qis = pad_q(qi.reshape(nbs, t_new, IDX_HEADS, IDX_DIM).transpose(0, 2, 1, 3)).reshape(nbs, IDX_HEADS * Q8, IDX_DIM)
    wi = kiw.reshape(nbs, t_new, KIW_W)[:, :, IDX_DIM:IDX_DIM + IDX_HEADS] * IDX_HEAD_SCALE
    ws = pad_q(wi.transpose(0, 2, 1)[..., None]).reshape(nbs, IDX_HEADS * Q8, 1)
    knew8 = pad_q(k_s.reshape(nbs, t_new, KV_W))
    vnew8 = pad_q(v_s.reshape(nbs, t_new, KV_W))
    kinew_t = jnp.pad(ki_s.reshape(nbs, t_new, IDX_DIM).transpose(0, 2, 1), ((0, 0), (0, 0), (0, PAGE_SIZE - t_new)))
    scores = _idx_sample_call(page_table, qis, ws, kinew_t, cache_kidx[0].transpose(0, 2, 1))
    bias = _select_sample_call(scores.reshape(nbs * Q8, L_SAMPLE)).reshape(nbs, Q8, L_SAMPLE)
    qg = pad_q(q.reshape(nbs, t_new, N_KV_HEADS, GROUP, HEAD_DIM).transpose(0, 2, 3, 1, 4))
    eye = jnp.eye(N_KV_HEADS, dtype=BF16)
    qblk = (qg[:, :, :, :, None, :] * eye[None, :, None, None, :, None]).reshape(nbs, N_KV_HEADS * GROUP * Q8, KV_W)
    n_pool = cache_k.shape[1]
    page_rows = lambda c: c[0].reshape(n_pool, PAGE_SIZE * N_KV_HEADS, HEAD_DIM)
    o_blk = _attn_sample_call(page_table, qblk, bias, knew8, vnew8, page_rows(cache_k), page_rows(cache_v))
    o_blk = o_blk.reshape(nbs, N_KV_HEADS, GROUP, Q8, N_KV_HEADS, HEAD_DIM)
    o_s = jnp.stack([o_blk[:, g, :, :t_new, g] for g in range(N_KV_HEADS)], axis=1)
    o_s = o_s.transpose(0, 3, 1, 2, 4).reshape(1, n_tok_s, Q_W).astype(BF16)
    x1s = _post0_call(xs, o_s, g1, sh2, sc2, g2, norm_ffn_g[0], w_o_b, w_up_b[0], w_dn_b[0], tms)
    prev_t = state_pool[0].transpose(1, 0, 2)
    y_s, h1s = _layer1_sample_call(x1s, prev_t, mods_sample(1), norm_mix_g[1], norm_ffn_g[1], final_norm_g,
                                   pool_w_b, pool_scale[0], w_up_b[1], w_dn_b[1], tms, t_new)
    y_sample = y_s.reshape(nbs, t_new, D_MODEL)
    pool_s = jnp.concatenate([state_pool[0][:, t_new:], h1s.reshape(nbs, t_new, D_MODEL)], axis=1)

    return (
        y_prompt, y_sample,
        k_p[None], v_p[None], ki_p[None], pool_p[None],
        k_s.reshape(1, nbs, t_new, N_KV_HEADS, HEAD_DIM), v_s.reshape(1, nbs, t_new, N_KV_HEADS, HEAD_DIM),
        ki_s.reshape(1, nbs, t_new, IDX_DIM), pool_s[None],
    )
```

```python
import functools
from typing import Callable, NamedTuple

import jax
import jax.numpy as jnp
from jax import lax
from jax.experimental import pallas as pl
from jax.experimental.pallas import tpu as pltpu

F32 = jnp.float32
BF16 = jnp.bfloat16
I32 = jnp.int32

D_MODEL = 1024
DEPTH = 2
PAST_LEN = 2048
PAGE_SIZE = 128
N_PAGES = PAST_LEN // PAGE_SIZE
N_HEADS = 8
HEAD_DIM = 128
N_KV_HEADS = 2
GROUP = N_HEADS // N_KV_HEADS
IDX_HEADS = 8
IDX_DIM = 64
TOPK = 256
POOL_WINDOWS = (2, 4, 8, 16)
POOL_GROUP_DIM = D_MODEL // len(POOL_WINDOWS)
POOL_STATE = max(POOL_WINDOWS) - 1
D_FF = 4 * D_MODEL
RMS_EPS = 1e-6
Q_W = N_HEADS * HEAD_DIM
KV_W = N_KV_HEADS * HEAD_DIM
QI_W = IDX_HEADS * IDX_DIM
D_PROJ = Q_W + 2 * KV_W + QI_W + IDX_DIM + IDX_HEADS

LANES = 128
SUBLANES = 8
VMEM_LIMIT_BYTES = 56 * 1024 * 1024

KIW_W = LANES
D_PROJ_PAD = Q_W + 2 * KV_W + QI_W + KIW_W
OFF_K = Q_W
OFF_V = OFF_K + KV_W
OFF_QI = OFF_V + KV_W
OFF_KIW = OFF_QI + QI_W

LOG2_E = 1.4426950408889634
ATT_SCALE = HEAD_DIM ** -0.5 * LOG2_E
IDX_SCALE = IDX_DIM ** -0.5
IDX_HEAD_SCALE = IDX_HEADS ** -0.5

INT_MIN = -(2 ** 31)
INT_MAX = 2 ** 31 - 1
F32_MAX = 3.4028234663852886e38
NEG_INF_KEY = INT_MIN + (1 << 23)
NEG_BIG = -1e30

TQ = 128
TK = 512
FF_CHUNK = 512
HALO = 16
SEL_ROWS = 128
L_SAMPLE = PAST_LEN + PAGE_SIZE
T_NEW = 4
SCORE_SEQS = 4
ATTN_SEQS = 2
Q8 = SUBLANES


def _params(sem):
    return pltpu.CompilerParams(dimension_semantics=sem, vmem_limit_bytes=VMEM_LIMIT_BYTES)


def _dot(a, b):
    return jnp.dot(a, b, preferred_element_type=F32)


def _dot_nt(a, b):
    return lax.dot_general(a, b, (((1,), (1,)), ((), ())), preferred_element_type=F32)


def _rms_mod(x, g, shift, scale):
    ms = jnp.mean(x * x, axis=-1, keepdims=True)
    y = x * lax.rsqrt(ms + RMS_EPS) * g
    return y * (1.0 + scale) + shift


def _rmsnorm(x, g):
    ms = jnp.mean(x * x, axis=-1, keepdims=True)
    return x * lax.rsqrt(ms + RMS_EPS) * g


def _split3(x):
    hi = x.astype(BF16)
    r1 = x - hi.astype(F32)
    mid = r1.astype(BF16)
    lo = (r1 - mid.astype(F32)).astype(BF16)
    return hi, mid, lo


def _mod_kernel(c_ref, w_ref, b_ref, o_ref, tok_ref, *, n_prompt, t_new):
    c = c_ref[...]
    a = c * (1.0 / (1.0 + jnp.exp(-c)))
    w = w_ref[0]
    a_hi = a.astype(BF16)
    a_lo = (a - a_hi.astype(F32)).astype(BF16)
    w_hi = w.astype(BF16)
    w_lo = (w - w_hi.astype(F32)).astype(BF16)
    mod = _dot(a_hi, w_hi) + _dot(a_lo, w_hi) + _dot(a_hi, w_lo) + b_ref[0]
    o_ref[0] = mod
    n_tok, rows = tok_ref.shape[1], mod.shape[0]
    r = lax.broadcasted_iota(I32, (n_tok, rows), 0)
    s = (lax.broadcasted_iota(I32, (n_tok, rows), 1) - n_prompt) * t_new
    pick = jnp.where((r >= s) & (r < s + t_new), 1.0, 0.0).astype(BF16)
    hi, mid, lo = _split3(mod)
    tok_ref[0] = (_dot(pick, hi) + _dot(pick, mid)) + _dot(pick, lo)


def _mod_call(c_all, w_mod, b_mod, n_prompt, n_sample, t_new):
    rows = c_all.shape[0]
    tn = 1536
    return pl.pallas_call(
        functools.partial(_mod_kernel, n_prompt=n_prompt, t_new=t_new),
        grid=(DEPTH, 6 * D_MODEL // tn),
        in_specs=[
            pl.BlockSpec((rows, D_MODEL), lambda i, j: (0, 0)),
            pl.BlockSpec((1, D_MODEL, tn), lambda i, j: (i, 0, j)),
            pl.BlockSpec((1, 1, tn), lambda i, j: (i, 0, j)),
        ],
        out_specs=[pl.BlockSpec((1, rows, tn), lambda i, j: (i, 0, j)),
                   pl.BlockSpec((1, n_sample * t_new, tn), lambda i, j: (i, 0, j))],
        out_shape=[jax.ShapeDtypeStruct((DEPTH, rows, 6 * D_MODEL), F32),
                   jax.ShapeDtypeStruct((DEPTH, n_sample * t_new, 6 * D_MODEL), F32)],
        compiler_params=_params(("arbitrary", "arbitrary")),
        name="mod_vectors",
    )(c_all, w_mod, b_mod.reshape(DEPTH, 1, 6 * D_MODEL))


def _proj_kernel(x_ref, sh_ref, sc_ref, g_ref, w_ref,
                 k_ref, v_ref, ki_ref, q_ref, qi_ref, kiw_ref, kb_ref, vb_ref, kib_ref):
    h = _rms_mod(x_ref[0], g_ref[...], sh_ref[0], sc_ref[0]).astype(BF16)
    q_ref[0] = (_dot(h, w_ref[:, 0:Q_W]) * ATT_SCALE).astype(BF16)
    k = _dot(h, w_ref[:, OFF_K:OFF_K + KV_W])
    v = _dot(h, w_ref[:, OFF_V:OFF_V + KV_W])
    for g in range(N_KV_HEADS):
        k_ref[0, :, g, :] = k[:, g * HEAD_DIM:(g + 1) * HEAD_DIM]
        v_ref[0, :, g, :] = v[:, g * HEAD_DIM:(g + 1) * HEAD_DIM]
    kb_ref[0] = k.astype(BF16)
    vb_ref[0] = v.astype(BF16)
    qi_ref[0] = (_dot(h, w_ref[:, OFF_QI:OFF_QI + QI_W]) * IDX_SCALE).astype(BF16)
    kiw = _dot(h, w_ref[:, OFF_KIW:OFF_KIW + KIW_W])
    kiw_ref[0] = kiw
    ki = kiw[:, 0:IDX_DIM]
    ki_ref[0] = ki
    kib_ref[0] = ki.astype(BF16)


class _Mod(NamedTuple):
    arr: jax.Array
    spec: Callable[[int], pl.BlockSpec]


def _mod_per_sequence(arr):
    return _Mod(arr, lambda tm: pl.BlockSpec((1, 1, D_MODEL), lambda b, i: (b, 0, 0)))


def _mod_per_token(mod_tok, layer, j):
    return _Mod(mod_tok, lambda tm: pl.BlockSpec((1, tm, D_MODEL), lambda b, i: (layer, i, j)))


def _proj_call(x, shift, scale, g, w_in_b, tm):
    nb, t, _ = x.shape
    tok = lambda w: pl.BlockSpec((1, tm, w), lambda b, i: (b, i, 0))
    shp = lambda w, dt: jax.ShapeDtypeStruct((nb, t, w), dt)
    heads = pl.BlockSpec((1, tm, N_KV_HEADS, HEAD_DIM), lambda b, i: (b, i, 0, 0))
    heads_shape = jax.ShapeDtypeStruct((nb, t, N_KV_HEADS, HEAD_DIM), F32)
    return pl.pallas_call(
        _proj_kernel,
        grid=(nb, t // tm),
        in_specs=[
            tok(D_MODEL), shift.spec(tm), scale.spec(tm),
            pl.BlockSpec((1, D_MODEL), lambda b, i: (0, 0)),
            pl.BlockSpec((D_MODEL, D_PROJ_PAD), lambda b, i: (0, 0)),
        ],
        out_specs=[heads, heads, tok(IDX_DIM), tok(Q_W), tok(QI_W), tok(KIW_W),
                   tok(KV_W), tok(KV_W), tok(IDX_DIM)],
        out_shape=[heads_shape, heads_shape, shp(IDX_DIM, F32), shp(Q_W, BF16), shp(QI_W, BF16),
                   shp(KIW_W, F32), shp(KV_W, BF16), shp(KV_W, BF16), shp(IDX_DIM, BF16)],
        compiler_params=_params(("arbitrary", "arbitrary")),
        name="dsa_proj",
    )(x, shift.arr, scale.arr, g.reshape(1, D_MODEL), w_in_b)


def _key_float(k):
    return lax.bitcast_convert_type(jnp.where(k >= 0, k, (k - 1) ^ INT_MAX), F32)


def _select_bias_t(score_ref, bias_ref, ithr_ref, nblk, blk):
    nsub = blk // SUBLANES
    nacc = 4
    row8 = lax.broadcasted_iota(I32, (SUBLANES, LANES), 0)

    def count(pred):
        def body(c, accs):
            accs = list(accs)
            c0 = pl.multiple_of(c * blk, blk)
            rows = score_ref[pl.ds(c0, blk), :]
            for j in range(nsub):
                hit = pred(rows[j * SUBLANES:(j + 1) * SUBLANES], c0 + j * SUBLANES)
                accs[j % nacc] = accs[j % nacc] + jnp.where(hit, 1, 0)
            return tuple(accs)
        accs = lax.fori_loop(0, nblk, body, tuple(jnp.zeros((SUBLANES, LANES), I32) for _ in range(nacc)))
        tot = (accs[0] + accs[1]) + (accs[2] + accs[3])
        return jnp.sum(tot, axis=0, keepdims=True)

    def bcast(v):
        return jnp.broadcast_to(v, (SUBLANES, LANES))

    def count_ge(probe):
        probe_b = bcast(probe)
        return count(lambda rows, row0: rows >= probe_b)

    n = count_ge(jnp.zeros((1, LANES), F32))
    ok = n >= TOPK
    thr_k = jnp.where(ok, 0, INT_MIN).astype(I32)
    n_ge = jnp.where(ok, n, INT_MAX).astype(I32)

    def bit_body(t, carry):
        thr_k, n_ge = carry
        cand_k = thr_k | lax.shift_left(jnp.int32(1), 30 - t)
        n = count_ge(_key_float(cand_k))
        ok = n >= TOPK
        return jnp.where(ok, cand_k, thr_k), jnp.where(ok, n, n_ge)

    thr_k, n_ge = lax.fori_loop(0, 31, bit_body, (thr_k, n_ge))
    found = thr_k > NEG_INF_KEY
    thr = jnp.where(found, _key_float(thr_k), -F32_MAX)
    thr_b = bcast(thr)
    tie = found & (n_ge > TOPK)
    ithr_ref[...] = jnp.full((SUBLANES, LANES), INT_MAX, I32)

    @pl.when(jnp.max(tie.astype(I32)) > 0)
    def _():
        n_gt = count(lambda rows, row0: rows > thr_b)
        need = TOPK - n_gt

        def idx_body(t, ithr):
            cand = ithr | lax.shift_left(jnp.int32(1), 11 - t)
            cand_b = bcast(cand)
            n = count(lambda rows, row0: (rows == thr_b) & ((row0 + row8) < cand_b))
            return jnp.where(n < need, cand, ithr)

        ithr = lax.fori_loop(0, 12, idx_body, jnp.zeros((1, LANES), I32))
        ithr_ref[...] = jnp.where(bcast(tie), bcast(ithr), ithr_ref[...])

    thr_q = jnp.broadcast_to(thr, (LANES, LANES))
    ithr_q = jnp.broadcast_to(ithr_ref[0:1, :], (LANES, LANES))
    rowq = lax.broadcasted_iota(I32, (LANES, LANES), 0)

    def bias_body(c, carry):
        for j in range(blk // LANES):
            row0 = pl.multiple_of(c * blk + j * LANES, LANES)
            sc = score_ref[pl.ds(row0, LANES), :]
            sel = (sc > thr_q) | ((sc == thr_q) & ((row0 + rowq) <= ithr_q))
            bias_ref[:, pl.ds(row0, LANES)] = jnp.where(sel, 0.0, NEG_BIG).astype(F32).T
        return carry

    lax.fori_loop(0, nblk, bias_body, 0)


def _dsa_prompt_kernel(q_ref, qi_ref, kiw_ref, kb_ref, vb_ref, kib_ref, o_ref,
                       score_ref, bias_ref, ithr_ref, qir_ref, s_ref, mlane_ref, llane_ref, acc_ref):
    i = pl.program_id(1)
    nchunk = (i * TQ + TQ + TK - 1) // TK
    qpos = i * TQ + lax.broadcasted_iota(I32, (TK, LANES), 1)
    krow = lax.broadcasted_iota(I32, (TK, LANES), 0)

    for h in range(IDX_HEADS):
        qir_ref[h * TQ:(h + 1) * TQ, :] = qi_ref[0, :, h * IDX_DIM:(h + 1) * IDX_DIM]
    kiw_t = kiw_ref[0].T
    w_rows = [kiw_t[IDX_DIM + h:IDX_DIM + h + 1, :] * IDX_HEAD_SCALE for h in range(IDX_HEADS)]

    def idx_body(c, carry):
        c0 = pl.multiple_of(c * TK, TK)
        logit = _dot_nt(kib_ref[0, pl.ds(c0, TK), :], qir_ref[...])
        score = jnp.zeros((TK, LANES), F32)
        for h in range(IDX_HEADS):
            score = score + jnp.maximum(logit[:, h * TQ:(h + 1) * TQ], 0.0) * w_rows[h]
        score_ref[pl.ds(c0, TK), :] = jnp.where((c0 + krow) <= qpos, score, -jnp.inf)
        return carry

    lax.fori_loop(0, nchunk, idx_body, 0)
    _select_bias_t(score_ref, bias_ref, ithr_ref, nchunk, TK)

    def q_group(g):
        return jnp.concatenate(
            [q_ref[0, :, (g * GROUP + r) * HEAD_DIM:(g * GROUP + r + 1) * HEAD_DIM] for r in range(GROUP)], axis=0)

    mlane_ref[...] = jnp.full(mlane_ref.shape, NEG_BIG, F32)

    def logits_body(c, carry):
        c0 = pl.multiple_of(c * TK, TK)
        bias4 = jnp.concatenate([bias_ref[:, pl.ds(c0, TK)]] * GROUP, axis=0)
        for g in range(N_KV_HEADS):
            s = _dot_nt(q_group(g), kb_ref[0, pl.ds(c0, TK), g * HEAD_DIM:(g + 1) * HEAD_DIM]) + bias4
            s_ref[g, :, pl.ds(c0, TK)] = s
            m = mlane_ref[g]
            for j in range(TK // LANES):
                m = jnp.maximum(m, s[:, j * LANES:(j + 1) * LANES])
            mlane_ref[g] = m
        return carry

    lax.fori_loop(0, nchunk, logits_body, 0)

    for g in range(N_KV_HEADS):
        mlane_ref[g] = jnp.broadcast_to(jnp.max(mlane_ref[g], axis=1, keepdims=True), mlane_ref.shape[1:])
    llane_ref[...] = jnp.zeros(llane_ref.shape, F32)
    acc_ref[...] = jnp.zeros(acc_ref.shape, F32)

    def pv_body(c, carry):
        c0 = pl.multiple_of(c * TK, TK)
        for g in range(N_KV_HEADS):
            m = mlane_ref[g]
            l = llane_ref[g]
            ps = []
            for j in range(TK // LANES):
                pj = jnp.exp2(s_ref[g, :, pl.ds(pl.multiple_of(c0 + j * LANES, LANES), LANES)] - m)
                l = l + pj
                ps.append(pj.astype(BF16))
            llane_ref[g] = l
            p = jnp.concatenate(ps, axis=1)
            acc_ref[g] += _dot(p, vb_ref[0, pl.ds(c0, TK), g * HEAD_DIM:(g + 1) * HEAD_DIM])
        return carry

    lax.fori_loop(0, nchunk, pv_body, 0)

    for g in range(N_KV_HEADS):
        og = acc_ref[g] * (1.0 / jnp.sum(llane_ref[g], axis=1, keepdims=True))
        for r in range(GROUP):
            col = (g * GROUP + r) * HEAD_DIM
            o_ref[0, :, col:col + HEAD_DIM] = og[r * TQ:(r + 1) * TQ].astype(BF16)


def _dsa_prompt_call(q, qi, kiw, kb, vb, kib):
    nb, s, _ = q.shape
    tile = lambda w: pl.BlockSpec((1, TQ, w), lambda b, i: (b, i, 0))
    full = lambda w: pl.BlockSpec((1, s, w), lambda b, i: (b, 0, 0))
    rows = GROUP * TQ
    return pl.pallas_call(
        _dsa_prompt_kernel,
        grid=(nb, s // TQ),
        in_specs=[tile(Q_W), tile(QI_W), tile(KIW_W), full(KV_W), full(KV_W), full(IDX_DIM)],
        out_specs=tile(Q_W),
        out_shape=jax.ShapeDtypeStruct((nb, s, Q_W), BF16),
        scratch_shapes=[
            pltpu.VMEM((s, TQ), F32),
            pltpu.VMEM((TQ, s), F32),
            pltpu.VMEM((SUBLANES, LANES), I32),
            pltpu.VMEM((IDX_HEADS * TQ, IDX_DIM), BF16),
            pltpu.VMEM((N_KV_HEADS, rows, s), F32),
            pltpu.VMEM((N_KV_HEADS, rows, LANES), F32),
            pltpu.VMEM((N_KV_HEADS, rows, LANES), F32),
            pltpu.VMEM((N_KV_HEADS, rows, HEAD_DIM), F32),
        ],
        compiler_params=_params(("arbitrary", "arbitrary")),
        name="dsa_prompt",
    )(q, qi, kiw, kb, vb, kib)


def _page_specs(page_shape, nseq):
    zeros = (0,) * len(page_shape)
    return [pl.BlockSpec((1,) + page_shape,
                         functools.partial(lambda i, pt, u, p: (pt[i * nseq + u, p],) + zeros, u=u, p=p))
            for u in range(nseq) for p in range(N_PAGES)]


def _kv_page(page_ref):
    heads = [page_ref[0, pl.ds(g, PAGE_SIZE, stride=N_KV_HEADS), :] for g in range(N_KV_HEADS)]
    return jnp.concatenate(heads, axis=1).astype(BF16)


def _pad_new_page(new8):
    return jnp.concatenate([new8, jnp.zeros((PAGE_SIZE - Q8, new8.shape[1]), F32)], axis=0).astype(BF16)


def _idx_sample_kernel(pt_ref, qis_ref, ws_ref, kin_ref, *rest):
    nseq = qis_ref.shape[0]
    pages = rest[:nseq * N_PAGES]
    o_ref = rest[nseq * N_PAGES]
    del pt_ref
    for u in range(nseq):
        qs = qis_ref[u]
        w = ws_ref[u]

        def page_score(kt):
            r = jnp.maximum(_dot(qs, kt.astype(BF16)), 0.0) * w
            s = r[0:Q8]
            for h in range(1, IDX_HEADS):
                s = s + r[h * Q8:(h + 1) * Q8]
            return s

        for p in range(N_PAGES):
            o_ref[u, :, p * PAGE_SIZE:(p + 1) * PAGE_SIZE] = page_score(pages[u * N_PAGES + p][0])
        o_ref[u, :, PAST_LEN:L_SAMPLE] = page_score(kin_ref[u])


def _idx_sample_call(page_table, qis, ws, kinew_t, cache_ki_t):
    n = qis.shape[0]
    rows = IDX_HEADS * Q8
    nseq = SCORE_SEQS
    grid_spec = pltpu.PrefetchScalarGridSpec(
        num_scalar_prefetch=1,
        grid=(n // nseq,),
        in_specs=[
            pl.BlockSpec((nseq, rows, IDX_DIM), lambda i, pt: (i, 0, 0)),
            pl.BlockSpec((nseq, rows, 1), lambda i, pt: (i, 0, 0)),
            pl.BlockSpec((nseq, IDX_DIM, PAGE_SIZE), lambda i, pt: (i, 0, 0)),
        ] + _page_specs((IDX_DIM, PAGE_SIZE), nseq),
        out_specs=pl.BlockSpec((nseq, Q8, L_SAMPLE), lambda i, pt: (i, 0, 0)),
    )
    return pl.pallas_call(
        _idx_sample_kernel,
        grid_spec=grid_spec,
        out_shape=jax.ShapeDtypeStruct((n, Q8, L_SAMPLE), F32),
        compiler_params=_params(("arbitrary",)),
        name="dsa_sample_scores",
    )(page_table, qis, ws, kinew_t, *([cache_ki_t] * (nseq * N_PAGES)))


def _select_sample_kernel(s_ref, bias_ref, score_ref, ithr_ref):
    krow = lax.broadcasted_iota(I32, (LANES, SEL_ROWS), 0)
    qrow = lax.broadcasted_iota(I32, (LANES, SEL_ROWS), 1) % Q8
    real = qrow < T_NEW
    for j in range(L_SAMPLE // LANES):
        kidx = j * LANES + krow
        valid = ((kidx < PAST_LEN) | ((kidx - PAST_LEN) <= qrow)) & real
        score_ref[j * LANES:(j + 1) * LANES, :] = jnp.where(valid, s_ref[:, j * LANES:(j + 1) * LANES].T, -jnp.inf)
    _select_bias_t(score_ref, bias_ref, ithr_ref, L_SAMPLE // LANES, LANES)


def _select_sample_call(scores):
    rows = scores.shape[0]
    return pl.pallas_call(
        _select_sample_kernel,
        grid=(rows // SEL_ROWS,),
        in_specs=[pl.BlockSpec((SEL_ROWS, L_SAMPLE), lambda i: (i, 0))],
        out_specs=pl.BlockSpec((SEL_ROWS, L_SAMPLE), lambda i: (i, 0)),
        out_shape=jax.ShapeDtypeStruct((rows, L_SAMPLE), F32),
        scratch_shapes=[pltpu.VMEM((L_SAMPLE, SEL_ROWS), F32), pltpu.VMEM((SUBLANES, LANES), I32)],
        compiler_params=_params(("arbitrary",)),
        name="dsa_sample_select",
    )(scores)


def _attn_sample_kernel(pt_ref, qb_ref, bias_ref, kn_ref, vn_ref, *rest):
    nseq = qb_ref.shape[0]
    kpages = rest[:nseq * N_PAGES]
    vpages = rest[nseq * N_PAGES:2 * nseq * N_PAGES]
    o_ref = rest[2 * nseq * N_PAGES]
    s_ref = rest[2 * nseq * N_PAGES + 1]
    del pt_ref
    for u in range(nseq):
        qb = qb_ref[u]
        reps = qb.shape[0] // Q8

        def page_logits(p, kp):
            bias8 = bias_ref[u, :, p * PAGE_SIZE:(p + 1) * PAGE_SIZE]
            s_ref[u, :, p * PAGE_SIZE:(p + 1) * PAGE_SIZE] = (
                _dot_nt(qb, kp) + jnp.concatenate([bias8] * reps, axis=0))

        for p in range(N_PAGES):
            page_logits(p, _kv_page(kpages[u * N_PAGES + p]))
        page_logits(N_PAGES, _pad_new_page(kn_ref[u]))

        s = s_ref[u]
        m = jnp.max(s, axis=1, keepdims=True)
        e = jnp.exp2(s - m)
        l = jnp.sum(e, axis=1, keepdims=True)
        eb = e.astype(BF16)
        acc = _dot(eb[:, PAST_LEN:L_SAMPLE], _pad_new_page(vn_ref[u]))
        for p in range(N_PAGES):
            acc = acc + _dot(eb[:, p * PAGE_SIZE:(p + 1) * PAGE_SIZE], _kv_page(vpages[u * N_PAGES + p]))
        o_ref[u] = acc * (1.0 / l)


def _attn_sample_call(page_table, qblk, bias, knew8, vnew8, cache_k, cache_v):
    n, rows, _ = qblk.shape
    nseq = ATTN_SEQS
    page = (PAGE_SIZE * N_KV_HEADS, HEAD_DIM)
    grid_spec = pltpu.PrefetchScalarGridSpec(
        num_scalar_prefetch=1,
        grid=(n // nseq,),
        in_specs=[
            pl.BlockSpec((nseq, rows, KV_W), lambda i, pt: (i, 0, 0)),
            pl.BlockSpec((nseq, Q8, L_SAMPLE), lambda i, pt: (i, 0, 0)),
            pl.BlockSpec((nseq, Q8, KV_W), lambda i, pt: (i, 0, 0)),
            pl.BlockSpec((nseq, Q8, KV_W), lambda i, pt: (i, 0, 0)),
        ] + _page_specs(page, nseq) + _page_specs(page, nseq),
        out_specs=pl.BlockSpec((nseq, rows, KV_W), lambda i, pt: (i, 0, 0)),
        scratch_shapes=[pltpu.VMEM((nseq, rows, L_SAMPLE), F32)],
    )
    return pl.pallas_call(
        _attn_sample_kernel,
        grid_spec=grid_spec,
        out_shape=jax.ShapeDtypeStruct((n, rows, KV_W), F32),
        compiler_params=_params(("arbitrary",)),
        name="dsa_sample_attn",
    )(page_table, qblk, bias, knew8, vnew8, *([cache_k] * (nseq * N_PAGES)), *([cache_v] * (nseq * N_PAGES)))


def _mlp_resid(x1, g_ffn, shift, scale, gate, wup_ref, wdn_ref, acc_ref):
    h = _rms_mod(x1, g_ffn, shift, scale).astype(BF16)
    for c in range(D_FF // FF_CHUNK):
        u = _dot(h, wup_ref[:, c * FF_CHUNK:(c + 1) * FF_CHUNK])
        u = jnp.square(jnp.maximum(u, 0.0)).astype(BF16)
        d = _dot(u, wdn_ref[c * FF_CHUNK:(c + 1) * FF_CHUNK, :])
        if c == 0:
            acc_ref[...] = d
        else:
            acc_ref[...] += d
    return x1 + gate * acc_ref[...]


def _post0_kernel(x_ref, o_ref, g1_ref, sh2_ref, sc2_ref, g2_ref, gffn_ref, wo_ref, wup_ref, wdn_ref,
                  out_ref, acc_ref):
    x1 = x_ref[0] + g1_ref[0] * _dot(o_ref[0], wo_ref[...])
    out_ref[0] = _mlp_resid(x1, gffn_ref[...], sh2_ref[0], sc2_ref[0], g2_ref[0], wup_ref, wdn_ref, acc_ref)


def _const_spec(shape):
    nd = len(shape)
    return pl.BlockSpec(shape, lambda b, i: (0,) * nd)


def _post0_call(x, o, g1, sh2, sc2, g2, g_ffn, w_o, w_up, w_dn, tm):
    nb, t, _ = x.shape
    tok = lambda w: pl.BlockSpec((1, tm, w), lambda b, i: (b, i, 0))
    return pl.pallas_call(
        _post0_kernel,
        grid=(nb, t // tm),
        in_specs=[tok(D_MODEL), tok(Q_W), g1.spec(tm), sh2.spec(tm), sc2.spec(tm),
                  g2.spec(tm), _const_spec((1, D_MODEL)), _const_spec((Q_W, D_MODEL)),
                  _const_spec((D_MODEL, D_FF)), _const_spec((D_FF, D_MODEL))],
        out_specs=tok(D_MODEL),
        out_shape=jax.ShapeDtypeStruct((nb, t, D_MODEL), F32),
        scratch_shapes=[pltpu.VMEM((tm, D_MODEL), F32)],
        compiler_params=_params(("arbitrary", "arbitrary")),
        name="layer0_out",
    )(x, o, g1.arr, sh2.arr, sc2.arr, g2.arr, g_ffn.reshape(1, D_MODEL), w_o, w_up, w_dn)


def _pool_project(pooled, wp_ref, ps_ref):
    parts = []
    for g in range(len(POOL_WINDOWS)):
        lo = g * POOL_GROUP_DIM
        parts.append(_dot(pooled[:, lo:lo + POOL_GROUP_DIM].astype(BF16), wp_ref[g]))
    return jnp.concatenate(parts, axis=1) * ps_ref[...]


def _layer1_prompt_kernel(x_ref, xh_ref, sh1_ref, sc1_ref, g1_ref, sh2_ref, sc2_ref, g2_ref,
                          gmix_ref, gffn_ref, gfin_ref, wp_ref, ps_ref, wup_ref, wdn_ref,
                          y_ref, hlast_ref, acc_ref):
    i = pl.program_id(1)
    tm = x_ref.shape[1]
    x = x_ref[0]
    sh1, sc1 = sh1_ref[0], sc1_ref[0]
    h = _rms_mod(x, gmix_ref[...], sh1, sc1)
    h_halo = _rms_mod(xh_ref[0], gmix_ref[...], sh1, sc1)
    h_halo = jnp.where(i > 0, h_halo, 0.0)
    ext = jnp.concatenate([h_halo, h], axis=0)
    pos = i * tm + lax.broadcasted_iota(I32, (tm, 1), 0)
    parts = []
    for g, w in enumerate(POOL_WINDOWS):
        lo = g * POOL_GROUP_DIM
        s = ext[:, lo:lo + POOL_GROUP_DIM]
        step = 1
        while step < w:
            s = s + pltpu.roll(s, step, axis=0)
            step *= 2
        cnt = jnp.minimum(w, pos + 1).astype(F32)
        parts.append(s[HALO:, :] / cnt - h[:, lo:lo + POOL_GROUP_DIM])
    pooled = jnp.concatenate(parts, axis=1)
    x1 = x + g1_ref[0] * _pool_project(pooled, wp_ref, ps_ref)
    x2 = _mlp_resid(x1, gffn_ref[...], sh2_ref[0], sc2_ref[0], g2_ref[0], wup_ref, wdn_ref, acc_ref)
    y_ref[0] = _rmsnorm(x2, gfin_ref[...])

    @pl.when(i == pl.num_programs(1) - 1)
    def _():
        hlast_ref[0] = h[tm - HALO:, :]


def _layer1_prompt_call(x, mods, g_mix, g_ffn, g_fin, pool_w, pool_scale, w_up, w_dn, tm):
    nb, t, _ = x.shape
    tok = pl.BlockSpec((1, tm, D_MODEL), lambda b, i: (b, i, 0))
    halo = pl.BlockSpec((1, HALO, D_MODEL), lambda b, i: (b, jnp.maximum(i * (tm // HALO) - 1, 0), 0))
    vec = _const_spec((1, D_MODEL))
    return pl.pallas_call(
        _layer1_prompt_kernel,
        grid=(nb, t // tm),
        in_specs=[tok, halo] + [m.spec(tm) for m in mods] + [vec, vec, vec,
                  _const_spec(pool_w.shape), vec, _const_spec((D_MODEL, D_FF)), _const_spec((D_FF, D_MODEL))],
        out_specs=[tok, pl.BlockSpec((1, HALO, D_MODEL), lambda b, i: (b, 0, 0))],
        out_shape=[jax.ShapeDtypeStruct((nb, t, D_MODEL), F32), jax.ShapeDtypeStruct((nb, HALO, D_MODEL), F32)],
        scratch_shapes=[pltpu.VMEM((tm, D_MODEL), F32)],
        compiler_params=_params(("arbitrary", "arbitrary")),
        name="layer1_prompt",
    )(x, x, *[m.arr for m in mods], g_mix.reshape(1, D_MODEL), g_ffn.reshape(1, D_MODEL),
      g_fin.reshape(1, D_MODEL), pool_w, pool_scale.reshape(1, D_MODEL), w_up, w_dn)


def _layer1_sample_kernel(x_ref, prev_ref, sh1_ref, sc1_ref, g1_ref, sh2_ref, sc2_ref, g2_ref,
                          gmix_ref, gffn_ref, gfin_ref, wp_ref, ps_ref, wup_ref, wdn_ref,
                          y_ref, h_ref, hs_ref, pooled_ref, acc_ref):
    tm = x_ref.shape[1]
    t_new = tm // prev_ref.shape[1]
    nseq = prev_ref.shape[1]
    x = x_ref[0]
    h = _rms_mod(x, gmix_ref[...], sh1_ref[0], sc1_ref[0])
    h_ref[0] = h
    ncol = D_MODEL // LANES
    for c in range(ncol):
        hs_ref[c] = h[:, c * LANES:(c + 1) * LANES]
    ext = [prev_ref[j] for j in range(POOL_STATE)]
    ext += [jnp.concatenate([hs_ref[c, pl.ds(t, nseq, stride=t_new), :] for c in range(ncol)], axis=1)
            for t in range(t_new)]
    for t in range(t_new):
        parts = []
        for g, w in enumerate(POOL_WINDOWS):
            lo = g * POOL_GROUP_DIM
            s = ext[POOL_STATE + t][:, lo:lo + POOL_GROUP_DIM]
            for j in range(1, w):
                s = s + ext[POOL_STATE + t - j][:, lo:lo + POOL_GROUP_DIM]
            parts.append(s / float(w) - ext[POOL_STATE + t][:, lo:lo + POOL_GROUP_DIM])
        pooled_t = jnp.concatenate(parts, axis=1)
        for c in range(ncol):
            pooled_ref[c, pl.ds(t, nseq, stride=t_new), :] = pooled_t[:, c * LANES:(c + 1) * LANES]
    pooled = jnp.concatenate([pooled_ref[c] for c in range(ncol)], axis=1)
    x1 = x + g1_ref[0] * _pool_project(pooled, wp_ref, ps_ref)
    x2 = _mlp_resid(x1, gffn_ref[...], sh2_ref[0], sc2_ref[0], g2_ref[0], wup_ref, wdn_ref, acc_ref)
    y_ref[0] = _rmsnorm(x2, gfin_ref[...])


def _layer1_sample_call(x, prev_t, mods, g_mix, g_ffn, g_fin, pool_w, pool_scale, w_up, w_dn, tm, t_new):
    nb, t, _ = x.shape
    tok = pl.BlockSpec((1, tm, D_MODEL), lambda b, i: (b, i, 0))
    prev = pl.BlockSpec((POOL_STATE, tm // t_new, D_MODEL), lambda b, i: (0, i, 0))
    vec = _const_spec((1, D_MODEL))
    return pl.pallas_call(
        _layer1_sample_kernel,
        grid=(nb, t // tm),
        in_specs=[tok, prev] + [m.spec(tm) for m in mods] + [vec, vec, vec,
                  _const_spec(pool_w.shape), vec, _const_spec((D_MODEL, D_FF)), _const_spec((D_FF, D_MODEL))],
        out_specs=[tok, tok],
        out_shape=[jax.ShapeDtypeStruct((nb, t, D_MODEL), F32), jax.ShapeDtypeStruct((nb, t, D_MODEL), F32)],
        scratch_shapes=[pltpu.VMEM((D_MODEL // LANES, tm, LANES), F32),
                        pltpu.VMEM((D_MODEL // LANES, tm, LANES), F32),
                        pltpu.VMEM((tm, D_MODEL), F32)],
        compiler_params=_params(("arbitrary", "arbitrary")),
        name="layer1_sample",
    )(x, prev_t, *[m.arr for m in mods], g_mix.reshape(1, D_MODEL), g_ffn.reshape(1, D_MODEL),
      g_fin.reshape(1, D_MODEL), pool_w, pool_scale.reshape(1, D_MODEL), w_up, w_dn)


def kernel(x_prompt, x_sample, cache_k, cache_v, cache_kidx, state_pool, page_table, c_prompt, c_sample,
           norm_mix_g, norm_ffn_g, w_mod, b_mod, dsa_w_in, dsa_w_o, pool_w, pool_scale, w_up, w_down,
           final_norm_g):
    nbp, seq, _ = x_prompt.shape
    nbs, t_new, _ = x_sample.shape
    n_tok_s = nbs * t_new

    w_in_b = jnp.pad(dsa_w_in[0], ((0, 0), (0, D_PROJ_PAD - D_PROJ))).astype(BF16)
    w_o_b = dsa_w_o[0].astype(BF16)
    w_up_b = w_up.astype(BF16)
    w_dn_b = w_down.astype(BF16)
    pool_w_b = pool_w[0].astype(BF16)

    n_c = nbp + nbs
    c_rows = -(-n_c // SUBLANES) * SUBLANES
    c_all = jnp.pad(jnp.concatenate([c_prompt, c_sample], axis=0), ((0, c_rows - n_c), (0, 0)))
    mod, mod_tok = _mod_call(c_all, w_mod, b_mod, nbp, nbs, t_new)

    def mods_prompt(layer):
        m = mod[layer, :nbp].reshape(nbp, 1, 6, D_MODEL)
        return [_mod_per_sequence(m[:, :, j]) for j in range(6)]

    def mods_sample(layer):
        return [_mod_per_token(mod_tok, layer, j) for j in range(6)]

    sh1, sc1, g1, sh2, sc2, g2 = mods_prompt(0)
    k_p, v_p, ki_p, q, qi, kiw, kb, vb, kib = _proj_call(x_prompt, sh1, sc1, norm_mix_g[0], w_in_b, 512)
    o = _dsa_prompt_call(q, qi, kiw, kb, vb, kib)
    x1 = _post0_call(x_prompt, o, g1, sh2, sc2, g2, norm_ffn_g[0], w_o_b, w_up_b[0], w_dn_b[0], 512)
    y_prompt, hlast = _layer1_prompt_call(x1, mods_prompt(1), norm_mix_g[1], norm_ffn_g[1], final_norm_g,
                                          pool_w_b, pool_scale[0], w_up_b[1], w_dn_b[1], 512)
    pool_p = hlast[:, HALO - POOL_STATE:]

    sh1, sc1, g1, sh2, sc2, g2 = mods_sample(0)
    xs = x_sample.reshape(1, n_tok_s, D_MODEL)
    tms = 256
    k_s, v_s, ki_s, q, qi, kiw, _, _, _ = _proj_call(xs, sh1, sc1, norm_mix_g[0], w_in_b, tms)

    pad_q = lambda a: jnp.pad(a, [(0, 0)] * (a.ndim - 2) + [(0, Q8 - t_new), (0, 0)])
    qis = pad_q(qi.reshape(nbs, t_new, IDX_HEADS, IDX_DIM).transpose(0, 2, 1, 3)).reshape(nbs, IDX_HEADS * Q8, IDX_DIM)
    wi = kiw.reshape(nbs, t_new, KIW_W)[:, :, IDX_DIM:IDX_DIM + IDX_HEADS] * IDX_HEAD_SCALE
    ws = pad_q(wi.transpose(0, 2, 1)[..., None]).reshape(nbs, IDX_HEADS * Q8, 1)
    knew8 = pad_q(k_s.reshape(nbs, t_new, KV_W))
    vnew8 = pad_q(v_s.reshape(nbs, t_new, KV_W))
    kinew_t = jnp.pad(ki_s.reshape(nbs, t_new, IDX_DIM).transpose(0, 2, 1), ((0, 0), (0, 0), (0, PAGE_SIZE - t_new)))
    scores = _idx_sample_call(page_table, qis, ws, kinew_t, cache_kidx[0].transpose(0, 2, 1))
    bias = _select_sample_call(scores.reshape(nbs * Q8, L_SAMPLE)).reshape(nbs, Q8, L_SAMPLE)
    qg = pad_q(q.reshape(nbs, t_new, N_KV_HEADS, GROUP, HEAD_DIM).transpose(0, 2, 3, 1, 4))
    eye = jnp.eye(N_KV_HEADS, dtype=BF16)
    qblk = (qg[:, :, :, :, None, :] * eye[None, :, None, None, :, None]).reshape(nbs, N_KV_HEADS * GROUP * Q8, KV_W)
    n_pool = cache_k.shape[1]
    page_rows = lambda c: c[0].reshape(n_pool, PAGE_SIZE * N_KV_HEADS, HEAD_DIM)
    o_blk = _attn_sample_call(page_table, qblk, bias, knew8, vnew8, page_rows(cache_k), page_rows(cache_v))
    o_blk = o_blk.reshape(nbs, N_KV_HEADS, GROUP, Q8, N_KV_HEADS, HEAD_DIM)
    o_s = jnp.stack([o_blk[:, g, :, :t_new, g] for g in range(N_KV_HEADS)], axis=1)
    o_s = o_s.transpose(0, 3, 1, 2, 4).reshape(1, n_tok_s, Q_W).astype(BF16)
    x1s = _post0_call(xs, o_s, g1, sh2, sc2, g2, norm_ffn_g[0], w_o_b, w_up_b[0], w_dn_b[0], tms)
    prev_t = state_pool[0].transpose(1, 0, 2)
    y_s, h1s = _layer1_sample_call(x1s, prev_t, mods_sample(1), norm_mix_g[1], norm_ffn_g[1], final_norm_g,
                                   pool_w_b, pool_scale[0], w_up_b[1], w_dn_b[1], tms, t_new)
    y_sample = y_s.reshape(nbs, t_new, D_MODEL)
    pool_s = jnp.concatenate([state_pool[0][:, t_new:], h1s.reshape(nbs, t_new, D_MODEL)], axis=1)

    return (
        y_prompt, y_sample,
        k_p[None], v_p[None], ki_p[None], pool_p[None],
        k_s.reshape(1, nbs, t_new, N_KV_HEADS, HEAD_DIM), v_s.reshape(1, nbs, t_new, N_KV_HEADS, HEAD_DIM),
        ki_s.reshape(1, nbs, t_new, IDX_DIM), pool_s[None],
    )
```

```python
import functools
from typing import Callable, NamedTuple

import jax
import jax.numpy as jnp
from jax import lax
from jax.experimental import pallas as pl
from jax.experimental.pallas import tpu as pltpu

F32 = jnp.float32
BF16 = jnp.bfloat16
I32 = jnp.int32

D_MODEL = 1024
DEPTH = 2
PAST_LEN = 2048
PAGE_SIZE = 128
N_PAGES = PAST_LEN // PAGE_SIZE
N_HEADS = 8
HEAD_DIM = 128
N_KV_HEADS = 2
GROUP = N_HEADS // N_KV_HEADS
IDX_HEADS = 8
IDX_DIM = 64
TOPK = 256
POOL_WINDOWS = (2, 4, 8, 16)
POOL_GROUP_DIM = D_MODEL // len(POOL_WINDOWS)
POOL_STATE = max(POOL_WINDOWS) - 1
D_FF = 4 * D_MODEL
RMS_EPS = 1e-6
Q_W = N_HEADS * HEAD_DIM
KV_W = N_KV_HEADS * HEAD_DIM
QI_W = IDX_HEADS * IDX_DIM
D_PROJ = Q_W + 2 * KV_W + QI_W + IDX_DIM + IDX_HEADS

LANES = 128
SUBLANES = 8
VMEM_LIMIT_BYTES = 56 * 1024 * 1024

KIW_W = LANES
D_PROJ_PAD = Q_W + 2 * KV_W + QI_W + KIW_W
OFF_K = Q_W
OFF_V = OFF_K + KV_W
OFF_QI = OFF_V + KV_W
OFF_KIW = OFF_QI + QI_W

LOG2_E = 1.4426950408889634
ATT_SCALE = HEAD_DIM ** -0.5 * LOG2_E
IDX_SCALE = IDX_DIM ** -0.5
IDX_HEAD_SCALE = IDX_HEADS ** -0.5

INT_MIN = -(2 ** 31)
INT_MAX = 2 ** 31 - 1
F32_MAX = 3.4028234663852886e38
NEG_INF_KEY = INT_MIN + (1 << 23)
NEG_BIG = -1e30

TQ = 128
TK = 512
FF_CHUNK = 512
HALO = 16
SEL_ROWS = 128
L_SAMPLE = PAST_LEN + PAGE_SIZE
T_NEW = 4
SCORE_SEQS = 4
ATTN_SEQS = 2
Q8 = SUBLANES


def _params(sem):
    return pltpu.CompilerParams(dimension_semantics=sem, vmem_limit_bytes=VMEM_LIMIT_BYTES)


def _dot(a, b):
    return jnp.dot(a, b, preferred_element_type=F32)


def _dot_nt(a, b):
    return lax.dot_general(a, b, (((1,), (1,)), ((), ())), preferred_element_type=F32)


def _rms_mod(x, g, shift, scale):
    ms = jnp.mean(x * x, axis=-1, keepdims=True)
    y = x * lax.rsqrt(ms + RMS_EPS) * g
    return y * (1.0 + scale) + shift


def _rmsnorm(x, g):
    ms = jnp.mean(x * x, axis=-1, keepdims=True)
    return x * lax.rsqrt(ms + RMS_EPS) * g


def _split3(x):
    hi = x.astype(BF16)
    r1 = x - hi.astype(F32)
    mid = r1.astype(BF16)
    lo = (r1 - mid.astype(F32)).astype(BF16)
    return hi, mid, lo


def _mod_kernel(c_ref, w_ref, b_ref, o_ref, tok_ref, *, n_prompt, t_new):
    c = c_ref[...]
    a = c * (1.0 / (1.0 + jnp.exp(-c)))
    w = w_ref[0]
    a_hi = a.astype(BF16)
    a_lo = (a - a_hi.astype(F32)).astype(BF16)
    w_hi = w.astype(BF16)
    w_lo = (w - w_hi.astype(F32)).astype(BF16)
    mod = _dot(a_hi, w_hi) + _dot(a_lo, w_hi) + _dot(a_hi, w_lo) + b_ref[0]
    o_ref[0] = mod
    n_tok, rows = tok_ref.shape[1], mod.shape[0]
    r = lax.broadcasted_iota(I32, (n_tok, rows), 0)
    s = (lax.broadcasted_iota(I32, (n_tok, rows), 1) - n_prompt) * t_new
    pick = jnp.where((r >= s) & (r < s + t_new), 1.0, 0.0).astype(BF16)
    hi, mid, lo = _split3(mod)
    tok_ref[0] = (_dot(pick, hi) + _dot(pick, mid)) + _dot(pick, lo)


def _mod_call(c_all, w_mod, b_mod, n_prompt, n_sample, t_new):
    rows = c_all.shape[0]
    tn = 1536
    return pl.pallas_call(
        functools.partial(_mod_kernel, n_prompt=n_prompt, t_new=t_new),
        grid=(DEPTH, 6 * D_MODEL // tn),
        in_specs=[
            pl.BlockSpec((rows, D_MODEL), lambda i, j: (0, 0)),
            pl.BlockSpec((1, D_MODEL, tn), lambda i, j: (i, 0, j)),
            pl.BlockSpec((1, 1, tn), lambda i, j: (i, 0, j)),
        ],
        out_specs=[pl.BlockSpec((1, rows, tn), lambda i, j: (i, 0, j)),
                   pl.BlockSpec((1, n_sample * t_new, tn), lambda i, j: (i, 0, j))],
        out_shape=[jax.ShapeDtypeStruct((DEPTH, rows, 6 * D_MODEL), F32),
                   jax.ShapeDtypeStruct((DEPTH, n_sample * t_new, 6 * D_MODEL), F32)],
        compiler_params=_params(("arbitrary", "arbitrary")),
        name="mod_vectors",
    )(c_all, w_mod, b_mod.reshape(DEPTH, 1, 6 * D_MODEL))


def _proj_kernel(x_ref, sh_ref, sc_ref, g_ref, w_ref,
                 k_ref, v_ref, ki_ref, q_ref, qi_ref, kiw_ref, kb_ref, vb_ref, kib_ref):
    h = _rms_mod(x_ref[0], g_ref[...], sh_ref[0], sc_ref[0]).astype(BF16)
    q_ref[0] = (_dot(h, w_ref[:, 0:Q_W]) * ATT_SCALE).astype(BF16)
    k = _dot(h, w_ref[:, OFF_K:OFF_K + KV_W])
    v = _dot(h, w_ref[:, OFF_V:OFF_V + KV_W])
    for g in range(N_KV_HEADS):
        k_ref[0, :, g, :] = k[:, g * HEAD_DIM:(g + 1) * HEAD_DIM]
        v_ref[0, :, g, :] = v[:, g * HEAD_DIM:(g + 1) * HEAD_DIM]
    kb_ref[0] = k.astype(BF16)
    vb_ref[0] = v.astype(BF16)
    qi_ref[0] = (_dot(h, w_ref[:, OFF_QI:OFF_QI + QI_W]) * IDX_SCALE).astype(BF16)
    kiw = _dot(h, w_ref[:, OFF_KIW:OFF_KIW + KIW_W])
    kiw_ref[0] = kiw
    ki = kiw[:, 0:IDX_DIM]
    ki_ref[0] = ki
    kib_ref[0] = ki.astype(BF16)


class _Mod(NamedTuple):
    arr: jax.Array
    spec: Callable[[int], pl.BlockSpec]


def _mod_per_sequence(arr):
    return _Mod(arr, lambda tm: pl.BlockSpec((1, 1, D_MODEL), lambda b, i: (b, 0, 0)))


def _mod_per_token(mod_tok, layer, j):
    return _Mod(mod_tok, lambda tm: pl.BlockSpec((1, tm, D_MODEL), lambda b, i: (layer, i, j)))


def _proj_call(x, shift, scale, g, w_in_b, tm):
    nb, t, _ = x.shape
    tok = lambda w: pl.BlockSpec((1, tm, w), lambda b, i: (b, i, 0))
    shp = lambda w, dt: jax.ShapeDtypeStruct((nb, t, w), dt)
    heads = pl.BlockSpec((1, tm, N_KV_HEADS, HEAD_DIM), lambda b, i: (b, i, 0, 0))
    heads_shape = jax.ShapeDtypeStruct((nb, t, N_KV_HEADS, HEAD_DIM), F32)
    return pl.pallas_call(
        _proj_kernel,
        grid=(nb, t // tm),
        in_specs=[
            tok(D_MODEL), shift.spec(tm), scale.spec(tm),
            pl.BlockSpec((1, D_MODEL), lambda b, i: (0, 0)),
            pl.BlockSpec((D_MODEL, D_PROJ_PAD), lambda b, i: (0, 0)),
        ],
        out_specs=[heads, heads, tok(IDX_DIM), tok(Q_W), tok(QI_W), tok(KIW_W),
                   tok(KV_W), tok(KV_W), tok(IDX_DIM)],
        out_shape=[heads_shape, heads_shape, shp(IDX_DIM, F32), shp(Q_W, BF16), shp(QI_W, BF16),
                   shp(KIW_W, F32), shp(KV_W, BF16), shp(KV_W, BF16), shp(IDX_DIM, BF16)],
        compiler_params=_params(("arbitrary", "arbitrary")),
        name="dsa_proj",
    )(x, shift.arr, scale.arr, g.reshape(1, D_MODEL), w_in_b)


def _key_float(k):
    return lax.bitcast_convert_type(jnp.where(k >= 0, k, (k - 1) ^ INT_MAX), F32)


def _select_bias_t(score_ref, bias_ref, ithr_ref, nblk, blk):
    nsub = blk // SUBLANES
    nacc = 4
    row8 = lax.broadcasted_iota(I32, (SUBLANES, LANES), 0)

    def count(pred):
        def body(c, accs):
            accs = list(accs)
            c0 = pl.multiple_of(c * blk, blk)
            rows = score_ref[pl.ds(c0, blk), :]
            for j in range(nsub):
                hit = pred(rows[j * SUBLANES:(j + 1) * SUBLANES], c0 + j * SUBLANES)
                accs[j % nacc] = accs[j % nacc] + jnp.where(hit, 1, 0)
            return tuple(accs)
        accs = lax.fori_loop(0, nblk, body, tuple(jnp.zeros((SUBLANES, LANES), I32) for _ in range(nacc)))
        tot = (accs[0] + accs[1]) + (accs[2] + accs[3])
        return jnp.sum(tot, axis=0, keepdims=True)

    def bcast(v):
        return jnp.broadcast_to(v, (SUBLANES, LANES))

    def count_ge(probe):
        probe_b = bcast(probe)
        return count(lambda rows, row0: rows >= probe_b)

    n = count_ge(jnp.zeros((1, LANES), F32))
    ok = n >= TOPK
    thr_k = jnp.where(ok, 0, INT_MIN).astype(I32)
    n_ge = jnp.where(ok, n, INT_MAX).astype(I32)

    def bit_body(t, carry):
        thr_k, n_ge = carry
        cand_k = thr_k | lax.shift_left(jnp.int32(1), 30 - t)
        n = count_ge(_key_float(cand_k))
        ok = n >= TOPK
        return jnp.where(ok, cand_k, thr_k), jnp.where(ok, n, n_ge)

    thr_k, n_ge = lax.fori_loop(0, 31, bit_body, (thr_k, n_ge))
    found = thr_k > NEG_INF_KEY
    thr = jnp.where(found, _key_float(thr_k), -F32_MAX)
    thr_b = bcast(thr)
    tie = found & (n_ge > TOPK)
    ithr_ref[...] = jnp.full((SUBLANES, LANES), INT_MAX, I32)

    @pl.when(jnp.max(tie.astype(I32)) > 0)
    def _():
        n_gt = count(lambda rows, row0: rows > thr_b)
        need = TOPK - n_gt

        def idx_body(t, ithr):
            cand = ithr | lax.shift_left(jnp.int32(1), 11 - t)
            cand_b = bcast(cand)
            n = count(lambda rows, row0: (rows == thr_b) & ((row0 + row8) < cand_b))
            return jnp.where(n < need, cand, ithr)

        ithr = lax.fori_loop(0, 12, idx_body, jnp.zeros((1, LANES), I32))
        ithr_ref[...] = jnp.where(bcast(tie), bcast(ithr), ithr_ref[...])

    thr_q = jnp.broadcast_to(thr, (LANES, LANES))
    ithr_q = jnp.broadcast_to(ithr_ref[0:1, :], (LANES, LANES))
    rowq = lax.broadcasted_iota(I32, (LANES, LANES), 0)

    def bias_body(c, carry):
        for j in range(blk // LANES):
            row0 = pl.multiple_of(c * blk + j * LANES, LANES)
            sc = score_ref[pl.ds(row0, LANES), :]
            sel = (sc > thr_q) | ((sc == thr_q) & ((row0 + rowq) <= ithr_q))
            bias_ref[:, pl.ds(row0, LANES)] = jnp.where(sel, 0.0, NEG_BIG).astype(F32).T
        return carry

    lax.fori_loop(0, nblk, bias_body, 0)


def _dsa_prompt_kernel(q_ref, qi_ref, kiw_ref, kb_ref, vb_ref, kib_ref, o_ref,
                       score_ref, bias_ref, ithr_ref, qir_ref, s_ref, mlane_ref, llane_ref, acc_ref):
    i = pl.program_id(1)
    nchunk = (i * TQ + TQ + TK - 1) // TK
    qpos = i * TQ + lax.broadcasted_iota(I32, (1, LANES), 1)

    def for_chunks(body):
        def pair(c, carry):
            body(pl.multiple_of(c * (2 * TK), 2 * TK), 2 * TK)
            return carry
        lax.fori_loop(0, nchunk // 2, pair, 0)

        @pl.when(nchunk % 2 == 1)
        def _():
            body(pl.multiple_of((nchunk - 1) * TK, TK), TK)

    for h in range(IDX_HEADS):
        qir_ref[h * TQ:(h + 1) * TQ, :] = qi_ref[0, :, h * IDX_DIM:(h + 1) * IDX_DIM]
    kiw_t = kiw_ref[0].T
    w_rows = [kiw_t[IDX_DIM + h:IDX_DIM + h + 1, :] * IDX_HEAD_SCALE for h in range(IDX_HEADS)]

    def idx_body(c0, width):
        logit = _dot_nt(kib_ref[0, pl.ds(c0, width), :], qir_ref[...])
        score = jnp.zeros((width, LANES), F32)
        for h in range(IDX_HEADS):
            score = score + jnp.maximum(logit[:, h * TQ:(h + 1) * TQ], 0.0) * w_rows[h]
        krow = c0 + lax.broadcasted_iota(I32, (width, LANES), 0)
        score_ref[pl.ds(c0, width), :] = jnp.where(krow <= qpos, score, -jnp.inf)

    for_chunks(idx_body)
    _select_bias_t(score_ref, bias_ref, ithr_ref, nchunk, TK)

    def q_group(g):
        return jnp.concatenate(
            [q_ref[0, :, (g * GROUP + r) * HEAD_DIM:(g * GROUP + r + 1) * HEAD_DIM] for r in range(GROUP)], axis=0)

    mlane_ref[...] = jnp.full(mlane_ref.shape, NEG_BIG, F32)

    def logits_body(c0, width):
        bias4 = jnp.concatenate([bias_ref[:, pl.ds(c0, width)]] * GROUP, axis=0)
        for g in range(N_KV_HEADS):
            s = _dot_nt(q_group(g), kb_ref[0, pl.ds(c0, width), g * HEAD_DIM:(g + 1) * HEAD_DIM]) + bias4
            s_ref[g, :, pl.ds(c0, width)] = s
            m = mlane_ref[g]
            for j in range(width // LANES):
                m = jnp.maximum(m, s[:, j * LANES:(j + 1) * LANES])
            mlane_ref[g] = m

    for_chunks(logits_body)

    for g in range(N_KV_HEADS):
        mlane_ref[g] = jnp.broadcast_to(jnp.max(mlane_ref[g], axis=1, keepdims=True), mlane_ref.shape[1:])
    llane_ref[...] = jnp.zeros(llane_ref.shape, F32)
    acc_ref[...] = jnp.zeros(acc_ref.shape, F32)

    def pv_body(c0, width):
        for g in range(N_KV_HEADS):
            m = mlane_ref[g]
            l = llane_ref[g]
            ps = []
            for j in range(width // LANES):
                pj = jnp.exp2(s_ref[g, :, pl.ds(pl.multiple_of(c0 + j * LANES, LANES), LANES)] - m)
                l = l + pj
                ps.append(pj.astype(BF16))
            llane_ref[g] = l
            p = jnp.concatenate(ps, axis=1)
            acc_ref[g] += _dot(p, vb_ref[0, pl.ds(c0, width), g * HEAD_DIM:(g + 1) * HEAD_DIM])

    for_chunks(pv_body)

    for g in range(N_KV_HEADS):
        og = acc_ref[g] * (1.0 / jnp.sum(llane_ref[g], axis=1, keepdims=True))
        for r in range(GROUP):
            col = (g * GROUP + r) * HEAD_DIM
            o_ref[0, :, col:col + HEAD_DIM] = og[r * TQ:(r + 1) * TQ].astype(BF16)


def _dsa_prompt_call(q, qi, kiw, kb, vb, kib):
    nb, s, _ = q.shape
    tile = lambda w: pl.BlockSpec((1, TQ, w), lambda b, i: (b, i, 0))
    full = lambda w: pl.BlockSpec((1, s, w), lambda b, i: (b, 0, 0))
    rows = GROUP * TQ
    return pl.pallas_call(
        _dsa_prompt_kernel,
        grid=(nb, s // TQ),
        in_specs=[tile(Q_W), tile(QI_W), tile(KIW_W), full(KV_W), full(KV_W), full(IDX_DIM)],
        out_specs=tile(Q_W),
        out_shape=jax.ShapeDtypeStruct((nb, s, Q_W), BF16),
        scratch_shapes=[
            pltpu.VMEM((s, TQ), F32),
            pltpu.VMEM((TQ, s), F32),
            pltpu.VMEM((SUBLANES, LANES), I32),
            pltpu.VMEM((IDX_HEADS * TQ, IDX_DIM), BF16),
            pltpu.VMEM((N_KV_HEADS, rows, s), F32),
            pltpu.VMEM((N_KV_HEADS, rows, LANES), F32),
            pltpu.VMEM((N_KV_HEADS, rows, LANES), F32),
            pltpu.VMEM((N_KV_HEADS, rows, HEAD_DIM), F32),
        ],
        compiler_params=_params(("arbitrary", "arbitrary")),
        name="dsa_prompt",
    )(q, qi, kiw, kb, vb, kib)


def _page_specs(page_shape, nseq):
    zeros = (0,) * len(page_shape)
    return [pl.BlockSpec((1,) + page_shape,
                         functools.partial(lambda i, pt, u, p: (pt[i * nseq + u, p],) + zeros, u=u, p=p))
            for u in range(nseq) for p in range(N_PAGES)]


def _kv_page(page_ref):
    heads = [page_ref[0, pl.ds(g, PAGE_SIZE, stride=N_KV_HEADS), :] for g in range(N_KV_HEADS)]
    return jnp.concatenate(heads, axis=1).astype(BF16)


def _pad_new_page(new8):
    return jnp.concatenate([new8, jnp.zeros((PAGE_SIZE - Q8, new8.shape[1]), F32)], axis=0).astype(BF16)


def _idx_sample_kernel(pt_ref, qis_ref, ws_ref, kin_ref, *rest):
    nseq = qis_ref.shape[0]
    pages = rest[:nseq * N_PAGES]
    o_ref = rest[nseq * N_PAGES]
    del pt_ref
    for u in range(nseq):
        qs = qis_ref[u]
        w = ws_ref[u]

        def page_score(kt):
            r = jnp.maximum(_dot(qs, kt.astype(BF16)), 0.0) * w
            s = r[0:Q8]
            for h in range(1, IDX_HEADS):
                s = s + r[h * Q8:(h + 1) * Q8]
            return s

        for p in range(N_PAGES):
            o_ref[u, :, p * PAGE_SIZE:(p + 1) * PAGE_SIZE] = page_score(pages[u * N_PAGES + p][0])
        o_ref[u, :, PAST_LEN:L_SAMPLE] = page_score(kin_ref[u])


def _idx_sample_call(page_table, qis, ws, kinew_t, cache_ki_t):
    n = qis.shape[0]
    rows = IDX_HEADS * Q8
    nseq = SCORE_SEQS
    grid_spec = pltpu.PrefetchScalarGridSpec(
        num_scalar_prefetch=1,
        grid=(n // nseq,),
        in_specs=[
            pl.BlockSpec((nseq, rows, IDX_DIM), lambda i, pt: (i, 0, 0)),
            pl.BlockSpec((nseq, rows, 1), lambda i, pt: (i, 0, 0)),
            pl.BlockSpec((nseq, IDX_DIM, PAGE_SIZE), lambda i, pt: (i, 0, 0)),
        ] + _page_specs((IDX_DIM, PAGE_SIZE), nseq),
        out_specs=pl.BlockSpec((nseq, Q8, L_SAMPLE), lambda i, pt: (i, 0, 0)),
    )
    return pl.pallas_call(
        _idx_sample_kernel,
        grid_spec=grid_spec,
        out_shape=jax.ShapeDtypeStruct((n, Q8, L_SAMPLE), F32),
        compiler_params=_params(("arbitrary",)),
        name="dsa_sample_scores",
    )(page_table, qis, ws, kinew_t, *([cache_ki_t] * (nseq * N_PAGES)))


def _select_sample_kernel(s_ref, bias_ref, score_ref, ithr_ref):
    krow = lax.broadcasted_iota(I32, (LANES, SEL_ROWS), 0)
    qrow = lax.broadcasted_iota(I32, (LANES, SEL_ROWS), 1) % Q8
    real = qrow < T_NEW
    for j in range(L_SAMPLE // LANES):
        kidx = j * LANES + krow
        valid = ((kidx < PAST_LEN) | ((kidx - PAST_LEN) <= qrow)) & real
        score_ref[j * LANES:(j + 1) * LANES, :] = jnp.where(valid, s_ref[:, j * LANES:(j + 1) * LANES].T, -jnp.inf)
    _select_bias_t(score_ref, bias_ref, ithr_ref, L_SAMPLE // LANES, LANES)


def _select_sample_call(scores):
    rows = scores.shape[0]
    return pl.pallas_call(
        _select_sample_kernel,
        grid=(rows // SEL_ROWS,),
        in_specs=[pl.BlockSpec((SEL_ROWS, L_SAMPLE), lambda i: (i, 0))],
        out_specs=pl.BlockSpec((SEL_ROWS, L_SAMPLE), lambda i: (i, 0)),
        out_shape=jax.ShapeDtypeStruct((rows, L_SAMPLE), F32),
        scratch_shapes=[pltpu.VMEM((L_SAMPLE, SEL_ROWS), F32), pltpu.VMEM((SUBLANES, LANES), I32)],
        compiler_params=_params(("arbitrary",)),
        name="dsa_sample_select",
    )(scores)


def _attn_sample_kernel(pt_ref, qb_ref, bias_ref, kn_ref, vn_ref, *rest):
    nseq = qb_ref.shape[0]
    kpages = rest[:nseq * N_PAGES]
    vpages = rest[nseq * N_PAGES:2 * nseq * N_PAGES]
    o_ref = rest[2 * nseq * N_PAGES]
    s_ref = rest[2 * nseq * N_PAGES + 1]
    del pt_ref
    for u in range(nseq):
        qb = qb_ref[u]
        reps = qb.shape[0] // Q8

        def page_logits(p, kp):
            bias8 = bias_ref[u, :, p * PAGE_SIZE:(p + 1) * PAGE_SIZE]
            s_ref[u, :, p * PAGE_SIZE:(p + 1) * PAGE_SIZE] = (
                _dot_nt(qb, kp) + jnp.concatenate([bias8] * reps, axis=0))

        for p in range(N_PAGES):
            page_logits(p, _kv_page(kpages[u * N_PAGES + p]))
        page_logits(N_PAGES, _pad_new_page(kn_ref[u]))

        s = s_ref[u]
        m = jnp.max(s, axis=1, keepdims=True)
        e = jnp.exp2(s - m)
        l = jnp.sum(e, axis=1, keepdims=True)
        eb = e.astype(BF16)
        acc = _dot(eb[:, PAST_LEN:L_SAMPLE], _pad_new_page(vn_ref[u]))
        for p in range(N_PAGES):
            acc = acc + _dot(eb[:, p * PAGE_SIZE:(p + 1) * PAGE_SIZE], _kv_page(vpages[u * N_PAGES + p]))
        o_ref[u] = acc * (1.0 / l)


def _attn_sample_call(page_table, qblk, bias, knew8, vnew8, cache_k, cache_v):
    n, rows, _ = qblk.shape
    nseq = ATTN_SEQS
    page = (PAGE_SIZE * N_KV_HEADS, HEAD_DIM)
    grid_spec = pltpu.PrefetchScalarGridSpec(
        num_scalar_prefetch=1,
        grid=(n // nseq,),
        in_specs=[
            pl.BlockSpec((nseq, rows, KV_W), lambda i, pt: (i, 0, 0)),
            pl.BlockSpec((nseq, Q8, L_SAMPLE), lambda i, pt: (i, 0, 0)),
            pl.BlockSpec((nseq, Q8, KV_W), lambda i, pt: (i, 0, 0)),
            pl.BlockSpec((nseq, Q8, KV_W), lambda i, pt: (i, 0, 0)),
        ] + _page_specs(page, nseq) + _page_specs(page, nseq),
        out_specs=pl.BlockSpec((nseq, rows, KV_W), lambda i, pt: (i, 0, 0)),
        scratch_shapes=[pltpu.VMEM((nseq, rows, L_SAMPLE), F32)],
    )
    return pl.pallas_call(
        _attn_sample_kernel,
        grid_spec=grid_spec,
        out_shape=jax.ShapeDtypeStruct((n, rows, KV_W), F32),
        compiler_params=_params(("arbitrary",)),
        name="dsa_sample_attn",
    )(page_table, qblk, bias, knew8, vnew8, *([cache_k] * (nseq * N_PAGES)), *([cache_v] * (nseq * N_PAGES)))


def _mlp_resid(x1, g_ffn, shift, scale, gate, wup_ref, wdn_ref, acc_ref):
    h = _rms_mod(x1, g_ffn, shift, scale).astype(BF16)
    for c in range(D_FF // FF_CHUNK):
        u = _dot(h, wup_ref[:, c * FF_CHUNK:(c + 1) * FF_CHUNK])
        u = jnp.square(jnp.maximum(u, 0.0)).astype(BF16)
        d = _dot(u, wdn_ref[c * FF_CHUNK:(c + 1) * FF_CHUNK, :])
        if c == 0:
            acc_ref[...] = d
        else:
            acc_ref[...] += d
    return x1 + gate * acc_ref[...]


def _post0_kernel(x_ref, o_ref, g1_ref, sh2_ref, sc2_ref, g2_ref, gffn_ref, wo_ref, wup_ref, wdn_ref,
                  out_ref, acc_ref):
    x1 = x_ref[0] + g1_ref[0] * _dot(o_ref[0], wo_ref[...])
    out_ref[0] = _mlp_resid(x1, gffn_ref[...], sh2_ref[0], sc2_ref[0], g2_ref[0], wup_ref, wdn_ref, acc_ref)


def _const_spec(shape):
    nd = len(shape)
    return pl.BlockSpec(shape, lambda b, i: (0,) * nd)


def _post0_call(x, o, g1, sh2, sc2, g2, g_ffn, w_o, w_up, w_dn, tm):
    nb, t, _ = x.shape
    tok = lambda w: pl.BlockSpec((1, tm, w), lambda b, i: (b, i, 0))
    return pl.pallas_call(
        _post0_kernel,
        grid=(nb, t // tm),
        in_specs=[tok(D_MODEL), tok(Q_W), g1.spec(tm), sh2.spec(tm), sc2.spec(tm),
                  g2.spec(tm), _const_spec((1, D_MODEL)), _const_spec((Q_W, D_MODEL)),
                  _const_spec((D_MODEL, D_FF)), _const_spec((D_FF, D_MODEL))],
        out_specs=tok(D_MODEL),
        out_shape=jax.ShapeDtypeStruct((nb, t, D_MODEL), F32),
        scratch_shapes=[pltpu.VMEM((tm, D_MODEL), F32)],
        compiler_params=_params(("arbitrary", "arbitrary")),
        name="layer0_out",
    )(x, o, g1.arr, sh2.arr, sc2.arr, g2.arr, g_ffn.reshape(1, D_MODEL), w_o, w_up, w_dn)


def _pool_project(pooled, wp_ref, ps_ref):
    parts = []
    for g in range(len(POOL_WINDOWS)):
        lo = g * POOL_GROUP_DIM
        parts.append(_dot(pooled[:, lo:lo + POOL_GROUP_DIM].astype(BF16), wp_ref[g]))
    return jnp.concatenate(parts, axis=1) * ps_ref[...]


def _layer1_prompt_kernel(x_ref, xh_ref, sh1_ref, sc1_ref, g1_ref, sh2_ref, sc2_ref, g2_ref,
                          gmix_ref, gffn_ref, gfin_ref, wp_ref, ps_ref, wup_ref, wdn_ref,
                          y_ref, hlast_ref, acc_ref):
    i = pl.program_id(1)
    tm = x_ref.shape[1]
    x = x_ref[0]
    sh1, sc1 = sh1_ref[0], sc1_ref[0]
    h = _rms_mod(x, gmix_ref[...], sh1, sc1)
    h_halo = _rms_mod(xh_ref[0], gmix_ref[...], sh1, sc1)
    h_halo = jnp.where(i > 0, h_halo, 0.0)
    ext = jnp.concatenate([h_halo, h], axis=0)
    pos = i * tm + lax.broadcasted_iota(I32, (tm, 1), 0)
    parts = []
    for g, w in enumerate(POOL_WINDOWS):
        lo = g * POOL_GROUP_DIM
        s = ext[:, lo:lo + POOL_GROUP_DIM]
        step = 1
        while step < w:
            s = s + pltpu.roll(s, step, axis=0)
            step *= 2
        cnt = jnp.minimum(w, pos + 1).astype(F32)
        parts.append(s[HALO:, :] / cnt - h[:, lo:lo + POOL_GROUP_DIM])
    pooled = jnp.concatenate(parts, axis=1)
    x1 = x + g1_ref[0] * _pool_project(pooled, wp_ref, ps_ref)
    x2 = _mlp_resid(x1, gffn_ref[...], sh2_ref[0], sc2_ref[0], g2_ref[0], wup_ref, wdn_ref, acc_ref)
    y_ref[0] = _rmsnorm(x2, gfin_ref[...])

    @pl.when(i == pl.num_programs(1) - 1)
    def _():
        hlast_ref[0] = h[tm - HALO:, :]


def _layer1_prompt_call(x, mods, g_mix, g_ffn, g_fin, pool_w, pool_scale, w_up, w_dn, tm):
    nb, t, _ = x.shape
    tok = pl.BlockSpec((1, tm, D_MODEL), lambda b, i: (b, i, 0))
    halo = pl.BlockSpec((1, HALO, D_MODEL), lambda b, i: (b, jnp.maximum(i * (tm // HALO) - 1, 0), 0))
    vec = _const_spec((1, D_MODEL))
    return pl.pallas_call(
        _layer1_prompt_kernel,
        grid=(nb, t // tm),
        in_specs=[tok, halo] + [m.spec(tm) for m in mods] + [vec, vec, vec,
                  _const_spec(pool_w.shape), vec, _const_spec((D_MODEL, D_FF)), _const_spec((D_FF, D_MODEL))],
        out_specs=[tok, pl.BlockSpec((1, HALO, D_MODEL), lambda b, i: (b, 0, 0))],
        out_shape=[jax.ShapeDtypeStruct((nb, t, D_MODEL), F32), jax.ShapeDtypeStruct((nb, HALO, D_MODEL), F32)],
        scratch_shapes=[pltpu.VMEM((tm, D_MODEL), F32)],
        compiler_params=_params(("arbitrary", "arbitrary")),
        name="layer1_prompt",
    )(x, x, *[m.arr for m in mods], g_mix.reshape(1, D_MODEL), g_ffn.reshape(1, D_MODEL),
      g_fin.reshape(1, D_MODEL), pool_w, pool_scale.reshape(1, D_MODEL), w_up, w_dn)


def _layer1_sample_kernel(x_ref, prev_ref, sh1_ref, sc1_ref, g1_ref, sh2_ref, sc2_ref, g2_ref,
                          gmix_ref, gffn_ref, gfin_ref, wp_ref, ps_ref, wup_ref, wdn_ref,
                          y_ref, h_ref, hs_ref, pooled_ref, acc_ref):
    tm = x_ref.shape[1]
    t_new = tm // prev_ref.shape[1]
    nseq = prev_ref.shape[1]
    x = x_ref[0]
    h = _rms_mod(x, gmix_ref[...], sh1_ref[0], sc1_ref[0])
    h_ref[0] = h
    ncol = D_MODEL // LANES
    for c in range(ncol):
        hs_ref[c] = h[:, c * LANES:(c + 1) * LANES]
    ext = [prev_ref[j] for j in range(POOL_STATE)]
    ext += [jnp.concatenate([hs_ref[c, pl.ds(t, nseq, stride=t_new), :] for c in range(ncol)], axis=1)
            for t in range(t_new)]
    for t in range(t_new):
        parts = []
        for g, w in enumerate(POOL_WINDOWS):
            lo = g * POOL_GROUP_DIM
            s = ext[POOL_STATE + t][:, lo:lo + POOL_GROUP_DIM]
            for j in range(1, w):
                s = s + ext[POOL_STATE + t - j][:, lo:lo + POOL_GROUP_DIM]
            parts.append(s / float(w) - ext[POOL_STATE + t][:, lo:lo + POOL_GROUP_DIM])
        pooled_t = jnp.concatenate(parts, axis=1)
        for c in range(ncol):
            pooled_ref[c, pl.ds(t, nseq, stride=t_new), :] = pooled_t[:, c * LANES:(c + 1) * LANES]
    pooled = jnp.concatenate([pooled_ref[c] for c in range(ncol)], axis=1)
    x1 = x + g1_ref[0] * _pool_project(pooled, wp_ref, ps_ref)
    x2 = _mlp_resid(x1, gffn_ref[...], sh2_ref[0], sc2_ref[0], g2_ref[0], wup_ref, wdn_ref, acc_ref)
    y_ref[0] = _rmsnorm(x2, gfin_ref[...])


def _layer1_sample_call(x, prev_t, mods, g_mix, g_ffn, g_fin, pool_w, pool_scale, w_up, w_dn, tm, t_new):
    nb, t, _ = x.shape
    tok = pl.BlockSpec((1, tm, D_MODEL), lambda b, i: (b, i, 0))
    prev = pl.BlockSpec((POOL_STATE, tm // t_new, D_MODEL), lambda b, i: (0, i, 0))
    vec = _const_spec((1, D_MODEL))
    return pl.pallas_call(
        _layer1_sample_kernel,
        grid=(nb, t // tm),
        in_specs=[tok, prev] + [m.spec(tm) for m in mods] + [vec, vec, vec,
                  _const_spec(pool_w.shape), vec, _const_spec((D_MODEL, D_FF)), _const_spec((D_FF, D_MODEL))],
        out_specs=[tok, tok],
        out_shape=[jax.ShapeDtypeStruct((nb, t, D_MODEL), F32), jax.ShapeDtypeStruct((nb, t, D_MODEL), F32)],
        scratch_shapes=[pltpu.VMEM((D_MODEL // LANES, tm, LANES), F32),
                        pltpu.VMEM((D_MODEL // LANES, tm, LANES), F32),
                        pltpu.VMEM((tm, D_MODEL), F32)],
        compiler_params=_params(("arbitrary", "arbitrary")),
        name="layer1_sample",
    )(x, prev_t, *[m.arr for m in mods], g_mix.reshape(1, D_MODEL), g_ffn.reshape(1, D_MODEL),
      g_fin.reshape(1, D_MODEL), pool_w, pool_scale.reshape(1, D_MODEL), w_up, w_dn)


def kernel(x_prompt, x_sample, cache_k, cache_v, cache_kidx, state_pool, page_table, c_prompt, c_sample,
           norm_mix_g, norm_ffn_g, w_mod, b_mod, dsa_w_in, dsa_w_o, pool_w, pool_scale, w_up, w_down,
           final_norm_g):
    nbp, seq, _ = x_prompt.shape
    nbs, t_new, _ = x_sample.shape
    n_tok_s = nbs * t_new

    w_in_b = jnp.pad(dsa_w_in[0], ((0, 0), (0, D_PROJ_PAD - D_PROJ))).astype(BF16)
    w_o_b = dsa_w_o[0].astype(BF16)
    w_up_b = w_up.astype(BF16)
    w_dn_b = w_down.astype(BF16)
    pool_w_b = pool_w[0].astype(BF16)

    n_c = nbp + nbs
    c_rows = -(-n_c // SUBLANES) * SUBLANES
    c_all = jnp.pad(jnp.concatenate([c_prompt, c_sample], axis=0), ((0, c_rows - n_c), (0, 0)))
    mod, mod_tok = _mod_call(c_all, w_mod, b_mod, nbp, nbs, t_new)

    def mods_prompt(layer):
        m = mod[layer, :nbp].reshape(nbp, 1, 6, D_MODEL)
        return [_mod_per_sequence(m[:, :, j]) for j in range(6)]

    def mods_sample(layer):
        return [_mod_per_token(mod_tok, layer, j) for j in range(6)]

    sh1, sc1, g1, sh2, sc2, g2 = mods_prompt(0)
    k_p, v_p, ki_p, q, qi, kiw, kb, vb, kib = _proj_call(x_prompt, sh1, sc1, norm_mix_g[0], w_in_b, 512)
    o = _dsa_prompt_call(q, qi, kiw, kb, vb, kib)
    x1 = _post0_call(x_prompt, o, g1, sh2, sc2, g2, norm_ffn_g[0], w_o_b, w_up_b[0], w_dn_b[0], 512)
    y_prompt, hlast = _layer1_prompt_call(x1, mods_prompt(1), norm_mix_g[1], norm_ffn_g[1], final_norm_g,
                                          pool_w_b, pool_scale[0], w_up_b[1], w_dn_b[1], 512)
    pool_p = hlast[:, HALO - POOL_STATE:]

    sh1, sc1, g1, sh2, sc2, g2 = mods_sample(0)
    xs = x_sample.reshape(1, n_tok_s, D_MODEL)
    tms = 256
    k_s, v_s, ki_s, q, qi, kiw, _, _, _ = _proj_call(xs, sh1, sc1, norm_mix_g[0], w_in_b, tms)

    pad_q = lambda a: jnp.pad(a, [(0, 0)] * (a.ndim - 2) + [(0, Q8 - t_new), (0, 0)])
    qis = pad_q(qi.reshape(nbs, t_new, IDX_HEADS, IDX_DIM).transpose(0, 2, 1, 3)).reshape(nbs, IDX_HEADS * Q8, IDX_DIM)
    wi = kiw.reshape(nbs, t_new, KIW_W)[:, :, IDX_DIM:IDX_DIM + IDX_HEADS] * IDX_HEAD_SCALE
    ws = pad_q(wi.transpose(0, 2, 1)[..., None]).reshape(nbs, IDX_HEADS * Q8, 1)
    knew8 = pad_q(k_s.reshape(nbs, t_new, KV_W))
    vnew8 = pad_q(v_s.reshape(nbs, t_new, KV_W))
    kinew_t = jnp.pad(ki_s.reshape(nbs, t_new, IDX_DIM).transpose(0, 2, 1), ((0, 0), (0, 0), (0, PAGE_SIZE - t_new)))
    scores = _idx_sample_call(page_table, qis, ws, kinew_t, cache_kidx[0].transpose(0, 2, 1))
    bias = _select_sample_call(scores.reshape(nbs * Q8, L_SAMPLE)).reshape(nbs, Q8, L_SAMPLE)
    qg = pad_q(q.reshape(nbs, t_new, N_KV_HEADS, GROUP, HEAD_DIM).transpose(0, 2, 3, 1, 4))
    eye = jnp.eye(N_KV_HEADS, dtype=BF16)
    qblk = (qg[:, :, :, :, None, :] * eye[None, :, None, None, :, None]).reshape(nbs, N_KV_HEADS * GROUP * Q8, KV_W)
    n_pool = cache_k.shape[1]
    page_rows = lambda c: c[0].reshape(n_pool, PAGE_SIZE * N_KV_HEADS, HEAD_DIM)
    o_blk = _attn_sample_call(page_table, qblk, bias, knew8, vnew8, page_rows(cache_k), page_rows(cache_v))
    o_blk = o_blk.reshape(nbs, N_KV_HEADS, GROUP, Q8, N_KV_HEADS, HEAD_DIM)
    o_s = jnp.stack([o_blk[:, g, :, :t_new, g] for g in range(N_KV_HEADS)], axis=1)
    o_s = o_s.transpose(0, 3, 1, 2, 4).reshape(1, n_tok_s, Q_W).astype(BF16)
    x1s = _post0_call(xs, o_s, g1, sh2, sc2, g2, norm_ffn_g[0], w_o_b, w_up_b[0], w_dn_b[0], tms)
    prev_t = state_pool[0].transpose(1, 0, 2)
    y_s, h1s = _layer1_sample_call(x1s, prev_t, mods_sample(1), norm_mix_g[1], norm_ffn_g[1], final_norm_g,
                                   pool_w_b, pool_scale[0], w_up_b[1], w_dn_b[1], tms, t_new)
    y_sample = y_s.reshape(nbs, t_new, D_MODEL)
    pool_s = jnp.concatenate([state_pool[0][:, t_new:], h1s.reshape(nbs, t_new, D_MODEL)], axis=1)

    return (
        y_prompt, y_sample,
        k_p[None], v_p[None], ki_p[None], pool_p[None],
        k_s.reshape(1, nbs, t_new, N_KV_HEADS, HEAD_DIM), v_s.reshape(1, nbs, t_new, N_KV_HEADS, HEAD_DIM),
        ki_s.reshape(1, nbs, t_new, IDX_DIM), pool_s[None],
    )
```

```python
import functools
from typing import Callable, NamedTuple

import jax
import jax.numpy as jnp
from jax import lax
from jax.experimental import pallas as pl
from jax.experimental.pallas import tpu as pltpu

F32 = jnp.float32
BF16 = jnp.bfloat16
I32 = jnp.int32

D_MODEL = 1024
DEPTH = 2
PAST_LEN = 2048
PAGE_SIZE = 128
N_PAGES = PAST_LEN // PAGE_SIZE
N_HEADS = 8
HEAD_DIM = 128
N_KV_HEADS = 2
GROUP = N_HEADS // N_KV_HEADS
IDX_HEADS = 8
IDX_DIM = 64
TOPK = 256
POOL_WINDOWS = (2, 4, 8, 16)
POOL_GROUP_DIM = D_MODEL // len(POOL_WINDOWS)
POOL_STATE = max(POOL_WINDOWS) - 1
D_FF = 4 * D_MODEL
RMS_EPS = 1e-6
Q_W = N_HEADS * HEAD_DIM
KV_W = N_KV_HEADS * HEAD_DIM
QI_W = IDX_HEADS * IDX_DIM
D_PROJ = Q_W + 2 * KV_W + QI_W + IDX_DIM + IDX_HEADS

LANES = 128
SUBLANES = 8
VMEM_LIMIT_BYTES = 56 * 1024 * 1024

KIW_W = LANES
D_PROJ_PAD = Q_W + 2 * KV_W + QI_W + KIW_W
OFF_K = Q_W
OFF_V = OFF_K + KV_W
OFF_QI = OFF_V + KV_W
OFF_KIW = OFF_QI + QI_W

LOG2_E = 1.4426950408889634
ATT_SCALE = HEAD_DIM ** -0.5 * LOG2_E
IDX_SCALE = IDX_DIM ** -0.5
IDX_HEAD_SCALE = IDX_HEADS ** -0.5

INT_MIN = -(2 ** 31)
INT_MAX = 2 ** 31 - 1
F32_MAX = 3.4028234663852886e38
NEG_INF_KEY = INT_MIN + (1 << 23)
NEG_BIG = -1e30

TQ = 128
TK = 512
FF_CHUNK = 512
HALO = 16
SEL_ROWS = 128
L_SAMPLE = PAST_LEN + PAGE_SIZE
T_NEW = 4
SCORE_SEQS = 4
ATTN_SEQS = 2
Q8 = SUBLANES


def _params(sem):
    return pltpu.CompilerParams(dimension_semantics=sem, vmem_limit_bytes=VMEM_LIMIT_BYTES)


def _dot(a, b):
    return jnp.dot(a, b, preferred_element_type=F32)


def _dot_nt(a, b):
    return lax.dot_general(a, b, (((1,), (1,)), ((), ())), preferred_element_type=F32)


def _rms_mod(x, g, shift, scale):
    ms = jnp.mean(x * x, axis=-1, keepdims=True)
    y = x * lax.rsqrt(ms + RMS_EPS) * g
    return y * (1.0 + scale) + shift


def _rmsnorm(x, g):
    ms = jnp.mean(x * x, axis=-1, keepdims=True)
    return x * lax.rsqrt(ms + RMS_EPS) * g


def _split3(x):
    hi = x.astype(BF16)
    r1 = x - hi.astype(F32)
    mid = r1.astype(BF16)
    lo = (r1 - mid.astype(F32)).astype(BF16)
    return hi, mid, lo


def _mod_kernel(c_ref, w_ref, b_ref, o_ref, tok_ref, *, n_prompt, t_new):
    c = c_ref[...]
    a = c * (1.0 / (1.0 + jnp.exp(-c)))
    w = w_ref[0]
    a_hi = a.astype(BF16)
    a_lo = (a - a_hi.astype(F32)).astype(BF16)
    w_hi = w.astype(BF16)
    w_lo = (w - w_hi.astype(F32)).astype(BF16)
    mod = _dot(a_hi, w_hi) + _dot(a_lo, w_hi) + _dot(a_hi, w_lo) + b_ref[0]
    o_ref[0] = mod
    n_tok, rows = tok_ref.shape[1], mod.shape[0]
    r = lax.broadcasted_iota(I32, (n_tok, rows), 0)
    s = (lax.broadcasted_iota(I32, (n_tok, rows), 1) - n_prompt) * t_new
    pick = jnp.where((r >= s) & (r < s + t_new), 1.0, 0.0).astype(BF16)
    hi, mid, lo = _split3(mod)
    tok_ref[0] = (_dot(pick, hi) + _dot(pick, mid)) + _dot(pick, lo)


def _mod_call(c_all, w_mod, b_mod, n_prompt, n_sample, t_new):
    rows = c_all.shape[0]
    tn = 1536
    return pl.pallas_call(
        functools.partial(_mod_kernel, n_prompt=n_prompt, t_new=t_new),
        grid=(DEPTH, 6 * D_MODEL // tn),
        in_specs=[
            pl.BlockSpec((rows, D_MODEL), lambda i, j: (0, 0)),
            pl.BlockSpec((1, D_MODEL, tn), lambda i, j: (i, 0, j)),
            pl.BlockSpec((1, 1, tn), lambda i, j: (i, 0, j)),
        ],
        out_specs=[pl.BlockSpec((1, rows, tn), lambda i, j: (i, 0, j)),
                   pl.BlockSpec((1, n_sample * t_new, tn), lambda i, j: (i, 0, j))],
        out_shape=[jax.ShapeDtypeStruct((DEPTH, rows, 6 * D_MODEL), F32),
                   jax.ShapeDtypeStruct((DEPTH, n_sample * t_new, 6 * D_MODEL), F32)],
        compiler_params=_params(("arbitrary", "arbitrary")),
        name="mod_vectors",
    )(c_all, w_mod, b_mod.reshape(DEPTH, 1, 6 * D_MODEL))


def _proj_kernel(x_ref, sh_ref, sc_ref, g_ref, w_ref,
                 k_ref, v_ref, ki_ref, q_ref, qi_ref, kiw_ref, kb_ref, vb_ref, kib_ref):
    h = _rms_mod(x_ref[0], g_ref[...], sh_ref[0], sc_ref[0]).astype(BF16)
    q_ref[0] = (_dot(h, w_ref[:, 0:Q_W]) * ATT_SCALE).astype(BF16)
    k = _dot(h, w_ref[:, OFF_K:OFF_K + KV_W])
    v = _dot(h, w_ref[:, OFF_V:OFF_V + KV_W])
    for g in range(N_KV_HEADS):
        k_ref[0, :, g, :] = k[:, g * HEAD_DIM:(g + 1) * HEAD_DIM]
        v_ref[0, :, g, :] = v[:, g * HEAD_DIM:(g + 1) * HEAD_DIM]
    kb_ref[0] = k.astype(BF16)
    vb_ref[0] = v.astype(BF16)
    qi_ref[0] = (_dot(h, w_ref[:, OFF_QI:OFF_QI + QI_W]) * IDX_SCALE).astype(BF16)
    kiw = _dot(h, w_ref[:, OFF_KIW:OFF_KIW + KIW_W])
    kiw_ref[0] = kiw
    ki = kiw[:, 0:IDX_DIM]
    ki_ref[0] = ki
    kib_ref[0] = ki.astype(BF16)


class _Mod(NamedTuple):
    arr: jax.Array
    spec: Callable[[int], pl.BlockSpec]


def _mod_per_sequence(arr):
    return _Mod(arr, lambda tm: pl.BlockSpec((1, 1, D_MODEL), lambda b, i: (b, 0, 0)))


def _mod_per_token(mod_tok, layer, j):
    return _Mod(mod_tok, lambda tm: pl.BlockSpec((1, tm, D_MODEL), lambda b, i: (layer, i, j)))


def _proj_call(x, shift, scale, g, w_in_b, tm):
    nb, t, _ = x.shape
    tok = lambda w: pl.BlockSpec((1, tm, w), lambda b, i: (b, i, 0))
    shp = lambda w, dt: jax.ShapeDtypeStruct((nb, t, w), dt)
    heads = pl.BlockSpec((1, tm, N_KV_HEADS, HEAD_DIM), lambda b, i: (b, i, 0, 0))
    heads_shape = jax.ShapeDtypeStruct((nb, t, N_KV_HEADS, HEAD_DIM), F32)
    return pl.pallas_call(
        _proj_kernel,
        grid=(nb, t // tm),
        in_specs=[
            tok(D_MODEL), shift.spec(tm), scale.spec(tm),
            pl.BlockSpec((1, D_MODEL), lambda b, i: (0, 0)),
            pl.BlockSpec((D_MODEL, D_PROJ_PAD), lambda b, i: (0, 0)),
        ],
        out_specs=[heads, heads, tok(IDX_DIM), tok(Q_W), tok(QI_W), tok(KIW_W),
                   tok(KV_W), tok(KV_W), tok(IDX_DIM)],
        out_shape=[heads_shape, heads_shape, shp(IDX_DIM, F32), shp(Q_W, BF16), shp(QI_W, BF16),
                   shp(KIW_W, F32), shp(KV_W, BF16), shp(KV_W, BF16), shp(IDX_DIM, BF16)],
        compiler_params=_params(("arbitrary", "arbitrary")),
        name="dsa_proj",
    )(x, shift.arr, scale.arr, g.reshape(1, D_MODEL), w_in_b)


def _key_float(k):
    return lax.bitcast_convert_type(jnp.where(k >= 0, k, (k - 1) ^ INT_MAX), F32)


def _select_threshold(score_ref, ithr_ref, nblk, blk):
    nsub = blk // SUBLANES
    nacc = 4
    row8 = lax.broadcasted_iota(I32, (SUBLANES, LANES), 0)

    def count(pred):
        def body(c, accs):
            accs = list(accs)
            c0 = pl.multiple_of(c * blk, blk)
            rows = score_ref[pl.ds(c0, blk), :]
            for j in range(nsub):
                hit = pred(rows[j * SUBLANES:(j + 1) * SUBLANES], c0 + j * SUBLANES)
                accs[j % nacc] = accs[j % nacc] + jnp.where(hit, 1, 0)
            return tuple(accs)
        accs = lax.fori_loop(0, nblk, body, tuple(jnp.zeros((SUBLANES, LANES), I32) for _ in range(nacc)))
        tot = (accs[0] + accs[1]) + (accs[2] + accs[3])
        return jnp.sum(tot, axis=0, keepdims=True)

    def bcast(v):
        return jnp.broadcast_to(v, (SUBLANES, LANES))

    def count_ge(probe):
        probe_b = bcast(probe)
        return count(lambda rows, row0: rows >= probe_b)

    n = count_ge(jnp.zeros((1, LANES), F32))
    ok = n >= TOPK
    thr_k = jnp.where(ok, 0, INT_MIN).astype(I32)
    n_ge = jnp.where(ok, n, INT_MAX).astype(I32)

    def bit_body(t, carry):
        thr_k, n_ge = carry
        cand_k = thr_k | lax.shift_left(jnp.int32(1), 30 - t)
        n = count_ge(_key_float(cand_k))
        ok = n >= TOPK
        return jnp.where(ok, cand_k, thr_k), jnp.where(ok, n, n_ge)

    thr_k, n_ge = lax.fori_loop(0, 31, bit_body, (thr_k, n_ge))
    found = thr_k > NEG_INF_KEY
    thr = jnp.where(found, _key_float(thr_k), -F32_MAX)
    thr_b = bcast(thr)
    tie = found & (n_ge > TOPK)
    ithr_ref[...] = jnp.full((SUBLANES, LANES), INT_MAX, I32)

    @pl.when(jnp.max(tie.astype(I32)) > 0)
    def _():
        n_gt = count(lambda rows, row0: rows > thr_b)
        need = TOPK - n_gt

        def idx_body(t, ithr):
            cand = ithr | lax.shift_left(jnp.int32(1), 11 - t)
            cand_b = bcast(cand)
            n = count(lambda rows, row0: (rows == thr_b) & ((row0 + row8) < cand_b))
            return jnp.where(n < need, cand, ithr)

        ithr = lax.fori_loop(0, 12, idx_body, jnp.zeros((1, LANES), I32))
        ithr_ref[...] = jnp.where(bcast(tie), bcast(ithr), ithr_ref[...])

    return thr


def _bias_block(score_ref, row0, thr_q, ithr_q):
    sc = score_ref[pl.ds(row0, LANES), :]
    key_index = row0 + lax.broadcasted_iota(I32, (LANES, LANES), 0)
    sel = (sc > thr_q) | ((sc == thr_q) & (key_index <= ithr_q))
    return jnp.where(sel, 0.0, NEG_BIG).astype(F32).T


def _dsa_prompt_kernel(q_ref, qi_ref, kiw_ref, kb_ref, vb_ref, kib_ref, o_ref,
                       score_ref, ithr_ref, qir_ref, s_ref, mlane_ref, llane_ref, acc_ref):
    i = pl.program_id(1)
    nchunk = (i * TQ + TQ + TK - 1) // TK
    qpos = i * TQ + lax.broadcasted_iota(I32, (1, LANES), 1)

    def for_chunks(body):
        def pair(c, carry):
            body(pl.multiple_of(c * (2 * TK), 2 * TK), 2 * TK)
            return carry
        lax.fori_loop(0, nchunk // 2, pair, 0)

        @pl.when(nchunk % 2 == 1)
        def _():
            body(pl.multiple_of((nchunk - 1) * TK, TK), TK)

    for h in range(IDX_HEADS):
        qir_ref[h * TQ:(h + 1) * TQ, :] = qi_ref[0, :, h * IDX_DIM:(h + 1) * IDX_DIM]
    kiw_t = kiw_ref[0].T
    w_rows = [kiw_t[IDX_DIM + h:IDX_DIM + h + 1, :] * IDX_HEAD_SCALE for h in range(IDX_HEADS)]

    def idx_body(c0, width):
        logit = _dot_nt(kib_ref[0, pl.ds(c0, width), :], qir_ref[...])
        score = jnp.zeros((width, LANES), F32)
        for h in range(IDX_HEADS):
            score = score + jnp.maximum(logit[:, h * TQ:(h + 1) * TQ], 0.0) * w_rows[h]
        krow = c0 + lax.broadcasted_iota(I32, (width, LANES), 0)
        score_ref[pl.ds(c0, width), :] = jnp.where(krow <= qpos, score, -jnp.inf)

    for_chunks(idx_body)
    thr = _select_threshold(score_ref, ithr_ref, nchunk, TK)
    thr_q = jnp.broadcast_to(thr, (LANES, LANES))
    ithr_q = jnp.broadcast_to(ithr_ref[0:1, :], (LANES, LANES))

    def q_group(g):
        return jnp.concatenate(
            [q_ref[0, :, (g * GROUP + r) * HEAD_DIM:(g * GROUP + r + 1) * HEAD_DIM] for r in range(GROUP)], axis=0)

    mlane_ref[...] = jnp.full(mlane_ref.shape, NEG_BIG, F32)

    def logits_body(c0, width):
        bias = jnp.concatenate(
            [_bias_block(score_ref, pl.multiple_of(c0 + j * LANES, LANES), thr_q, ithr_q)
             for j in range(width // LANES)], axis=1)
        bias4 = jnp.concatenate([bias] * GROUP, axis=0)
        for g in range(N_KV_HEADS):
            s = _dot_nt(q_group(g), kb_ref[0, pl.ds(c0, width), g * HEAD_DIM:(g + 1) * HEAD_DIM]) + bias4
            s_ref[g, :, pl.ds(c0, width)] = s
            m = mlane_ref[g]
            for j in range(width // LANES):
                m = jnp.maximum(m, s[:, j * LANES:(j + 1) * LANES])
            mlane_ref[g] = m

    for_chunks(logits_body)

    for g in range(N_KV_HEADS):
        mlane_ref[g] = jnp.broadcast_to(jnp.max(mlane_ref[g], axis=1, keepdims=True), mlane_ref.shape[1:])
    llane_ref[...] = jnp.zeros(llane_ref.shape, F32)
    acc_ref[...] = jnp.zeros(acc_ref.shape, F32)

    def pv_body(c0, width):
        for g in range(N_KV_HEADS):
            m = mlane_ref[g]
            l = llane_ref[g]
            ps = []
            for j in range(width // LANES):
                pj = jnp.exp2(s_ref[g, :, pl.ds(pl.multiple_of(c0 + j * LANES, LANES), LANES)] - m)
                l = l + pj
                ps.append(pj.astype(BF16))
            llane_ref[g] = l
            p = jnp.concatenate(ps, axis=1)
            acc_ref[g] += _dot(p, vb_ref[0, pl.ds(c0, width), g * HEAD_DIM:(g + 1) * HEAD_DIM])

    for_chunks(pv_body)

    for g in range(N_KV_HEADS):
        og = acc_ref[g] * (1.0 / jnp.sum(llane_ref[g], axis=1, keepdims=True))
        for r in range(GROUP):
            col = (g * GROUP + r) * HEAD_DIM
            o_ref[0, :, col:col + HEAD_DIM] = og[r * TQ:(r + 1) * TQ].astype(BF16)


def _dsa_prompt_call(q, qi, kiw, kb, vb, kib):
    nb, s, _ = q.shape
    tile = lambda w: pl.BlockSpec((1, TQ, w), lambda b, i: (b, i, 0))
    full = lambda w: pl.BlockSpec((1, s, w), lambda b, i: (b, 0, 0))
    rows = GROUP * TQ
    return pl.pallas_call(
        _dsa_prompt_kernel,
        grid=(nb, s // TQ),
        in_specs=[tile(Q_W), tile(QI_W), tile(KIW_W), full(KV_W), full(KV_W), full(IDX_DIM)],
        out_specs=tile(Q_W),
        out_shape=jax.ShapeDtypeStruct((nb, s, Q_W), BF16),
        scratch_shapes=[
            pltpu.VMEM((s, TQ), F32),
            pltpu.VMEM((SUBLANES, LANES), I32),
            pltpu.VMEM((IDX_HEADS * TQ, IDX_DIM), BF16),
            pltpu.VMEM((N_KV_HEADS, rows, s), F32),
            pltpu.VMEM((N_KV_HEADS, rows, LANES), F32),
            pltpu.VMEM((N_KV_HEADS, rows, LANES), F32),
            pltpu.VMEM((N_KV_HEADS, rows, HEAD_DIM), F32),
        ],
        compiler_params=_params(("arbitrary", "arbitrary")),
        name="dsa_prompt",
    )(q, qi, kiw, kb, vb, kib)


def _page_specs(page_shape, nseq):
    zeros = (0,) * len(page_shape)
    return [pl.BlockSpec((1,) + page_shape,
                         functools.partial(lambda i, pt, u, p: (pt[i * nseq + u, p],) + zeros, u=u, p=p))
            for u in range(nseq) for p in range(N_PAGES)]


def _kv_page(page_ref):
    heads = [page_ref[0, pl.ds(g, PAGE_SIZE, stride=N_KV_HEADS), :] for g in range(N_KV_HEADS)]
    return jnp.concatenate(heads, axis=1).astype(BF16)


def _pad_new_page(new8):
    return jnp.concatenate([new8, jnp.zeros((PAGE_SIZE - Q8, new8.shape[1]), F32)], axis=0).astype(BF16)


def _idx_sample_kernel(pt_ref, qis_ref, ws_ref, kin_ref, *rest):
    nseq = qis_ref.shape[0]
    pages = rest[:nseq * N_PAGES]
    o_ref = rest[nseq * N_PAGES]
    del pt_ref
    for u in range(nseq):
        qs = qis_ref[u]
        w = ws_ref[u]

        def page_score(kt):
            r = jnp.maximum(_dot(qs, kt.astype(BF16)), 0.0) * w
            s = r[0:Q8]
            for h in range(1, IDX_HEADS):
                s = s + r[h * Q8:(h + 1) * Q8]
            return s

        for p in range(N_PAGES):
            o_ref[u, :, p * PAGE_SIZE:(p + 1) * PAGE_SIZE] = page_score(pages[u * N_PAGES + p][0])
        o_ref[u, :, PAST_LEN:L_SAMPLE] = page_score(kin_ref[u])


def _idx_sample_call(page_table, qis, ws, kinew_t, cache_ki_t):
    n = qis.shape[0]
    rows = IDX_HEADS * Q8
    nseq = SCORE_SEQS
    grid_spec = pltpu.PrefetchScalarGridSpec(
        num_scalar_prefetch=1,
        grid=(n // nseq,),
        in_specs=[
            pl.BlockSpec((nseq, rows, IDX_DIM), lambda i, pt: (i, 0, 0)),
            pl.BlockSpec((nseq, rows, 1), lambda i, pt: (i, 0, 0)),
            pl.BlockSpec((nseq, IDX_DIM, PAGE_SIZE), lambda i, pt: (i, 0, 0)),
        ] + _page_specs((IDX_DIM, PAGE_SIZE), nseq),
        out_specs=pl.BlockSpec((nseq, Q8, L_SAMPLE), lambda i, pt: (i, 0, 0)),
    )
    return pl.pallas_call(
        _idx_sample_kernel,
        grid_spec=grid_spec,
        out_shape=jax.ShapeDtypeStruct((n, Q8, L_SAMPLE), F32),
        compiler_params=_params(("arbitrary",)),
        name="dsa_sample_scores",
    )(page_table, qis, ws, kinew_t, *([cache_ki_t] * (nseq * N_PAGES)))


def _select_sample_kernel(s_ref, bias_ref, score_ref, ithr_ref):
    krow = lax.broadcasted_iota(I32, (LANES, SEL_ROWS), 0)
    qrow = lax.broadcasted_iota(I32, (LANES, SEL_ROWS), 1) % Q8
    real = qrow < T_NEW
    for j in range(L_SAMPLE // LANES):
        kidx = j * LANES + krow
        valid = ((kidx < PAST_LEN) | ((kidx - PAST_LEN) <= qrow)) & real
        score_ref[j * LANES:(j + 1) * LANES, :] = jnp.where(valid, s_ref[:, j * LANES:(j + 1) * LANES].T, -jnp.inf)
    thr = _select_threshold(score_ref, ithr_ref, L_SAMPLE // LANES, LANES)
    thr_q = jnp.broadcast_to(thr, (LANES, SEL_ROWS))
    ithr_q = jnp.broadcast_to(ithr_ref[0:1, :], (LANES, SEL_ROWS))
    for j in range(L_SAMPLE // LANES):
        bias_ref[:, j * LANES:(j + 1) * LANES] = _bias_block(score_ref, j * LANES, thr_q, ithr_q)


def _select_sample_call(scores):
    rows = scores.shape[0]
    return pl.pallas_call(
        _select_sample_kernel,
        grid=(rows // SEL_ROWS,),
        in_specs=[pl.BlockSpec((SEL_ROWS, L_SAMPLE), lambda i: (i, 0))],
        out_specs=pl.BlockSpec((SEL_ROWS, L_SAMPLE), lambda i: (i, 0)),
        out_shape=jax.ShapeDtypeStruct((rows, L_SAMPLE), F32),
        scratch_shapes=[pltpu.VMEM((L_SAMPLE, SEL_ROWS), F32), pltpu.VMEM((SUBLANES, LANES), I32)],
        compiler_params=_params(("arbitrary",)),
        name="dsa_sample_select",
    )(scores)


def _attn_sample_kernel(pt_ref, qb_ref, bias_ref, kn_ref, vn_ref, *rest):
    nseq = qb_ref.shape[0]
    kpages = rest[:nseq * N_PAGES]
    vpages = rest[nseq * N_PAGES:2 * nseq * N_PAGES]
    o_ref = rest[2 * nseq * N_PAGES]
    s_ref = rest[2 * nseq * N_PAGES + 1]
    del pt_ref
    for u in range(nseq):
        qb = qb_ref[u]
        reps = qb.shape[0] // Q8

        def page_logits(p, kp):
            bias8 = bias_ref[u, :, p * PAGE_SIZE:(p + 1) * PAGE_SIZE]
            s_ref[u, :, p * PAGE_SIZE:(p + 1) * PAGE_SIZE] = (
                _dot_nt(qb, kp) + jnp.concatenate([bias8] * reps, axis=0))

        for p in range(N_PAGES):
            page_logits(p, _kv_page(kpages[u * N_PAGES + p]))
        page_logits(N_PAGES, _pad_new_page(kn_ref[u]))

        s = s_ref[u]
        m = jnp.max(s, axis=1, keepdims=True)
        e = jnp.exp2(s - m)
        l = jnp.sum(e, axis=1, keepdims=True)
        eb = e.astype(BF16)
        acc = _dot(eb[:, PAST_LEN:L_SAMPLE], _pad_new_page(vn_ref[u]))
        for p in range(N_PAGES):
            acc = acc + _dot(eb[:, p * PAGE_SIZE:(p + 1) * PAGE_SIZE], _kv_page(vpages[u * N_PAGES + p]))
        o_ref[u] = acc * (1.0 / l)


def _attn_sample_call(page_table, qblk, bias, knew8, vnew8, cache_k, cache_v):
    n, rows, _ = qblk.shape
    nseq = ATTN_SEQS
    page = (PAGE_SIZE * N_KV_HEADS, HEAD_DIM)
    grid_spec = pltpu.PrefetchScalarGridSpec(
        num_scalar_prefetch=1,
        grid=(n // nseq,),
        in_specs=[
            pl.BlockSpec((nseq, rows, KV_W), lambda i, pt: (i, 0, 0)),
            pl.BlockSpec((nseq, Q8, L_SAMPLE), lambda i, pt: (i, 0, 0)),
            pl.BlockSpec((nseq, Q8, KV_W), lambda i, pt: (i, 0, 0)),
            pl.BlockSpec((nseq, Q8, KV_W), lambda i, pt: (i, 0, 0)),
        ] + _page_specs(page, nseq) + _page_specs(page, nseq),
        out_specs=pl.BlockSpec((nseq, rows, KV_W), lambda i, pt: (i, 0, 0)),
        scratch_shapes=[pltpu.VMEM((nseq, rows, L_SAMPLE), F32)],
    )
    return pl.pallas_call(
        _attn_sample_kernel,
        grid_spec=grid_spec,
        out_shape=jax.ShapeDtypeStruct((n, rows, KV_W), F32),
        compiler_params=_params(("arbitrary",)),
        name="dsa_sample_attn",
    )(page_table, qblk, bias, knew8, vnew8, *([cache_k] * (nseq * N_PAGES)), *([cache_v] * (nseq * N_PAGES)))


def _mlp_resid(x1, g_ffn, shift, scale, gate, wup_ref, wdn_ref, acc_ref):
    h = _rms_mod(x1, g_ffn, shift, scale).astype(BF16)
    for c in range(D_FF // FF_CHUNK):
        u = _dot(h, wup_ref[:, c * FF_CHUNK:(c + 1) * FF_CHUNK])
        u = jnp.square(jnp.maximum(u, 0.0)).astype(BF16)
        d = _dot(u, wdn_ref[c * FF_CHUNK:(c + 1) * FF_CHUNK, :])
        if c == 0:
            acc_ref[...] = d
        else:
            acc_ref[...] += d
    return x1 + gate * acc_ref[...]


def _post0_kernel(x_ref, o_ref, g1_ref, sh2_ref, sc2_ref, g2_ref, gffn_ref, wo_ref, wup_ref, wdn_ref,
                  out_ref, acc_ref):
    x1 = x_ref[0] + g1_ref[0] * _dot(o_ref[0], wo_ref[...])
    out_ref[0] = _mlp_resid(x1, gffn_ref[...], sh2_ref[0], sc2_ref[0], g2_ref[0], wup_ref, wdn_ref, acc_ref)


def _const_spec(shape):
    nd = len(shape)
    return pl.BlockSpec(shape, lambda b, i: (0,) * nd)


def _post0_call(x, o, g1, sh2, sc2, g2, g_ffn, w_o, w_up, w_dn, tm):
    nb, t, _ = x.shape
    tok = lambda w: pl.BlockSpec((1, tm, w), lambda b, i: (b, i, 0))
    return pl.pallas_call(
        _post0_kernel,
        grid=(nb, t // tm),
        in_specs=[tok(D_MODEL), tok(Q_W), g1.spec(tm), sh2.spec(tm), sc2.spec(tm),
                  g2.spec(tm), _const_spec((1, D_MODEL)), _const_spec((Q_W, D_MODEL)),
                  _const_spec((D_MODEL, D_FF)), _const_spec((D_FF, D_MODEL))],
        out_specs=tok(D_MODEL),
        out_shape=jax.ShapeDtypeStruct((nb, t, D_MODEL), F32),
        scratch_shapes=[pltpu.VMEM((tm, D_MODEL), F32)],
        compiler_params=_params(("arbitrary", "arbitrary")),
        name="layer0_out",
    )(x, o, g1.arr, sh2.arr, sc2.arr, g2.arr, g_ffn.reshape(1, D_MODEL), w_o, w_up, w_dn)


def _pool_project(pooled, wp_ref, ps_ref):
    parts = []
    for g in range(len(POOL_WINDOWS)):
        lo = g * POOL_GROUP_DIM
        parts.append(_dot(pooled[:, lo:lo + POOL_GROUP_DIM].astype(BF16), wp_ref[g]))
    return jnp.concatenate(parts, axis=1) * ps_ref[...]


def _layer1_prompt_kernel(x_ref, xh_ref, sh1_ref, sc1_ref, g1_ref, sh2_ref, sc2_ref, g2_ref,
                          gmix_ref, gffn_ref, gfin_ref, wp_ref, ps_ref, wup_ref, wdn_ref,
                          y_ref, hlast_ref, acc_ref):
    i = pl.program_id(1)
    tm = x_ref.shape[1]
    x = x_ref[0]
    sh1, sc1 = sh1_ref[0], sc1_ref[0]
    h = _rms_mod(x, gmix_ref[...], sh1, sc1)
    h_halo = _rms_mod(xh_ref[0], gmix_ref[...], sh1, sc1)
    h_halo = jnp.where(i > 0, h_halo, 0.0)
    ext = jnp.concatenate([h_halo, h], axis=0)
    pos = i * tm + lax.broadcasted_iota(I32, (tm, 1), 0)
    parts = []
    for g, w in enumerate(POOL_WINDOWS):
        lo = g * POOL_GROUP_DIM
        s = ext[:, lo:lo + POOL_GROUP_DIM]
        step = 1
        while step < w:
            s = s + pltpu.roll(s, step, axis=0)
            step *= 2
        cnt = jnp.minimum(w, pos + 1).astype(F32)
        parts.append(s[HALO:, :] / cnt - h[:, lo:lo + POOL_GROUP_DIM])
    pooled = jnp.concatenate(parts, axis=1)
    x1 = x + g1_ref[0] * _pool_project(pooled, wp_ref, ps_ref)
    x2 = _mlp_resid(x1, gffn_ref[...], sh2_ref[0], sc2_ref[0], g2_ref[0], wup_ref, wdn_ref, acc_ref)
    y_ref[0] = _rmsnorm(x2, gfin_ref[...])

    @pl.when(i == pl.num_programs(1) - 1)
    def _():
        hlast_ref[0] = h[tm - HALO:, :]


def _layer1_prompt_call(x, mods, g_mix, g_ffn, g_fin, pool_w, pool_scale, w_up, w_dn, tm):
    nb, t, _ = x.shape
    tok = pl.BlockSpec((1, tm, D_MODEL), lambda b, i: (b, i, 0))
    halo = pl.BlockSpec((1, HALO, D_MODEL), lambda b, i: (b, jnp.maximum(i * (tm // HALO) - 1, 0), 0))
    vec = _const_spec((1, D_MODEL))
    return pl.pallas_call(
        _layer1_prompt_kernel,
        grid=(nb, t // tm),
        in_specs=[tok, halo] + [m.spec(tm) for m in mods] + [vec, vec, vec,
                  _const_spec(pool_w.shape), vec, _const_spec((D_MODEL, D_FF)), _const_spec((D_FF, D_MODEL))],
        out_specs=[tok, pl.BlockSpec((1, HALO, D_MODEL), lambda b, i: (b, 0, 0))],
        out_shape=[jax.ShapeDtypeStruct((nb, t, D_MODEL), F32), jax.ShapeDtypeStruct((nb, HALO, D_MODEL), F32)],
        scratch_shapes=[pltpu.VMEM((tm, D_MODEL), F32)],
        compiler_params=_params(("arbitrary", "arbitrary")),
        name="layer1_prompt",
    )(x, x, *[m.arr for m in mods], g_mix.reshape(1, D_MODEL), g_ffn.reshape(1, D_MODEL),
      g_fin.reshape(1, D_MODEL), pool_w, pool_scale.reshape(1, D_MODEL), w_up, w_dn)


def _layer1_sample_kernel(x_ref, prev_ref, sh1_ref, sc1_ref, g1_ref, sh2_ref, sc2_ref, g2_ref,
                          gmix_ref, gffn_ref, gfin_ref, wp_ref, ps_ref, wup_ref, wdn_ref,
                          y_ref, h_ref, hs_ref, pooled_ref, acc_ref):
    tm = x_ref.shape[1]
    t_new = tm // prev_ref.shape[1]
    nseq = prev_ref.shape[1]
    x = x_ref[0]
    h = _rms_mod(x, gmix_ref[...], sh1_ref[0], sc1_ref[0])
    h_ref[0] = h
    ncol = D_MODEL // LANES
    for c in range(ncol):
        hs_ref[c] = h[:, c * LANES:(c + 1) * LANES]
    ext = [prev_ref[j] for j in range(POOL_STATE)]
    ext += [jnp.concatenate([hs_ref[c, pl.ds(t, nseq, stride=t_new), :] for c in range(ncol)], axis=1)
            for t in range(t_new)]
    for t in range(t_new):
        parts = []
        for g, w in enumerate(POOL_WINDOWS):
            lo = g * POOL_GROUP_DIM
            s = ext[POOL_STATE + t][:, lo:lo + POOL_GROUP_DIM]
            for j in range(1, w):
                s = s + ext[POOL_STATE + t - j][:, lo:lo + POOL_GROUP_DIM]
            parts.append(s / float(w) - ext[POOL_STATE + t][:, lo:lo + POOL_GROUP_DIM])
        pooled_t = jnp.concatenate(parts, axis=1)
        for c in range(ncol):
            pooled_ref[c, pl.ds(t, nseq, stride=t_new), :] = pooled_t[:, c * LANES:(c + 1) * LANES]
    pooled = jnp.concatenate([pooled_ref[c] for c in range(ncol)], axis=1)
    x1 = x + g1_ref[0] * _pool_project(pooled, wp_ref, ps_ref)
    x2 = _mlp_resid(x1, gffn_ref[...], sh2_ref[0], sc2_ref[0], g2_ref[0], wup_ref, wdn_ref, acc_ref)
    y_ref[0] = _rmsnorm(x2, gfin_ref[...])


def _layer1_sample_call(x, prev_t, mods, g_mix, g_ffn, g_fin, pool_w, pool_scale, w_up, w_dn, tm, t_new):
    nb, t, _ = x.shape
    tok = pl.BlockSpec((1, tm, D_MODEL), lambda b, i: (b, i, 0))
    prev = pl.BlockSpec((POOL_STATE, tm // t_new, D_MODEL), lambda b, i: (0, i, 0))
    vec = _const_spec((1, D_MODEL))
    return pl.pallas_call(
        _layer1_sample_kernel,
        grid=(nb, t // tm),
        in_specs=[tok, prev] + [m.spec(tm) for m in mods] + [vec, vec, vec,
                  _const_spec(pool_w.shape), vec, _const_spec((D_MODEL, D_FF)), _const_spec((D_FF, D_MODEL))],
        out_specs=[tok, tok],
        out_shape=[jax.ShapeDtypeStruct((nb, t, D_MODEL), F32), jax.ShapeDtypeStruct((nb, t, D_MODEL), F32)],
        scratch_shapes=[pltpu.VMEM((D_MODEL // LANES, tm, LANES), F32),
                        pltpu.VMEM((D_MODEL // LANES, tm, LANES), F32),
                        pltpu.VMEM((tm, D_MODEL), F32)],
        compiler_params=_params(("arbitrary", "arbitrary")),
        name="layer1_sample",
    )(x, prev_t, *[m.arr for m in mods], g_mix.reshape(1, D_MODEL), g_ffn.reshape(1, D_MODEL),
      g_fin.reshape(1, D_MODEL), pool_w, pool_scale.reshape(1, D_MODEL), w_up, w_dn)


def kernel(x_prompt, x_sample, cache_k, cache_v, cache_kidx, state_pool, page_table, c_prompt, c_sample,
           norm_mix_g, norm_ffn_g, w_mod, b_mod, dsa_w_in, dsa_w_o, pool_w, pool_scale, w_up, w_down,
           final_norm_g):
    nbp, seq, _ = x_prompt.shape
    nbs, t_new, _ = x_sample.shape
    n_tok_s = nbs * t_new

    w_in_b = jnp.pad(dsa_w_in[0], ((0, 0), (0, D_PROJ_PAD - D_PROJ))).astype(BF16)
    w_o_b = dsa_w_o[0].astype(BF16)
    w_up_b = w_up.astype(BF16)
    w_dn_b = w_down.astype(BF16)
    pool_w_b = pool_w[0].astype(BF16)

    n_c = nbp + nbs
    c_rows = -(-n_c // SUBLANES) * SUBLANES
    c_all = jnp.pad(jnp.concatenate([c_prompt, c_sample], axis=0), ((0, c_rows - n_c), (0, 0)))
    mod, mod_tok = _mod_call(c_all, w_mod, b_mod, nbp, nbs, t_new)

    def mods_prompt(layer):
        m = mod[layer, :nbp].reshape(nbp, 1, 6, D_MODEL)
        return [_mod_per_sequence(m[:, :, j]) for j in range(6)]

    def mods_sample(layer):
        return [_mod_per_token(mod_tok, layer, j) for j in range(6)]

    sh1, sc1, g1, sh2, sc2, g2 = mods_prompt(0)
    k_p, v_p, ki_p, q, qi, kiw, kb, vb, kib = _proj_call(x_prompt, sh1, sc1, norm_mix_g[0], w_in_b, 512)
    o = _dsa_prompt_call(q, qi, kiw, kb, vb, kib)
    x1 = _post0_call(x_prompt, o, g1, sh2, sc2, g2, norm_ffn_g[0], w_o_b, w_up_b[0], w_dn_b[0], 512)
    y_prompt, hlast = _layer1_prompt_call(x1, mods_prompt(1), norm_mix_g[1], norm_ffn_g[1], final_norm_g,
                                          pool_w_b, pool_scale[0], w_up_b[1], w_dn_b[1], 512)
    pool_p = hlast[:, HALO - POOL_STATE:]

    sh1, sc1, g1, sh2, sc2, g2 = mods_sample(0)
    xs = x_sample.reshape(1, n_tok_s, D_MODEL)
    tms = 256
    k_s, v_s, ki_s, q, qi, kiw, _, _, _ = _proj_call(xs, sh1, sc1, norm_mix_g[0], w_in_b, tms)

    pad_q = lambda a: jnp.pad(a, [(0, 0)] * (a.ndim - 2) + [(0, Q8 - t_new), (0, 0)])
    qis = pad_q(qi.reshape(nbs, t_new, IDX_HEADS, IDX_DIM).transpose(0, 2, 1, 3)).reshape(nbs, IDX_HEADS * Q8, IDX_DIM)
    wi = kiw.reshape(nbs, t_new, KIW_W)[:, :, IDX_DIM:IDX_DIM + IDX_HEADS] * IDX_HEAD_SCALE
    ws = pad_q(wi.transpose(0, 2, 1)[..., None]).reshape(nbs, IDX_HEADS * Q8, 1)
    knew8 = pad_q(k_s.reshape(nbs, t_new, KV_W))
    vnew8 = pad_q(v_s.reshape(nbs, t_new, KV_W))
    kinew_t = jnp.pad(ki_s.reshape(nbs, t_new, IDX_DIM).transpose(0, 2, 1), ((0, 0), (0, 0), (0, PAGE_SIZE - t_new)))
    scores = _idx_sample_call(page_table, qis, ws, kinew_t, cache_kidx[0].transpose(0, 2, 1))
    bias = _select_sample_call(scores.reshape(nbs * Q8, L_SAMPLE)).reshape(nbs, Q8, L_SAMPLE)
    qg = pad_q(q.reshape(nbs, t_new, N_KV_HEADS, GROUP, HEAD_DIM).transpose(0, 2, 3, 1, 4))
    eye = jnp.eye(N_KV_HEADS, dtype=BF16)
    qblk = (qg[:, :, :, :, None, :] * eye[None, :, None, None, :, None]).reshape(nbs, N_KV_HEADS * GROUP * Q8, KV_W)
    n_pool = cache_k.shape[1]
    page_rows = lambda c: c[0].reshape(n_pool, PAGE_SIZE * N_KV_HEADS, HEAD_DIM)
    o_blk = _attn_sample_call(page_table, qblk, bias, knew8, vnew8, page_rows(cache_k), page_rows(cache_v))
    o_blk = o_blk.reshape(nbs, N_KV_HEADS, GROUP, Q8, N_KV_HEADS, HEAD_DIM)
    o_s = jnp.stack([o_blk[:, g, :, :t_new, g] for g in range(N_KV_HEADS)], axis=1)
    o_s = o_s.transpose(0, 3, 1, 2, 4).reshape(1, n_tok_s, Q_W).astype(BF16)
    x1s = _post0_call(xs, o_s, g1, sh2, sc2, g2, norm_ffn_g[0], w_o_b, w_up_b[0], w_dn_b[0], tms)
    prev_t = state_pool[0].transpose(1, 0, 2)
    y_s, h1s = _layer1_sample_call(x1s, prev_t, mods_sample(1), norm_mix_g[1], norm_ffn_g[1], final_norm_g,
                                   pool_w_b, pool_scale[0], w_up_b[1], w_dn_b[1], tms, t_new)
    y_sample = y_s.reshape(nbs, t_new, D_MODEL)
    pool_s = jnp.concatenate([state_pool[0][:, t_new:], h1s.reshape(nbs, t_new, D_MODEL)], axis=1)

    return (
        y_prompt, y_sample,
        k_p[None], v_p[None], ki_p[None], pool_p[None],
        k_s.reshape(1, nbs, t_new, N_KV_HEADS, HEAD_DIM), v_s.reshape(1, nbs, t_new, N_KV_HEADS, HEAD_DIM),
        ki_s.reshape(1, nbs, t_new, IDX_DIM), pool_s[None],
    )
```

```python
import functools
from typing import Callable, NamedTuple

import jax
import jax.numpy as jnp
from jax import lax
from jax.experimental import pallas as pl
from jax.experimental.pallas import tpu as pltpu

F32 = jnp.float32
BF16 = jnp.bfloat16
I32 = jnp.int32

D_MODEL = 1024
DEPTH = 2
PAST_LEN = 2048
PAGE_SIZE = 128
N_PAGES = PAST_LEN // PAGE_SIZE
N_HEADS = 8
HEAD_DIM = 128
N_KV_HEADS = 2
GROUP = N_HEADS // N_KV_HEADS
IDX_HEADS = 8
IDX_DIM = 64
TOPK = 256
POOL_WINDOWS = (2, 4, 8, 16)
POOL_GROUP_DIM = D_MODEL // len(POOL_WINDOWS)
POOL_STATE = max(POOL_WINDOWS) - 1
D_FF = 4 * D_MODEL
RMS_EPS = 1e-6
Q_W = N_HEADS * HEAD_DIM
KV_W = N_KV_HEADS * HEAD_DIM
QI_W = IDX_HEADS * IDX_DIM
D_PROJ = Q_W + 2 * KV_W + QI_W + IDX_DIM + IDX_HEADS

LANES = 128
SUBLANES = 8
VMEM_LIMIT_BYTES = 56 * 1024 * 1024

KIW_W = LANES
D_PROJ_PAD = Q_W + 2 * KV_W + QI_W + KIW_W
OFF_K = Q_W
OFF_V = OFF_K + KV_W
OFF_QI = OFF_V + KV_W
OFF_KIW = OFF_QI + QI_W

LOG2_E = 1.4426950408889634
ATT_SCALE = HEAD_DIM ** -0.5 * LOG2_E
IDX_SCALE = IDX_DIM ** -0.5
IDX_HEAD_SCALE = IDX_HEADS ** -0.5

INT_MIN = -(2 ** 31)
INT_MAX = 2 ** 31 - 1
F32_MAX = 3.4028234663852886e38
NEG_INF_KEY = INT_MIN + (1 << 23)
NEG_BIG = -1e30

TQ = 128
TK = 512
FF_CHUNK = 512
HALO = 16
SEL_ROWS = 128
L_SAMPLE = PAST_LEN + PAGE_SIZE
T_NEW = 4
SCORE_SEQS = 4
ATTN_SEQS = 2
Q8 = SUBLANES


def _params(sem):
    return pltpu.CompilerParams(dimension_semantics=sem, vmem_limit_bytes=VMEM_LIMIT_BYTES)


def _dot(a, b):
    return jnp.dot(a, b, preferred_element_type=F32)


def _dot_nt(a, b):
    return lax.dot_general(a, b, (((1,), (1,)), ((), ())), preferred_element_type=F32)


def _rms_mod(x, g, shift, scale):
    ms = jnp.mean(x * x, axis=-1, keepdims=True)
    y = x * lax.rsqrt(ms + RMS_EPS) * g
    return y * (1.0 + scale) + shift


def _rmsnorm(x, g):
    ms = jnp.mean(x * x, axis=-1, keepdims=True)
    return x * lax.rsqrt(ms + RMS_EPS) * g


def _split3(x):
    hi = x.astype(BF16)
    r1 = x - hi.astype(F32)
    mid = r1.astype(BF16)
    lo = (r1 - mid.astype(F32)).astype(BF16)
    return hi, mid, lo


def _mod_kernel(c_ref, w_ref, b_ref, o_ref, tok_ref, *, n_prompt, t_new):
    c = c_ref[...]
    a = c * (1.0 / (1.0 + jnp.exp(-c)))
    w = w_ref[0]
    a_hi = a.astype(BF16)
    a_lo = (a - a_hi.astype(F32)).astype(BF16)
    w_hi = w.astype(BF16)
    w_lo = (w - w_hi.astype(F32)).astype(BF16)
    mod = _dot(a_hi, w_hi) + _dot(a_lo, w_hi) + _dot(a_hi, w_lo) + b_ref[0]
    o_ref[0] = mod
    n_tok, rows = tok_ref.shape[1], mod.shape[0]
    r = lax.broadcasted_iota(I32, (n_tok, rows), 0)
    s = (lax.broadcasted_iota(I32, (n_tok, rows), 1) - n_prompt) * t_new
    pick = jnp.where((r >= s) & (r < s + t_new), 1.0, 0.0).astype(BF16)
    hi, mid, lo = _split3(mod)
    tok_ref[0] = (_dot(pick, hi) + _dot(pick, mid)) + _dot(pick, lo)


def _mod_call(c_all, w_mod, b_mod, n_prompt, n_sample, t_new):
    rows = c_all.shape[0]
    tn = 1536
    return pl.pallas_call(
        functools.partial(_mod_kernel, n_prompt=n_prompt, t_new=t_new),
        grid=(DEPTH, 6 * D_MODEL // tn),
        in_specs=[
            pl.BlockSpec((rows, D_MODEL), lambda i, j: (0, 0)),
            pl.BlockSpec((1, D_MODEL, tn), lambda i, j: (i, 0, j)),
            pl.BlockSpec((1, 1, tn), lambda i, j: (i, 0, j)),
        ],
        out_specs=[pl.BlockSpec((1, rows, tn), lambda i, j: (i, 0, j)),
                   pl.BlockSpec((1, n_sample * t_new, tn), lambda i, j: (i, 0, j))],
        out_shape=[jax.ShapeDtypeStruct((DEPTH, rows, 6 * D_MODEL), F32),
                   jax.ShapeDtypeStruct((DEPTH, n_sample * t_new, 6 * D_MODEL), F32)],
        compiler_params=_params(("arbitrary", "arbitrary")),
        name="mod_vectors",
    )(c_all, w_mod, b_mod.reshape(DEPTH, 1, 6 * D_MODEL))


def _proj_kernel(x_ref, sh_ref, sc_ref, g_ref, w_ref,
                 k_ref, v_ref, ki_ref, q_ref, qi_ref, kiw_ref, kb_ref, vb_ref, kib_ref):
    h = _rms_mod(x_ref[0], g_ref[...], sh_ref[0], sc_ref[0]).astype(BF16)
    q_ref[0] = (_dot(h, w_ref[:, 0:Q_W]) * ATT_SCALE).astype(BF16)
    k = _dot(h, w_ref[:, OFF_K:OFF_K + KV_W])
    v = _dot(h, w_ref[:, OFF_V:OFF_V + KV_W])
    for g in range(N_KV_HEADS):
        k_ref[0, :, g, :] = k[:, g * HEAD_DIM:(g + 1) * HEAD_DIM]
        v_ref[0, :, g, :] = v[:, g * HEAD_DIM:(g + 1) * HEAD_DIM]
    kb_ref[0] = k.astype(BF16)
    vb_ref[0] = v.astype(BF16)
    qi_ref[0] = (_dot(h, w_ref[:, OFF_QI:OFF_QI + QI_W]) * IDX_SCALE).astype(BF16)
    kiw = _dot(h, w_ref[:, OFF_KIW:OFF_KIW + KIW_W])
    kiw_ref[0] = kiw
    ki = kiw[:, 0:IDX_DIM]
    ki_ref[0] = ki
    kib_ref[0] = ki.astype(BF16)


class _Mod(NamedTuple):
    arr: jax.Array
    spec: Callable[[int], pl.BlockSpec]


def _mod_per_sequence(arr):
    return _Mod(arr, lambda tm: pl.BlockSpec((1, 1, D_MODEL), lambda b, i: (b, 0, 0)))


def _mod_per_token(mod_tok, layer, j):
    return _Mod(mod_tok, lambda tm: pl.BlockSpec((1, tm, D_MODEL), lambda b, i: (layer, i, j)))


def _proj_call(x, shift, scale, g, w_in_b, tm):
    nb, t, _ = x.shape
    tok = lambda w: pl.BlockSpec((1, tm, w), lambda b, i: (b, i, 0))
    shp = lambda w, dt: jax.ShapeDtypeStruct((nb, t, w), dt)
    heads = pl.BlockSpec((1, tm, N_KV_HEADS, HEAD_DIM), lambda b, i: (b, i, 0, 0))
    heads_shape = jax.ShapeDtypeStruct((nb, t, N_KV_HEADS, HEAD_DIM), F32)
    return pl.pallas_call(
        _proj_kernel,
        grid=(nb, t // tm),
        in_specs=[
            tok(D_MODEL), shift.spec(tm), scale.spec(tm),
            pl.BlockSpec((1, D_MODEL), lambda b, i: (0, 0)),
            pl.BlockSpec((D_MODEL, D_PROJ_PAD), lambda b, i: (0, 0)),
        ],
        out_specs=[heads, heads, tok(IDX_DIM), tok(Q_W), tok(QI_W), tok(KIW_W),
                   tok(KV_W), tok(KV_W), tok(IDX_DIM)],
        out_shape=[heads_shape, heads_shape, shp(IDX_DIM, F32), shp(Q_W, BF16), shp(QI_W, BF16),
                   shp(KIW_W, F32), shp(KV_W, BF16), shp(KV_W, BF16), shp(IDX_DIM, BF16)],
        compiler_params=_params(("arbitrary", "arbitrary")),
        name="dsa_proj",
    )(x, shift.arr, scale.arr, g.reshape(1, D_MODEL), w_in_b)


def _key_float(k):
    return lax.bitcast_convert_type(jnp.where(k >= 0, k, (k - 1) ^ INT_MAX), F32)


def _select_threshold(score_ref, ithr_ref, nblk, blk):
    nsub = blk // SUBLANES
    nacc = 4
    row8 = lax.broadcasted_iota(I32, (SUBLANES, LANES), 0)

    def count(pred):
        def body(c, accs):
            accs = list(accs)
            c0 = pl.multiple_of(c * blk, blk)
            rows = score_ref[pl.ds(c0, blk), :]
            for j in range(nsub):
                hit = pred(rows[j * SUBLANES:(j + 1) * SUBLANES], c0 + j * SUBLANES)
                accs[j % nacc] = accs[j % nacc] + jnp.where(hit, 1, 0)
            return tuple(accs)
        accs = lax.fori_loop(0, nblk, body, tuple(jnp.zeros((SUBLANES, LANES), I32) for _ in range(nacc)))
        tot = (accs[0] + accs[1]) + (accs[2] + accs[3])
        return jnp.sum(tot, axis=0, keepdims=True)

    def bcast(v):
        return jnp.broadcast_to(v, (SUBLANES, LANES))

    def count_ge(probe):
        probe_b = bcast(probe)
        return count(lambda rows, row0: rows >= probe_b)

    n = count_ge(jnp.zeros((1, LANES), F32))
    ok = n >= TOPK
    thr_k = jnp.where(ok, 0, INT_MIN).astype(I32)
    n_ge = jnp.where(ok, n, INT_MAX).astype(I32)

    def bit_body(t, carry):
        thr_k, n_ge = carry
        cand_k = thr_k | lax.shift_left(jnp.int32(1), 30 - t)
        n = count_ge(_key_float(cand_k))
        ok = n >= TOPK
        return jnp.where(ok, cand_k, thr_k), jnp.where(ok, n, n_ge)

    thr_k, n_ge = lax.fori_loop(0, 31, bit_body, (thr_k, n_ge))
    found = thr_k > NEG_INF_KEY
    thr = jnp.where(found, _key_float(thr_k), -F32_MAX)
    thr_b = bcast(thr)
    tie = found & (n_ge > TOPK)
    ithr_ref[...] = jnp.full((SUBLANES, LANES), INT_MAX, I32)

    @pl.when(jnp.max(tie.astype(I32)) > 0)
    def _():
        n_gt = count(lambda rows, row0: rows > thr_b)
        need = TOPK - n_gt

        def idx_body(t, ithr):
            cand = ithr | lax.shift_left(jnp.int32(1), 11 - t)
            cand_b = bcast(cand)
            n = count(lambda rows, row0: (rows == thr_b) & ((row0 + row8) < cand_b))
            return jnp.where(n < need, cand, ithr)

        ithr = lax.fori_loop(0, 12, idx_body, jnp.zeros((1, LANES), I32))
        ithr_ref[...] = jnp.where(bcast(tie), bcast(ithr), ithr_ref[...])

    return thr


def _bias_block(score_ref, row0, thr_q, ithr_q):
    sc = score_ref[pl.ds(row0, LANES), :]
    key_index = row0 + lax.broadcasted_iota(I32, (LANES, LANES), 0)
    sel = (sc > thr_q) | ((sc == thr_q) & (key_index <= ithr_q))
    return jnp.where(sel, 0.0, NEG_BIG).astype(F32).T


def _dsa_prompt_kernel(q_ref, qi_ref, kiw_ref, kb_ref, vb_ref, kib_ref, o_ref,
                       score_ref, ithr_ref, qir_ref, s_ref, mlane_ref, llane_ref, acc_ref):
    i = pl.program_id(1)
    nchunk = (i * TQ + TQ + TK - 1) // TK
    qpos = i * TQ + lax.broadcasted_iota(I32, (1, LANES), 1)

    def for_chunks(body):
        def quad(c, carry):
            body(pl.multiple_of(c * (4 * TK), 4 * TK), 4 * TK)
            return carry
        nquad = nchunk // 4
        lax.fori_loop(0, nquad, quad, 0)

        @pl.when(nchunk % 4 >= 2)
        def _():
            body(pl.multiple_of(nquad * (4 * TK), 2 * TK), 2 * TK)

        @pl.when(nchunk % 2 == 1)
        def _():
            body(pl.multiple_of((nchunk - 1) * TK, TK), TK)

    for h in range(IDX_HEADS):
        qir_ref[h * TQ:(h + 1) * TQ, :] = qi_ref[0, :, h * IDX_DIM:(h + 1) * IDX_DIM]
    kiw_t = kiw_ref[0].T
    w_rows = [kiw_t[IDX_DIM + h:IDX_DIM + h + 1, :] * IDX_HEAD_SCALE for h in range(IDX_HEADS)]

    def idx_body(c0, width):
        logit = _dot_nt(kib_ref[0, pl.ds(c0, width), :], qir_ref[...])
        score = jnp.zeros((width, LANES), F32)
        for h in range(IDX_HEADS):
            score = score + jnp.maximum(logit[:, h * TQ:(h + 1) * TQ], 0.0) * w_rows[h]
        krow = c0 + lax.broadcasted_iota(I32, (width, LANES), 0)
        score_ref[pl.ds(c0, width), :] = jnp.where(krow <= qpos, score, -jnp.inf)

    for_chunks(idx_body)
    thr = _select_threshold(score_ref, ithr_ref, nchunk, TK)
    thr_q = jnp.broadcast_to(thr, (LANES, LANES))
    ithr_q = jnp.broadcast_to(ithr_ref[0:1, :], (LANES, LANES))

    def q_group(g):
        return jnp.concatenate(
            [q_ref[0, :, (g * GROUP + r) * HEAD_DIM:(g * GROUP + r + 1) * HEAD_DIM] for r in range(GROUP)], axis=0)

    mlane_ref[...] = jnp.full(mlane_ref.shape, NEG_BIG, F32)

    def logits_body(c0, width):
        bias = jnp.concatenate(
            [_bias_block(score_ref, pl.multiple_of(c0 + j * LANES, LANES), thr_q, ithr_q)
             for j in range(width // LANES)], axis=1)
        bias4 = jnp.concatenate([bias] * GROUP, axis=0)
        for g in range(N_KV_HEADS):
            s = _dot_nt(q_group(g), kb_ref[0, pl.ds(c0, width), g * HEAD_DIM:(g + 1) * HEAD_DIM]) + bias4
            s_ref[g, :, pl.ds(c0, width)] = s
            m = mlane_ref[g]
            for j in range(width // LANES):
                m = jnp.maximum(m, s[:, j * LANES:(j + 1) * LANES])
            mlane_ref[g] = m

    for_chunks(logits_body)

    for g in range(N_KV_HEADS):
        mlane_ref[g] = jnp.broadcast_to(jnp.max(mlane_ref[g], axis=1, keepdims=True), mlane_ref.shape[1:])
    llane_ref[...] = jnp.zeros(llane_ref.shape, F32)
    acc_ref[...] = jnp.zeros(acc_ref.shape, F32)

    def pv_body(c0, width):
        for g in range(N_KV_HEADS):
            m = mlane_ref[g]
            l = llane_ref[g]
            ps = []
            for j in range(width // LANES):
                pj = jnp.exp2(s_ref[g, :, pl.ds(pl.multiple_of(c0 + j * LANES, LANES), LANES)] - m)
                l = l + pj
                ps.append(pj.astype(BF16))
            llane_ref[g] = l
            p = jnp.concatenate(ps, axis=1)
            acc_ref[g] += _dot(p, vb_ref[0, pl.ds(c0, width), g * HEAD_DIM:(g + 1) * HEAD_DIM])

    for_chunks(pv_body)

    for g in range(N_KV_HEADS):
        og = acc_ref[g] * (1.0 / jnp.sum(llane_ref[g], axis=1, keepdims=True))
        for r in range(GROUP):
            col = (g * GROUP + r) * HEAD_DIM
            o_ref[0, :, col:col + HEAD_DIM] = og[r * TQ:(r + 1) * TQ].astype(BF16)


def _dsa_prompt_call(q, qi, kiw, kb, vb, kib):
    nb, s, _ = q.shape
    tile = lambda w: pl.BlockSpec((1, TQ, w), lambda b, i: (b, i, 0))
    full = lambda w: pl.BlockSpec((1, s, w), lambda b, i: (b, 0, 0))
    rows = GROUP * TQ
    return pl.pallas_call(
        _dsa_prompt_kernel,
        grid=(nb, s // TQ),
        in_specs=[tile(Q_W), tile(QI_W), tile(KIW_W), full(KV_W), full(KV_W), full(IDX_DIM)],
        out_specs=tile(Q_W),
        out_shape=jax.ShapeDtypeStruct((nb, s, Q_W), BF16),
        scratch_shapes=[
            pltpu.VMEM((s, TQ), F32),
            pltpu.VMEM((SUBLANES, LANES), I32),
            pltpu.VMEM((IDX_HEADS * TQ, IDX_DIM), BF16),
            pltpu.VMEM((N_KV_HEADS, rows, s), F32),
            pltpu.VMEM((N_KV_HEADS, rows, LANES), F32),
            pltpu.VMEM((N_KV_HEADS, rows, LANES), F32),
            pltpu.VMEM((N_KV_HEADS, rows, HEAD_DIM), F32),
        ],
        compiler_params=_params(("arbitrary", "arbitrary")),
        name="dsa_prompt",
    )(q, qi, kiw, kb, vb, kib)


def _page_specs(page_shape, nseq):
    zeros = (0,) * len(page_shape)
    return [pl.BlockSpec((1,) + page_shape,
                         functools.partial(lambda i, pt, u, p: (pt[i * nseq + u, p],) + zeros, u=u, p=p))
            for u in range(nseq) for p in range(N_PAGES)]


def _kv_page(page_ref):
    heads = [page_ref[0, pl.ds(g, PAGE_SIZE, stride=N_KV_HEADS), :] for g in range(N_KV_HEADS)]
    return jnp.concatenate(heads, axis=1).astype(BF16)


def _pad_new_page(new8):
    return jnp.concatenate([new8, jnp.zeros((PAGE_SIZE - Q8, new8.shape[1]), F32)], axis=0).astype(BF16)


def _idx_sample_kernel(pt_ref, qis_ref, ws_ref, kin_ref, *rest):
    nseq = qis_ref.shape[0]
    pages = rest[:nseq * N_PAGES]
    o_ref = rest[nseq * N_PAGES]
    del pt_ref
    for u in range(nseq):
        qs = qis_ref[u]
        w = ws_ref[u]

        def page_score(kt):
            r = jnp.maximum(_dot(qs, kt.astype(BF16)), 0.0) * w
            s = r[0:Q8]
            for h in range(1, IDX_HEADS):
                s = s + r[h * Q8:(h + 1) * Q8]
            return s

        for p in range(N_PAGES):
            o_ref[u, :, p * PAGE_SIZE:(p + 1) * PAGE_SIZE] = page_score(pages[u * N_PAGES + p][0])
        o_ref[u, :, PAST_LEN:L_SAMPLE] = page_score(kin_ref[u])


def _idx_sample_call(page_table, qis, ws, kinew_t, cache_ki_t):
    n = qis.shape[0]
    rows = IDX_HEADS * Q8
    nseq = SCORE_SEQS
    grid_spec = pltpu.PrefetchScalarGridSpec(
        num_scalar_prefetch=1,
        grid=(n // nseq,),
        in_specs=[
            pl.BlockSpec((nseq, rows, IDX_DIM), lambda i, pt: (i, 0, 0)),
            pl.BlockSpec((nseq, rows, 1), lambda i, pt: (i, 0, 0)),
            pl.BlockSpec((nseq, IDX_DIM, PAGE_SIZE), lambda i, pt: (i, 0, 0)),
        ] + _page_specs((IDX_DIM, PAGE_SIZE), nseq),
        out_specs=pl.BlockSpec((nseq, Q8, L_SAMPLE), lambda i, pt: (i, 0, 0)),
    )
    return pl.pallas_call(
        _idx_sample_kernel,
        grid_spec=grid_spec,
        out_shape=jax.ShapeDtypeStruct((n, Q8, L_SAMPLE), F32),
        compiler_params=_params(("arbitrary",)),
        name="dsa_sample_scores",
    )(page_table, qis, ws, kinew_t, *([cache_ki_t] * (nseq * N_PAGES)))


def _select_sample_kernel(s_ref, bias_ref, score_ref, ithr_ref):
    krow = lax.broadcasted_iota(I32, (LANES, SEL_ROWS), 0)
    qrow = lax.broadcasted_iota(I32, (LANES, SEL_ROWS), 1) % Q8
    real = qrow < T_NEW
    for j in range(L_SAMPLE // LANES):
        kidx = j * LANES + krow
        valid = ((kidx < PAST_LEN) | ((kidx - PAST_LEN) <= qrow)) & real
        score_ref[j * LANES:(j + 1) * LANES, :] = jnp.where(valid, s_ref[:, j * LANES:(j + 1) * LANES].T, -jnp.inf)
    thr = _select_threshold(score_ref, ithr_ref, L_SAMPLE // LANES, LANES)
    thr_q = jnp.broadcast_to(thr, (LANES, SEL_ROWS))
    ithr_q = jnp.broadcast_to(ithr_ref[0:1, :], (LANES, SEL_ROWS))
    for j in range(L_SAMPLE // LANES):
        bias_ref[:, j * LANES:(j + 1) * LANES] = _bias_block(score_ref, j * LANES, thr_q, ithr_q)


def _select_sample_call(scores):
    rows = scores.shape[0]
    return pl.pallas_call(
        _select_sample_kernel,
        grid=(rows // SEL_ROWS,),
        in_specs=[pl.BlockSpec((SEL_ROWS, L_SAMPLE), lambda i: (i, 0))],
        out_specs=pl.BlockSpec((SEL_ROWS, L_SAMPLE), lambda i: (i, 0)),
        out_shape=jax.ShapeDtypeStruct((rows, L_SAMPLE), F32),
        scratch_shapes=[pltpu.VMEM((L_SAMPLE, SEL_ROWS), F32), pltpu.VMEM((SUBLANES, LANES), I32)],
        compiler_params=_params(("arbitrary",)),
        name="dsa_sample_select",
    )(scores)


def _attn_sample_kernel(pt_ref, qb_ref, bias_ref, kn_ref, vn_ref, *rest):
    nseq = qb_ref.shape[0]
    kpages = rest[:nseq * N_PAGES]
    vpages = rest[nseq * N_PAGES:2 * nseq * N_PAGES]
    o_ref = rest[2 * nseq * N_PAGES]
    s_ref = rest[2 * nseq * N_PAGES + 1]
    del pt_ref
    for u in range(nseq):
        qb = qb_ref[u]
        reps = qb.shape[0] // Q8

        def page_logits(p, kp):
            bias8 = bias_ref[u, :, p * PAGE_SIZE:(p + 1) * PAGE_SIZE]
            s_ref[u, :, p * PAGE_SIZE:(p + 1) * PAGE_SIZE] = (
                _dot_nt(qb, kp) + jnp.concatenate([bias8] * reps, axis=0))

        for p in range(N_PAGES):
            page_logits(p, _kv_page(kpages[u * N_PAGES + p]))
        page_logits(N_PAGES, _pad_new_page(kn_ref[u]))

        s = s_ref[u]
        m = jnp.max(s, axis=1, keepdims=True)
        e = jnp.exp2(s - m)
        l = jnp.sum(e, axis=1, keepdims=True)
        eb = e.astype(BF16)
        acc = _dot(eb[:, PAST_LEN:L_SAMPLE], _pad_new_page(vn_ref[u]))
        for p in range(N_PAGES):
            acc = acc + _dot(eb[:, p * PAGE_SIZE:(p + 1) * PAGE_SIZE], _kv_page(vpages[u * N_PAGES + p]))
        o_ref[u] = acc * (1.0 / l)


def _attn_sample_call(page_table, qblk, bias, knew8, vnew8, cache_k, cache_v):
    n, rows, _ = qblk.shape
    nseq = ATTN_SEQS
    page = (PAGE_SIZE * N_KV_HEADS, HEAD_DIM)
    grid_spec = pltpu.PrefetchScalarGridSpec(
        num_scalar_prefetch=1,
        grid=(n // nseq,),
        in_specs=[
            pl.BlockSpec((nseq, rows, KV_W), lambda i, pt: (i, 0, 0)),
            pl.BlockSpec((nseq, Q8, L_SAMPLE), lambda i, pt: (i, 0, 0)),
            pl.BlockSpec((nseq, Q8, KV_W), lambda i, pt: (i, 0, 0)),
            pl.BlockSpec((nseq, Q8, KV_W), lambda i, pt: (i, 0, 0)),
        ] + _page_specs(page, nseq) + _page_specs(page, nseq),
        out_specs=pl.BlockSpec((nseq, rows, KV_W), lambda i, pt: (i, 0, 0)),
        scratch_shapes=[pltpu.VMEM((nseq, rows, L_SAMPLE), F32)],
    )
    return pl.pallas_call(
        _attn_sample_kernel,
        grid_spec=grid_spec,
        out_shape=jax.ShapeDtypeStruct((n, rows, KV_W), F32),
        compiler_params=_params(("arbitrary",)),
        name="dsa_sample_attn",
    )(page_table, qblk, bias, knew8, vnew8, *([cache_k] * (nseq * N_PAGES)), *([cache_v] * (nseq * N_PAGES)))


def _mlp_resid(x1, g_ffn, shift, scale, gate, wup_ref, wdn_ref, acc_ref):
    h = _rms_mod(x1, g_ffn, shift, scale).astype(BF16)
    for c in range(D_FF // FF_CHUNK):
        u = _dot(h, wup_ref[:, c * FF_CHUNK:(c + 1) * FF_CHUNK])
        u = jnp.square(jnp.maximum(u, 0.0)).astype(BF16)
        d = _dot(u, wdn_ref[c * FF_CHUNK:(c + 1) * FF_CHUNK, :])
        if c == 0:
            acc_ref[...] = d
        else:
            acc_ref[...] += d
    return x1 + gate * acc_ref[...]


def _post0_kernel(x_ref, o_ref, g1_ref, sh2_ref, sc2_ref, g2_ref, gffn_ref, wo_ref, wup_ref, wdn_ref,
                  out_ref, acc_ref):
    x1 = x_ref[0] + g1_ref[0] * _dot(o_ref[0], wo_ref[...])
    out_ref[0] = _mlp_resid(x1, gffn_ref[...], sh2_ref[0], sc2_ref[0], g2_ref[0], wup_ref, wdn_ref, acc_ref)


def _const_spec(shape):
    nd = len(shape)
    return pl.BlockSpec(shape, lambda b, i: (0,) * nd)


def _post0_call(x, o, g1, sh2, sc2, g2, g_ffn, w_o, w_up, w_dn, tm):
    nb, t, _ = x.shape
    tok = lambda w: pl.BlockSpec((1, tm, w), lambda b, i: (b, i, 0))
    return pl.pallas_call(
        _post0_kernel,
        grid=(nb, t // tm),
        in_specs=[tok(D_MODEL), tok(Q_W), g1.spec(tm), sh2.spec(tm), sc2.spec(tm),
                  g2.spec(tm), _const_spec((1, D_MODEL)), _const_spec((Q_W, D_MODEL)),
                  _const_spec((D_MODEL, D_FF)), _const_spec((D_FF, D_MODEL))],
        out_specs=tok(D_MODEL),
        out_shape=jax.ShapeDtypeStruct((nb, t, D_MODEL), F32),
        scratch_shapes=[pltpu.VMEM((tm, D_MODEL), F32)],
        compiler_params=_params(("arbitrary", "arbitrary")),
        name="layer0_out",
    )(x, o, g1.arr, sh2.arr, sc2.arr, g2.arr, g_ffn.reshape(1, D_MODEL), w_o, w_up, w_dn)


def _pool_project(pooled, wp_ref, ps_ref):
    parts = []
    for g in range(len(POOL_WINDOWS)):
        lo = g * POOL_GROUP_DIM
        parts.append(_dot(pooled[:, lo:lo + POOL_GROUP_DIM].astype(BF16), wp_ref[g]))
    return jnp.concatenate(parts, axis=1) * ps_ref[...]


def _layer1_prompt_kernel(x_ref, xh_ref, sh1_ref, sc1_ref, g1_ref, sh2_ref, sc2_ref, g2_ref,
                          gmix_ref, gffn_ref, gfin_ref, wp_ref, ps_ref, wup_ref, wdn_ref,
                          y_ref, hlast_ref, acc_ref):
    i = pl.program_id(1)
    tm = x_ref.shape[1]
    x = x_ref[0]
    sh1, sc1 = sh1_ref[0], sc1_ref[0]
    h = _rms_mod(x, gmix_ref[...], sh1, sc1)
    h_halo = _rms_mod(xh_ref[0], gmix_ref[...], sh1, sc1)
    h_halo = jnp.where(i > 0, h_halo, 0.0)
    ext = jnp.concatenate([h_halo, h], axis=0)
    pos = i * tm + lax.broadcasted_iota(I32, (tm, 1), 0)
    parts = []
    for g, w in enumerate(POOL_WINDOWS):
        lo = g * POOL_GROUP_DIM
        s = ext[:, lo:lo + POOL_GROUP_DIM]
        step = 1
        while step < w:
            s = s + pltpu.roll(s, step, axis=0)
            step *= 2
        cnt = jnp.minimum(w, pos + 1).astype(F32)
        parts.append(s[HALO:, :] / cnt - h[:, lo:lo + POOL_GROUP_DIM])
    pooled = jnp.concatenate(parts, axis=1)
    x1 = x + g1_ref[0] * _pool_project(pooled, wp_ref, ps_ref)
    x2 = _mlp_resid(x1, gffn_ref[...], sh2_ref[0], sc2_ref[0], g2_ref[0], wup_ref, wdn_ref, acc_ref)
    y_ref[0] = _rmsnorm(x2, gfin_ref[...])

    @pl.when(i == pl.num_programs(1) - 1)
    def _():
        hlast_ref[0] = h[tm - HALO:, :]


def _layer1_prompt_call(x, mods, g_mix, g_ffn, g_fin, pool_w, pool_scale, w_up, w_dn, tm):
    nb, t, _ = x.shape
    tok = pl.BlockSpec((1, tm, D_MODEL), lambda b, i: (b, i, 0))
    halo = pl.BlockSpec((1, HALO, D_MODEL), lambda b, i: (b, jnp.maximum(i * (tm // HALO) - 1, 0), 0))
    vec = _const_spec((1, D_MODEL))
    return pl.pallas_call(
        _layer1_prompt_kernel,
        grid=(nb, t // tm),
        in_specs=[tok, halo] + [m.spec(tm) for m in mods] + [vec, vec, vec,
                  _const_spec(pool_w.shape), vec, _const_spec((D_MODEL, D_FF)), _const_spec((D_FF, D_MODEL))],
        out_specs=[tok, pl.BlockSpec((1, HALO, D_MODEL), lambda b, i: (b, 0, 0))],
        out_shape=[jax.ShapeDtypeStruct((nb, t, D_MODEL), F32), jax.ShapeDtypeStruct((nb, HALO, D_MODEL), F32)],
        scratch_shapes=[pltpu.VMEM((tm, D_MODEL), F32)],
        compiler_params=_params(("arbitrary", "arbitrary")),
        name="layer1_prompt",
    )(x, x, *[m.arr for m in mods], g_mix.reshape(1, D_MODEL), g_ffn.reshape(1, D_MODEL),
      g_fin.reshape(1, D_MODEL), pool_w, pool_scale.reshape(1, D_MODEL), w_up, w_dn)


def _layer1_sample_kernel(x_ref, prev_ref, sh1_ref, sc1_ref, g1_ref, sh2_ref, sc2_ref, g2_ref,
                          gmix_ref, gffn_ref, gfin_ref, wp_ref, ps_ref, wup_ref, wdn_ref,
                          y_ref, h_ref, hs_ref, pooled_ref, acc_ref):
    tm = x_ref.shape[1]
    t_new = tm // prev_ref.shape[1]
    nseq = prev_ref.shape[1]
    x = x_ref[0]
    h = _rms_mod(x, gmix_ref[...], sh1_ref[0], sc1_ref[0])
    h_ref[0] = h
    ncol = D_MODEL // LANES
    for c in range(ncol):
        hs_ref[c] = h[:, c * LANES:(c + 1) * LANES]
    ext = [prev_ref[j] for j in range(POOL_STATE)]
    ext += [jnp.concatenate([hs_ref[c, pl.ds(t, nseq, stride=t_new), :] for c in range(ncol)], axis=1)
            for t in range(t_new)]
    for t in range(t_new):
        parts = []
        for g, w in enumerate(POOL_WINDOWS):
            lo = g * POOL_GROUP_DIM
            s = ext[POOL_STATE + t][:, lo:lo + POOL_GROUP_DIM]
            for j in range(1, w):
                s = s + ext[POOL_STATE + t - j][:, lo:lo + POOL_GROUP_DIM]
            parts.append(s / float(w) - ext[POOL_STATE + t][:, lo:lo + POOL_GROUP_DIM])
        pooled_t = jnp.concatenate(parts, axis=1)
        for c in range(ncol):
            pooled_ref[c, pl.ds(t, nseq, stride=t_new), :] = pooled_t[:, c * LANES:(c + 1) * LANES]
    pooled = jnp.concatenate([pooled_ref[c] for c in range(ncol)], axis=1)
    x1 = x + g1_ref[0] * _pool_project(pooled, wp_ref, ps_ref)
    x2 = _mlp_resid(x1, gffn_ref[...], sh2_ref[0], sc2_ref[0], g2_ref[0], wup_ref, wdn_ref, acc_ref)
    y_ref[0] = _rmsnorm(x2, gfin_ref[...])


def _layer1_sample_call(x, prev_t, mods, g_mix, g_ffn, g_fin, pool_w, pool_scale, w_up, w_dn, tm, t_new):
    nb, t, _ = x.shape
    tok = pl.BlockSpec((1, tm, D_MODEL), lambda b, i: (b, i, 0))
    prev = pl.BlockSpec((POOL_STATE, tm // t_new, D_MODEL), lambda b, i: (0, i, 0))
    vec = _const_spec((1, D_MODEL))
    return pl.pallas_call(
        _layer1_sample_kernel,
        grid=(nb, t // tm),
        in_specs=[tok, prev] + [m.spec(tm) for m in mods] + [vec, vec, vec,
                  _const_spec(pool_w.shape), vec, _const_spec((D_MODEL, D_FF)), _const_spec((D_FF, D_MODEL))],
        out_specs=[tok, tok],
        out_shape=[jax.ShapeDtypeStruct((nb, t, D_MODEL), F32), jax.ShapeDtypeStruct((nb, t, D_MODEL), F32)],
        scratch_shapes=[pltpu.VMEM((D_MODEL // LANES, tm, LANES), F32),
                        pltpu.VMEM((D_MODEL // LANES, tm, LANES), F32),
                        pltpu.VMEM((tm, D_MODEL), F32)],
        compiler_params=_params(("arbitrary", "arbitrary")),
        name="layer1_sample",
    )(x, prev_t, *[m.arr for m in mods], g_mix.reshape(1, D_MODEL), g_ffn.reshape(1, D_MODEL),
      g_fin.reshape(1, D_MODEL), pool_w, pool_scale.reshape(1, D_MODEL), w_up, w_dn)


def kernel(x_prompt, x_sample, cache_k, cache_v, cache_kidx, state_pool, page_table, c_prompt, c_sample,
           norm_mix_g, norm_ffn_g, w_mod, b_mod, dsa_w_in, dsa_w_o, pool_w, pool_scale, w_up, w_down,
           final_norm_g):
    nbp, seq, _ = x_prompt.shape
    nbs, t_new, _ = x_sample.shape
    n_tok_s = nbs * t_new

    w_in_b = jnp.pad(dsa_w_in[0], ((0, 0), (0, D_PROJ_PAD - D_PROJ))).astype(BF16)
    w_o_b = dsa_w_o[0].astype(BF16)
    w_up_b = w_up.astype(BF16)
    w_dn_b = w_down.astype(BF16)
    pool_w_b = pool_w[0].astype(BF16)

    n_c = nbp + nbs
    c_rows = -(-n_c // SUBLANES) * SUBLANES
    c_all = jnp.pad(jnp.concatenate([c_prompt, c_sample], axis=0), ((0, c_rows - n_c), (0, 0)))
    mod, mod_tok = _mod_call(c_all, w_mod, b_mod, nbp, nbs, t_new)

    def mods_prompt(layer):
        m = mod[layer, :nbp].reshape(nbp, 1, 6, D_MODEL)
        return [_mod_per_sequence(m[:, :, j]) for j in range(6)]

    def mods_sample(layer):
        return [_mod_per_token(mod_tok, layer, j) for j in range(6)]

    sh1, sc1, g1, sh2, sc2, g2 = mods_prompt(0)
    k_p, v_p, ki_p, q, qi, kiw, kb, vb, kib = _proj_call(x_prompt, sh1, sc1, norm_mix_g[0], w_in_b, 512)
    o = _dsa_prompt_call(q, qi, kiw, kb, vb, kib)
    x1 = _post0_call(x_prompt, o, g1, sh2, sc2, g2, norm_ffn_g[0], w_o_b, w_up_b[0], w_dn_b[0], 512)
    y_prompt, hlast = _layer1_prompt_call(x1, mods_prompt(1), norm_mix_g[1], norm_ffn_g[1], final_norm_g,
                                          pool_w_b, pool_scale[0], w_up_b[1], w_dn_b[1], 512)
    pool_p = hlast[:, HALO - POOL_STATE:]

    sh1, sc1, g1, sh2, sc2, g2 = mods_sample(0)
    xs = x_sample.reshape(1, n_tok_s, D_MODEL)
    tms = 256
    k_s, v_s, ki_s, q, qi, kiw, _, _, _ = _proj_call(xs, sh1, sc1, norm_mix_g[0], w_in_b, tms)

    pad_q = lambda a: jnp.pad(a, [(0, 0)] * (a.ndim - 2) + [(0, Q8 - t_new), (0, 0)])
    qis = pad_q(qi.reshape(nbs, t_new, IDX_HEADS, IDX_DIM).transpose(0, 2, 1, 3)).reshape(nbs, IDX_HEADS * Q8, IDX_DIM)
    wi = kiw.reshape(nbs, t_new, KIW_W)[:, :, IDX_DIM:IDX_DIM + IDX_HEADS] * IDX_HEAD_SCALE
    ws = pad_q(wi.transpose(0, 2, 1)[..., None]).reshape(nbs, IDX_HEADS * Q8, 1)
    knew8 = pad_q(k_s.reshape(nbs, t_new, KV_W))
    vnew8 = pad_q(v_s.reshape(nbs, t_new, KV_W))
    kinew_t = jnp.pad(ki_s.reshape(nbs, t_new, IDX_DIM).transpose(0, 2, 1), ((0, 0), (0, 0), (0, PAGE_SIZE - t_new)))
    scores = _idx_sample_call(page_table, qis, ws, kinew_t, cache_kidx[0].transpose(0, 2, 1))
    bias = _select_sample_call(scores.reshape(nbs * Q8, L_SAMPLE)).reshape(nbs, Q8, L_SAMPLE)
    qg = pad_q(q.reshape(nbs, t_new, N_KV_HEADS, GROUP, HEAD_DIM).transpose(0, 2, 3, 1, 4))
    eye = jnp.eye(N_KV_HEADS, dtype=BF16)
    qblk = (qg[:, :, :, :, None, :] * eye[None, :, None, None, :, None]).reshape(nbs, N_KV_HEADS * GROUP * Q8, KV_W)
    n_pool = cache_k.shape[1]
    page_rows = lambda c: c[0].reshape(n_pool, PAGE_SIZE * N_KV_HEADS, HEAD_DIM)
    o_blk = _attn_sample_call(page_table, qblk, bias, knew8, vnew8, page_rows(cache_k), page_rows(cache_v))
    o_blk = o_blk.reshape(nbs, N_KV_HEADS, GROUP, Q8, N_KV_HEADS, HEAD_DIM)
    o_s = jnp.stack([o_blk[:, g, :, :t_new, g] for g in range(N_KV_HEADS)], axis=1)
    o_s = o_s.transpose(0, 3, 1, 2, 4).reshape(1, n_tok_s, Q_W).astype(BF16)
    x1s = _post0_call(xs, o_s, g1, sh2, sc2, g2, norm_ffn_g[0], w_o_b, w_up_b[0], w_dn_b[0], tms)
    prev_t = state_pool[0].transpose(1, 0, 2)
    y_s, h1s = _layer1_sample_call(x1s, prev_t, mods_sample(1), norm_mix_g[1], norm_ffn_g[1], final_norm_g,
                                   pool_w_b, pool_scale[0], w_up_b[1], w_dn_b[1], tms, t_new)
    y_sample = y_s.reshape(nbs, t_new, D_MODEL)
    pool_s = jnp.concatenate([state_pool[0][:, t_new:], h1s.reshape(nbs, t_new, D_MODEL)], axis=1)

    return (
        y_prompt, y_sample,
        k_p[None], v_p[None], ki_p[None], pool_p[None],
        k_s.reshape(1, nbs, t_new, N_KV_HEADS, HEAD_DIM), v_s.reshape(1, nbs, t_new, N_KV_HEADS, HEAD_DIM),
        ki_s.reshape(1, nbs, t_new, IDX_DIM), pool_s[None],
    )
```

```python
import functools
from typing import Callable, NamedTuple

import jax
import jax.numpy as jnp
from jax import lax
from jax.experimental import pallas as pl
from jax.experimental.pallas import tpu as pltpu

F32 = jnp.float32
BF16 = jnp.bfloat16
I32 = jnp.int32

D_MODEL = 1024
DEPTH = 2
PAST_LEN = 2048
PAGE_SIZE = 128
N_PAGES = PAST_LEN // PAGE_SIZE
N_HEADS = 8
HEAD_DIM = 128
N_KV_HEADS = 2
GROUP = N_HEADS // N_KV_HEADS
IDX_HEADS = 8
IDX_DIM = 64
TOPK = 256
POOL_WINDOWS = (2, 4, 8, 16)
POOL_GROUP_DIM = D_MODEL // len(POOL_WINDOWS)
POOL_STATE = max(POOL_WINDOWS) - 1
D_FF = 4 * D_MODEL
RMS_EPS = 1e-6
Q_W = N_HEADS * HEAD_DIM
KV_W = N_KV_HEADS * HEAD_DIM
QI_W = IDX_HEADS * IDX_DIM
D_PROJ = Q_W + 2 * KV_W + QI_W + IDX_DIM + IDX_HEADS

LANES = 128
SUBLANES = 8
VMEM_LIMIT_BYTES = 56 * 1024 * 1024

KIW_W = LANES
D_PROJ_PAD = Q_W + 2 * KV_W + QI_W + KIW_W
OFF_K = Q_W
OFF_V = OFF_K + KV_W
OFF_QI = OFF_V + KV_W
OFF_KIW = OFF_QI + QI_W

LOG2_E = 1.4426950408889634
ATT_SCALE = HEAD_DIM ** -0.5 * LOG2_E
IDX_SCALE = IDX_DIM ** -0.5
IDX_HEAD_SCALE = IDX_HEADS ** -0.5

INT_MIN = -(2 ** 31)
INT_MAX = 2 ** 31 - 1
F32_MAX = 3.4028234663852886e38
NEG_INF_KEY = INT_MIN + (1 << 23)
NEG_BIG = -1e30

TQ = 128
TK = 512
FF_CHUNK = 512
HALO = 16
SEL_ROWS = 128
L_SAMPLE = PAST_LEN + PAGE_SIZE
T_NEW = 4
SCORE_SEQS = 4
ATTN_SEQS = 2
Q8 = SUBLANES


def _params(sem):
    return pltpu.CompilerParams(dimension_semantics=sem, vmem_limit_bytes=VMEM_LIMIT_BYTES)


def _dot(a, b):
    return jnp.dot(a, b, preferred_element_type=F32)


def _dot_nt(a, b):
    return lax.dot_general(a, b, (((1,), (1,)), ((), ())), preferred_element_type=F32)


def _rms_mod(x, g, shift, scale):
    ms = jnp.mean(x * x, axis=-1, keepdims=True)
    y = x * lax.rsqrt(ms + RMS_EPS) * g
    return y * (1.0 + scale) + shift


def _rmsnorm(x, g):
    ms = jnp.mean(x * x, axis=-1, keepdims=True)
    return x * lax.rsqrt(ms + RMS_EPS) * g


def _split3(x):
    hi = x.astype(BF16)
    r1 = x - hi.astype(F32)
    mid = r1.astype(BF16)
    lo = (r1 - mid.astype(F32)).astype(BF16)
    return hi, mid, lo


def _mod_kernel(c_ref, w_ref, b_ref, o_ref, tok_ref, *, n_prompt, t_new):
    c = c_ref[...]
    a = c * (1.0 / (1.0 + jnp.exp(-c)))
    w = w_ref[0]
    a_hi = a.astype(BF16)
    a_lo = (a - a_hi.astype(F32)).astype(BF16)
    w_hi = w.astype(BF16)
    w_lo = (w - w_hi.astype(F32)).astype(BF16)
    mod = _dot(a_hi, w_hi) + _dot(a_lo, w_hi) + _dot(a_hi, w_lo) + b_ref[0]
    o_ref[0] = mod
    n_tok, rows = tok_ref.shape[1], mod.shape[0]
    r = lax.broadcasted_iota(I32, (n_tok, rows), 0)
    s = (lax.broadcasted_iota(I32, (n_tok, rows), 1) - n_prompt) * t_new
    pick = jnp.where((r >= s) & (r < s + t_new), 1.0, 0.0).astype(BF16)
    hi, mid, lo = _split3(mod)
    tok_ref[0] = (_dot(pick, hi) + _dot(pick, mid)) + _dot(pick, lo)


def _mod_call(c_all, w_mod, b_mod, n_prompt, n_sample, t_new):
    rows = c_all.shape[0]
    tn = 1536
    return pl.pallas_call(
        functools.partial(_mod_kernel, n_prompt=n_prompt, t_new=t_new),
        grid=(DEPTH, 6 * D_MODEL // tn),
        in_specs=[
            pl.BlockSpec((rows, D_MODEL), lambda i, j: (0, 0)),
            pl.BlockSpec((1, D_MODEL, tn), lambda i, j: (i, 0, j)),
            pl.BlockSpec((1, 1, tn), lambda i, j: (i, 0, j)),
        ],
        out_specs=[pl.BlockSpec((1, rows, tn), lambda i, j: (i, 0, j)),
                   pl.BlockSpec((1, n_sample * t_new, tn), lambda i, j: (i, 0, j))],
        out_shape=[jax.ShapeDtypeStruct((DEPTH, rows, 6 * D_MODEL), F32),
                   jax.ShapeDtypeStruct((DEPTH, n_sample * t_new, 6 * D_MODEL), F32)],
        compiler_params=_params(("arbitrary", "arbitrary")),
        name="mod_vectors",
    )(c_all, w_mod, b_mod.reshape(DEPTH, 1, 6 * D_MODEL))


def _proj_kernel(x_ref, sh_ref, sc_ref, g_ref, w_ref,
                 k_ref, v_ref, ki_ref, q_ref, qi_ref, kiw_ref, kb_ref, vb_ref, kib_ref):
    h = _rms_mod(x_ref[0], g_ref[...], sh_ref[0], sc_ref[0]).astype(BF16)
    q_ref[0] = (_dot(h, w_ref[:, 0:Q_W]) * ATT_SCALE).astype(BF16)
    k = _dot(h, w_ref[:, OFF_K:OFF_K + KV_W])
    v = _dot(h, w_ref[:, OFF_V:OFF_V + KV_W])
    for g in range(N_KV_HEADS):
        k_ref[0, :, g, :] = k[:, g * HEAD_DIM:(g + 1) * HEAD_DIM]
        v_ref[0, :, g, :] = v[:, g * HEAD_DIM:(g + 1) * HEAD_DIM]
    kb_ref[0] = k.astype(BF16)
    vb_ref[0] = v.astype(BF16)
    qi_ref[0] = (_dot(h, w_ref[:, OFF_QI:OFF_QI + QI_W]) * IDX_SCALE).astype(BF16)
    kiw = _dot(h, w_ref[:, OFF_KIW:OFF_KIW + KIW_W])
    kiw_ref[0] = kiw
    ki = kiw[:, 0:IDX_DIM]
    ki_ref[0] = ki
    kib_ref[0] = ki.astype(BF16)


class _Mod(NamedTuple):
    arr: jax.Array
    spec: Callable[[int], pl.BlockSpec]


def _mod_per_sequence(arr):
    return _Mod(arr, lambda tm: pl.BlockSpec((1, 1, D_MODEL), lambda b, i: (b, 0, 0)))


def _mod_per_token(mod_tok, layer, j):
    return _Mod(mod_tok, lambda tm: pl.BlockSpec((1, tm, D_MODEL), lambda b, i: (layer, i, j)))


def _proj_call(x, shift, scale, g, w_in_b, tm):
    nb, t, _ = x.shape
    tok = lambda w: pl.BlockSpec((1, tm, w), lambda b, i: (b, i, 0))
    shp = lambda w, dt: jax.ShapeDtypeStruct((nb, t, w), dt)
    heads = pl.BlockSpec((1, tm, N_KV_HEADS, HEAD_DIM), lambda b, i: (b, i, 0, 0))
    heads_shape = jax.ShapeDtypeStruct((nb, t, N_KV_HEADS, HEAD_DIM), F32)
    return pl.pallas_call(
        _proj_kernel,
        grid=(nb, t // tm),
        in_specs=[
            tok(D_MODEL), shift.spec(tm), scale.spec(tm),
            pl.BlockSpec((1, D_MODEL), lambda b, i: (0, 0)),
            pl.BlockSpec((D_MODEL, D_PROJ_PAD), lambda b, i: (0, 0)),
        ],
        out_specs=[heads, heads, tok(IDX_DIM), tok(Q_W), tok(QI_W), tok(KIW_W),
                   tok(KV_W), tok(KV_W), tok(IDX_DIM)],
        out_shape=[heads_shape, heads_shape, shp(IDX_DIM, F32), shp(Q_W, BF16), shp(QI_W, BF16),
                   shp(KIW_W, F32), shp(KV_W, BF16), shp(KV_W, BF16), shp(IDX_DIM, BF16)],
        compiler_params=_params(("arbitrary", "arbitrary")),
        name="dsa_proj",
    )(x, shift.arr, scale.arr, g.reshape(1, D_MODEL), w_in_b)


def _key_float(k):
    return lax.bitcast_convert_type(jnp.where(k >= 0, k, (k - 1) ^ INT_MAX), F32)


def _select_threshold(score_ref, ithr_ref, nblk, blk):
    nsub = blk // SUBLANES
    nacc = 4
    row8 = lax.broadcasted_iota(I32, (SUBLANES, LANES), 0)

    def count(pred):
        def sweep(c0, width, accs):
            accs = list(accs)
            rows = score_ref[pl.ds(c0, width), :]
            for j in range(width // SUBLANES):
                hit = pred(rows[j * SUBLANES:(j + 1) * SUBLANES], c0 + j * SUBLANES)
                accs[j % nacc] = accs[j % nacc] + jnp.where(hit, 1, 0)
            return tuple(accs)

        accs = lax.fori_loop(0, nblk // 2, lambda c, a: sweep(pl.multiple_of(c * (2 * blk), 2 * blk), 2 * blk, a),
                             tuple(jnp.zeros((SUBLANES, LANES), I32) for _ in range(nacc)))
        accs = lax.cond(nblk % 2 == 1, lambda a: sweep(pl.multiple_of((nblk - 1) * blk, blk), blk, a),
                        lambda a: a, accs)
        tot = (accs[0] + accs[1]) + (accs[2] + accs[3])
        return jnp.sum(tot, axis=0, keepdims=True)

    def bcast(v):
        return jnp.broadcast_to(v, (SUBLANES, LANES))

    def count_ge(probe):
        probe_b = bcast(probe)
        return count(lambda rows, row0: rows >= probe_b)

    n = count_ge(jnp.zeros((1, LANES), F32))
    ok = n >= TOPK
    thr_k = jnp.where(ok, 0, INT_MIN).astype(I32)
    n_ge = jnp.where(ok, n, INT_MAX).astype(I32)

    def bit_body(t, carry):
        thr_k, n_ge = carry
        cand_k = thr_k | lax.shift_left(jnp.int32(1), 30 - t)
        n = count_ge(_key_float(cand_k))
        ok = n >= TOPK
        return jnp.where(ok, cand_k, thr_k), jnp.where(ok, n, n_ge)

    thr_k, n_ge = lax.fori_loop(0, 31, bit_body, (thr_k, n_ge))
    found = thr_k > NEG_INF_KEY
    thr = jnp.where(found, _key_float(thr_k), -F32_MAX)
    thr_b = bcast(thr)
    tie = found & (n_ge > TOPK)
    ithr_ref[...] = jnp.full((SUBLANES, LANES), INT_MAX, I32)

    @pl.when(jnp.max(tie.astype(I32)) > 0)
    def _():
        n_gt = count(lambda rows, row0: rows > thr_b)
        need = TOPK - n_gt

        def idx_body(t, ithr):
            cand = ithr | lax.shift_left(jnp.int32(1), 11 - t)
            cand_b = bcast(cand)
            n = count(lambda rows, row0: (rows == thr_b) & ((row0 + row8) < cand_b))
            return jnp.where(n < need, cand, ithr)

        ithr = lax.fori_loop(0, 12, idx_body, jnp.zeros((1, LANES), I32))
        ithr_ref[...] = jnp.where(bcast(tie), bcast(ithr), ithr_ref[...])

    return thr


def _bias_block(score_ref, row0, thr_q, ithr_q):
    sc = score_ref[pl.ds(row0, LANES), :]
    key_index = row0 + lax.broadcasted_iota(I32, (LANES, LANES), 0)
    sel = (sc > thr_q) | ((sc == thr_q) & (key_index <= ithr_q))
    return jnp.where(sel, 0.0, NEG_BIG).astype(F32).T


def _dsa_prompt_kernel(q_ref, qi_ref, kiw_ref, kb_ref, vb_ref, kib_ref, o_ref,
                       score_ref, ithr_ref, qir_ref, s_ref, mlane_ref, llane_ref, acc_ref):
    i = pl.program_id(1)
    nchunk = (i * TQ + TQ + TK - 1) // TK
    qpos = i * TQ + lax.broadcasted_iota(I32, (1, LANES), 1)

    def for_chunks(body):
        def quad(c, carry):
            body(pl.multiple_of(c * (4 * TK), 4 * TK), 4 * TK)
            return carry
        nquad = nchunk // 4
        lax.fori_loop(0, nquad, quad, 0)

        @pl.when(nchunk % 4 >= 2)
        def _():
            body(pl.multiple_of(nquad * (4 * TK), 2 * TK), 2 * TK)

        @pl.when(nchunk % 2 == 1)
        def _():
            body(pl.multiple_of((nchunk - 1) * TK, TK), TK)

    for h in range(IDX_HEADS):
        qir_ref[h * TQ:(h + 1) * TQ, :] = qi_ref[0, :, h * IDX_DIM:(h + 1) * IDX_DIM]
    kiw_t = kiw_ref[0].T
    w_rows = [kiw_t[IDX_DIM + h:IDX_DIM + h + 1, :] * IDX_HEAD_SCALE for h in range(IDX_HEADS)]

    def idx_body(c0, width):
        logit = _dot_nt(kib_ref[0, pl.ds(c0, width), :], qir_ref[...])
        score = jnp.zeros((width, LANES), F32)
        for h in range(IDX_HEADS):
            score = score + jnp.maximum(logit[:, h * TQ:(h + 1) * TQ], 0.0) * w_rows[h]
        krow = c0 + lax.broadcasted_iota(I32, (width, LANES), 0)
        score_ref[pl.ds(c0, width), :] = jnp.where(krow <= qpos, score, -jnp.inf)

    for_chunks(idx_body)
    thr = _select_threshold(score_ref, ithr_ref, nchunk, TK)
    thr_q = jnp.broadcast_to(thr, (LANES, LANES))
    ithr_q = jnp.broadcast_to(ithr_ref[0:1, :], (LANES, LANES))

    def q_group(g):
        return jnp.concatenate(
            [q_ref[0, :, (g * GROUP + r) * HEAD_DIM:(g * GROUP + r + 1) * HEAD_DIM] for r in range(GROUP)], axis=0)

    mlane_ref[...] = jnp.full(mlane_ref.shape, NEG_BIG, F32)

    def logits_body(c0, width):
        bias = jnp.concatenate(
            [_bias_block(score_ref, pl.multiple_of(c0 + j * LANES, LANES), thr_q, ithr_q)
             for j in range(width // LANES)], axis=1)
        bias4 = jnp.concatenate([bias] * GROUP, axis=0)
        for g in range(N_KV_HEADS):
            s = _dot_nt(q_group(g), kb_ref[0, pl.ds(c0, width), g * HEAD_DIM:(g + 1) * HEAD_DIM]) + bias4
            s_ref[g, :, pl.ds(c0, width)] = s
            m = mlane_ref[g]
            for j in range(width // LANES):
                m = jnp.maximum(m, s[:, j * LANES:(j + 1) * LANES])
            mlane_ref[g] = m

    for_chunks(logits_body)

    for g in range(N_KV_HEADS):
        mlane_ref[g] = jnp.broadcast_to(jnp.max(mlane_ref[g], axis=1, keepdims=True), mlane_ref.shape[1:])
    llane_ref[...] = jnp.zeros(llane_ref.shape, F32)
    acc_ref[...] = jnp.zeros(acc_ref.shape, F32)

    def pv_body(c0, width):
        for g in range(N_KV_HEADS):
            m = mlane_ref[g]
            l = llane_ref[g]
            ps = []
            for j in range(width // LANES):
                pj = jnp.exp2(s_ref[g, :, pl.ds(pl.multiple_of(c0 + j * LANES, LANES), LANES)] - m)
                l = l + pj
                ps.append(pj.astype(BF16))
            llane_ref[g] = l
            p = jnp.concatenate(ps, axis=1)
            acc_ref[g] += _dot(p, vb_ref[0, pl.ds(c0, width), g * HEAD_DIM:(g + 1) * HEAD_DIM])

    for_chunks(pv_body)

    for g in range(N_KV_HEADS):
        og = acc_ref[g] * (1.0 / jnp.sum(llane_ref[g], axis=1, keepdims=True))
        for r in range(GROUP):
            col = (g * GROUP + r) * HEAD_DIM
            o_ref[0, :, col:col + HEAD_DIM] = og[r * TQ:(r + 1) * TQ].astype(BF16)


def _dsa_prompt_call(q, qi, kiw, kb, vb, kib):
    nb, s, _ = q.shape
    tile = lambda w: pl.BlockSpec((1, TQ, w), lambda b, i: (b, i, 0))
    full = lambda w: pl.BlockSpec((1, s, w), lambda b, i: (b, 0, 0))
    rows = GROUP * TQ
    return pl.pallas_call(
        _dsa_prompt_kernel,
        grid=(nb, s // TQ),
        in_specs=[tile(Q_W), tile(QI_W), tile(KIW_W), full(KV_W), full(KV_W), full(IDX_DIM)],
        out_specs=tile(Q_W),
        out_shape=jax.ShapeDtypeStruct((nb, s, Q_W), BF16),
        scratch_shapes=[
            pltpu.VMEM((s, TQ), F32),
            pltpu.VMEM((SUBLANES, LANES), I32),
            pltpu.VMEM((IDX_HEADS * TQ, IDX_DIM), BF16),
            pltpu.VMEM((N_KV_HEADS, rows, s), F32),
            pltpu.VMEM((N_KV_HEADS, rows, LANES), F32),
            pltpu.VMEM((N_KV_HEADS, rows, LANES), F32),
            pltpu.VMEM((N_KV_HEADS, rows, HEAD_DIM), F32),
        ],
        compiler_params=_params(("arbitrary", "arbitrary")),
        name="dsa_prompt",
    )(q, qi, kiw, kb, vb, kib)


def _page_specs(page_shape, nseq):
    zeros = (0,) * len(page_shape)
    return [pl.BlockSpec((1,) + page_shape,
                         functools.partial(lambda i, pt, u, p: (pt[i * nseq + u, p],) + zeros, u=u, p=p))
            for u in range(nseq) for p in range(N_PAGES)]


def _kv_page(page_ref):
    heads = [page_ref[0, pl.ds(g, PAGE_SIZE, stride=N_KV_HEADS), :] for g in range(N_KV_HEADS)]
    return jnp.concatenate(heads, axis=1).astype(BF16)


def _pad_new_page(new8):
    return jnp.concatenate([new8, jnp.zeros((PAGE_SIZE - Q8, new8.shape[1]), F32)], axis=0).astype(BF16)


def _idx_sample_kernel(pt_ref, qis_ref, ws_ref, kin_ref, *rest):
    nseq = qis_ref.shape[0]
    pages = rest[:nseq * N_PAGES]
    o_ref = rest[nseq * N_PAGES]
    del pt_ref
    for u in range(nseq):
        qs = qis_ref[u]
        w = ws_ref[u]

        def page_score(kt):
            r = jnp.maximum(_dot(qs, kt.astype(BF16)), 0.0) * w
            s = r[0:Q8]
            for h in range(1, IDX_HEADS):
                s = s + r[h * Q8:(h + 1) * Q8]
            return s

        for p in range(N_PAGES):
            o_ref[u, :, p * PAGE_SIZE:(p + 1) * PAGE_SIZE] = page_score(pages[u * N_PAGES + p][0])
        o_ref[u, :, PAST_LEN:L_SAMPLE] = page_score(kin_ref[u])


def _idx_sample_call(page_table, qis, ws, kinew_t, cache_ki_t):
    n = qis.shape[0]
    rows = IDX_HEADS * Q8
    nseq = SCORE_SEQS
    grid_spec = pltpu.PrefetchScalarGridSpec(
        num_scalar_prefetch=1,
        grid=(n // nseq,),
        in_specs=[
            pl.BlockSpec((nseq, rows, IDX_DIM), lambda i, pt: (i, 0, 0)),
            pl.BlockSpec((nseq, rows, 1), lambda i, pt: (i, 0, 0)),
            pl.BlockSpec((nseq, IDX_DIM, PAGE_SIZE), lambda i, pt: (i, 0, 0)),
        ] + _page_specs((IDX_DIM, PAGE_SIZE), nseq),
        out_specs=pl.BlockSpec((nseq, Q8, L_SAMPLE), lambda i, pt: (i, 0, 0)),
    )
    return pl.pallas_call(
        _idx_sample_kernel,
        grid_spec=grid_spec,
        out_shape=jax.ShapeDtypeStruct((n, Q8, L_SAMPLE), F32),
        compiler_params=_params(("arbitrary",)),
        name="dsa_sample_scores",
    )(page_table, qis, ws, kinew_t, *([cache_ki_t] * (nseq * N_PAGES)))


def _select_sample_kernel(s_ref, bias_ref, score_ref, ithr_ref):
    krow = lax.broadcasted_iota(I32, (LANES, SEL_ROWS), 0)
    qrow = lax.broadcasted_iota(I32, (LANES, SEL_ROWS), 1) % Q8
    real = qrow < T_NEW
    for j in range(L_SAMPLE // LANES):
        kidx = j * LANES + krow
        valid = ((kidx < PAST_LEN) | ((kidx - PAST_LEN) <= qrow)) & real
        score_ref[j * LANES:(j + 1) * LANES, :] = jnp.where(valid, s_ref[:, j * LANES:(j + 1) * LANES].T, -jnp.inf)
    thr = _select_threshold(score_ref, ithr_ref, L_SAMPLE // LANES, LANES)
    thr_q = jnp.broadcast_to(thr, (LANES, SEL_ROWS))
    ithr_q = jnp.broadcast_to(ithr_ref[0:1, :], (LANES, SEL_ROWS))
    for j in range(L_SAMPLE // LANES):
        bias_ref[:, j * LANES:(j + 1) * LANES] = _bias_block(score_ref, j * LANES, thr_q, ithr_q)


def _select_sample_call(scores):
    rows = scores.shape[0]
    return pl.pallas_call(
        _select_sample_kernel,
        grid=(rows // SEL_ROWS,),
        in_specs=[pl.BlockSpec((SEL_ROWS, L_SAMPLE), lambda i: (i, 0))],
        out_specs=pl.BlockSpec((SEL_ROWS, L_SAMPLE), lambda i: (i, 0)),
        out_shape=jax.ShapeDtypeStruct((rows, L_SAMPLE), F32),
        scratch_shapes=[pltpu.VMEM((L_SAMPLE, SEL_ROWS), F32), pltpu.VMEM((SUBLANES, LANES), I32)],
        compiler_params=_params(("arbitrary",)),
        name="dsa_sample_select",
    )(scores)


def _attn_sample_kernel(pt_ref, qb_ref, bias_ref, kn_ref, vn_ref, *rest):
    nseq = qb_ref.shape[0]
    kpages = rest[:nseq * N_PAGES]
    vpages = rest[nseq * N_PAGES:2 * nseq * N_PAGES]
    o_ref = rest[2 * nseq * N_PAGES]
    s_ref = rest[2 * nseq * N_PAGES + 1]
    del pt_ref
    for u in range(nseq):
        qb = qb_ref[u]
        reps = qb.shape[0] // Q8

        def page_logits(p, kp):
            bias8 = bias_ref[u, :, p * PAGE_SIZE:(p + 1) * PAGE_SIZE]
            s_ref[u, :, p * PAGE_SIZE:(p + 1) * PAGE_SIZE] = (
                _dot_nt(qb, kp) + jnp.concatenate([bias8] * reps, axis=0))

        for p in range(N_PAGES):
            page_logits(p, _kv_page(kpages[u * N_PAGES + p]))
        page_logits(N_PAGES, _pad_new_page(kn_ref[u]))

        s = s_ref[u]
        m = jnp.max(s, axis=1, keepdims=True)
        e = jnp.exp2(s - m)
        l = jnp.sum(e, axis=1, keepdims=True)
        eb = e.astype(BF16)
        acc = _dot(eb[:, PAST_LEN:L_SAMPLE], _pad_new_page(vn_ref[u]))
        for p in range(N_PAGES):
            acc = acc + _dot(eb[:, p * PAGE_SIZE:(p + 1) * PAGE_SIZE], _kv_page(vpages[u * N_PAGES + p]))
        o_ref[u] = acc * (1.0 / l)


def _attn_sample_call(page_table, qblk, bias, knew8, vnew8, cache_k, cache_v):
    n, rows, _ = qblk.shape
    nseq = ATTN_SEQS
    page = (PAGE_SIZE * N_KV_HEADS, HEAD_DIM)
    grid_spec = pltpu.PrefetchScalarGridSpec(
        num_scalar_prefetch=1,
        grid=(n // nseq,),
        in_specs=[
            pl.BlockSpec((nseq, rows, KV_W), lambda i, pt: (i, 0, 0)),
            pl.BlockSpec((nseq, Q8, L_SAMPLE), lambda i, pt: (i, 0, 0)),
            pl.BlockSpec((nseq, Q8, KV_W), lambda i, pt: (i, 0, 0)),
            pl.BlockSpec((nseq, Q8, KV_W), lambda i, pt: (i, 0, 0)),
        ] + _page_specs(page, nseq) + _page_specs(page, nseq),
        out_specs=pl.BlockSpec((nseq, rows, KV_W), lambda i, pt: (i, 0, 0)),
        scratch_shapes=[pltpu.VMEM((nseq, rows, L_SAMPLE), F32)],
    )
    return pl.pallas_call(
        _attn_sample_kernel,
        grid_spec=grid_spec,
        out_shape=jax.ShapeDtypeStruct((n, rows, KV_W), F32),
        compiler_params=_params(("arbitrary",)),
        name="dsa_sample_attn",
    )(page_table, qblk, bias, knew8, vnew8, *([cache_k] * (nseq * N_PAGES)), *([cache_v] * (nseq * N_PAGES)))


def _mlp_resid(x1, g_ffn, shift, scale, gate, wup_ref, wdn_ref, acc_ref):
    h = _rms_mod(x1, g_ffn, shift, scale).astype(BF16)
    for c in range(D_FF // FF_CHUNK):
        u = _dot(h, wup_ref[:, c * FF_CHUNK:(c + 1) * FF_CHUNK])
        u = jnp.square(jnp.maximum(u, 0.0)).astype(BF16)
        d = _dot(u, wdn_ref[c * FF_CHUNK:(c + 1) * FF_CHUNK, :])
        if c == 0:
            acc_ref[...] = d
        else:
            acc_ref[...] += d
    return x1 + gate * acc_ref[...]


def _post0_kernel(x_ref, o_ref, g1_ref, sh2_ref, sc2_ref, g2_ref, gffn_ref, wo_ref, wup_ref, wdn_ref,
                  out_ref, acc_ref):
    x1 = x_ref[0] + g1_ref[0] * _dot(o_ref[0], wo_ref[...])
    out_ref[0] = _mlp_resid(x1, gffn_ref[...], sh2_ref[0], sc2_ref[0], g2_ref[0], wup_ref, wdn_ref, acc_ref)


def _const_spec(shape):
    nd = len(shape)
    return pl.BlockSpec(shape, lambda b, i: (0,) * nd)


def _post0_call(x, o, g1, sh2, sc2, g2, g_ffn, w_o, w_up, w_dn, tm):
    nb, t, _ = x.shape
    tok = lambda w: pl.BlockSpec((1, tm, w), lambda b, i: (b, i, 0))
    return pl.pallas_call(
        _post0_kernel,
        grid=(nb, t // tm),
        in_specs=[tok(D_MODEL), tok(Q_W), g1.spec(tm), sh2.spec(tm), sc2.spec(tm),
                  g2.spec(tm), _const_spec((1, D_MODEL)), _const_spec((Q_W, D_MODEL)),
                  _const_spec((D_MODEL, D_FF)), _const_spec((D_FF, D_MODEL))],
        out_specs=tok(D_MODEL),
        out_shape=jax.ShapeDtypeStruct((nb, t, D_MODEL), F32),
        scratch_shapes=[pltpu.VMEM((tm, D_MODEL), F32)],
        compiler_params=_params(("arbitrary", "arbitrary")),
        name="layer0_out",
    )(x, o, g1.arr, sh2.arr, sc2.arr, g2.arr, g_ffn.reshape(1, D_MODEL), w_o, w_up, w_dn)


def _pool_project(pooled, wp_ref, ps_ref):
    parts = []
    for g in range(len(POOL_WINDOWS)):
        lo = g * POOL_GROUP_DIM
        parts.append(_dot(pooled[:, lo:lo + POOL_GROUP_DIM].astype(BF16), wp_ref[g]))
    return jnp.concatenate(parts, axis=1) * ps_ref[...]


def _layer1_prompt_kernel(x_ref, xh_ref, sh1_ref, sc1_ref, g1_ref, sh2_ref, sc2_ref, g2_ref,
                          gmix_ref, gffn_ref, gfin_ref, wp_ref, ps_ref, wup_ref, wdn_ref,
                          y_ref, hlast_ref, acc_ref):
    i = pl.program_id(1)
    tm = x_ref.shape[1]
    x = x_ref[0]
    sh1, sc1 = sh1_ref[0], sc1_ref[0]
    h = _rms_mod(x, gmix_ref[...], sh1, sc1)
    h_halo = _rms_mod(xh_ref[0], gmix_ref[...], sh1, sc1)
    h_halo = jnp.where(i > 0, h_halo, 0.0)
    ext = jnp.concatenate([h_halo, h], axis=0)
    pos = i * tm + lax.broadcasted_iota(I32, (tm, 1), 0)
    parts = []
    for g, w in enumerate(POOL_WINDOWS):
        lo = g * POOL_GROUP_DIM
        s = ext[:, lo:lo + POOL_GROUP_DIM]
        step = 1
        while step < w:
            s = s + pltpu.roll(s, step, axis=0)
            step *= 2
        cnt = jnp.minimum(w, pos + 1).astype(F32)
        parts.append(s[HALO:, :] / cnt - h[:, lo:lo + POOL_GROUP_DIM])
    pooled = jnp.concatenate(parts, axis=1)
    x1 = x + g1_ref[0] * _pool_project(pooled, wp_ref, ps_ref)
    x2 = _mlp_resid(x1, gffn_ref[...], sh2_ref[0], sc2_ref[0], g2_ref[0], wup_ref, wdn_ref, acc_ref)
    y_ref[0] = _rmsnorm(x2, gfin_ref[...])

    @pl.when(i == pl.num_programs(1) - 1)
    def _():
        hlast_ref[0] = h[tm - HALO:, :]


def _layer1_prompt_call(x, mods, g_mix, g_ffn, g_fin, pool_w, pool_scale, w_up, w_dn, tm):
    nb, t, _ = x.shape
    tok = pl.BlockSpec((1, tm, D_MODEL), lambda b, i: (b, i, 0))
    halo = pl.BlockSpec((1, HALO, D_MODEL), lambda b, i: (b, jnp.maximum(i * (tm // HALO) - 1, 0), 0))
    vec = _const_spec((1, D_MODEL))
    return pl.pallas_call(
        _layer1_prompt_kernel,
        grid=(nb, t // tm),
        in_specs=[tok, halo] + [m.spec(tm) for m in mods] + [vec, vec, vec,
                  _const_spec(pool_w.shape), vec, _const_spec((D_MODEL, D_FF)), _const_spec((D_FF, D_MODEL))],
        out_specs=[tok, pl.BlockSpec((1, HALO, D_MODEL), lambda b, i: (b, 0, 0))],
        out_shape=[jax.ShapeDtypeStruct((nb, t, D_MODEL), F32), jax.ShapeDtypeStruct((nb, HALO, D_MODEL), F32)],
        scratch_shapes=[pltpu.VMEM((tm, D_MODEL), F32)],
        compiler_params=_params(("arbitrary", "arbitrary")),
        name="layer1_prompt",
    )(x, x, *[m.arr for m in mods], g_mix.reshape(1, D_MODEL), g_ffn.reshape(1, D_MODEL),
      g_fin.reshape(1, D_MODEL), pool_w, pool_scale.reshape(1, D_MODEL), w_up, w_dn)


def _layer1_sample_kernel(x_ref, prev_ref, sh1_ref, sc1_ref, g1_ref, sh2_ref, sc2_ref, g2_ref,
                          gmix_ref, gffn_ref, gfin_ref, wp_ref, ps_ref, wup_ref, wdn_ref,
                          y_ref, h_ref, hs_ref, pooled_ref, acc_ref):
    tm = x_ref.shape[1]
    t_new = tm // prev_ref.shape[1]
    nseq = prev_ref.shape[1]
    x = x_ref[0]
    h = _rms_mod(x, gmix_ref[...], sh1_ref[0], sc1_ref[0])
    h_ref[0] = h
    ncol = D_MODEL // LANES
    for c in range(ncol):
        hs_ref[c] = h[:, c * LANES:(c + 1) * LANES]
    ext = [prev_ref[j] for j in range(POOL_STATE)]
    ext += [jnp.concatenate([hs_ref[c, pl.ds(t, nseq, stride=t_new), :] for c in range(ncol)], axis=1)
            for t in range(t_new)]
    for t in range(t_new):
        parts = []
        for g, w in enumerate(POOL_WINDOWS):
            lo = g * POOL_GROUP_DIM
            s = ext[POOL_STATE + t][:, lo:lo + POOL_GROUP_DIM]
            for j in range(1, w):
                s = s + ext[POOL_STATE + t - j][:, lo:lo + POOL_GROUP_DIM]
            parts.append(s / float(w) - ext[POOL_STATE + t][:, lo:lo + POOL_GROUP_DIM])
        pooled_t = jnp.concatenate(parts, axis=1)
        for c in range(ncol):
            pooled_ref[c, pl.ds(t, nseq, stride=t_new), :] = pooled_t[:, c * LANES:(c + 1) * LANES]
    pooled = jnp.concatenate([pooled_ref[c] for c in range(ncol)], axis=1)
    x1 = x + g1_ref[0] * _pool_project(pooled, wp_ref, ps_ref)
    x2 = _mlp_resid(x1, gffn_ref[...], sh2_ref[0], sc2_ref[0], g2_ref[0], wup_ref, wdn_ref, acc_ref)
    y_ref[0] = _rmsnorm(x2, gfin_ref[...])


def _layer1_sample_call(x, prev_t, mods, g_mix, g_ffn, g_fin, pool_w, pool_scale, w_up, w_dn, tm, t_new):
    nb, t, _ = x.shape
    tok = pl.BlockSpec((1, tm, D_MODEL), lambda b, i: (b, i, 0))
    prev = pl.BlockSpec((POOL_STATE, tm // t_new, D_MODEL), lambda b, i: (0, i, 0))
    vec = _const_spec((1, D_MODEL))
    return pl.pallas_call(
        _layer1_sample_kernel,
        grid=(nb, t // tm),
        in_specs=[tok, prev] + [m.spec(tm) for m in mods] + [vec, vec, vec,
                  _const_spec(pool_w.shape), vec, _const_spec((D_MODEL, D_FF)), _const_spec((D_FF, D_MODEL))],
        out_specs=[tok, tok],
        out_shape=[jax.ShapeDtypeStruct((nb, t, D_MODEL), F32), jax.ShapeDtypeStruct((nb, t, D_MODEL), F32)],
        scratch_shapes=[pltpu.VMEM((D_MODEL // LANES, tm, LANES), F32),
                        pltpu.VMEM((D_MODEL // LANES, tm, LANES), F32),
                        pltpu.VMEM((tm, D_MODEL), F32)],
        compiler_params=_params(("arbitrary", "arbitrary")),
        name="layer1_sample",
    )(x, prev_t, *[m.arr for m in mods], g_mix.reshape(1, D_MODEL), g_ffn.reshape(1, D_MODEL),
      g_fin.reshape(1, D_MODEL), pool_w, pool_scale.reshape(1, D_MODEL), w_up, w_dn)


def kernel(x_prompt, x_sample, cache_k, cache_v, cache_kidx, state_pool, page_table, c_prompt, c_sample,
           norm_mix_g, norm_ffn_g, w_mod, b_mod, dsa_w_in, dsa_w_o, pool_w, pool_scale, w_up, w_down,
           final_norm_g):
    nbp, seq, _ = x_prompt.shape
    nbs, t_new, _ = x_sample.shape
    n_tok_s = nbs * t_new

    w_in_b = jnp.pad(dsa_w_in[0], ((0, 0), (0, D_PROJ_PAD - D_PROJ))).astype(BF16)
    w_o_b = dsa_w_o[0].astype(BF16)
    w_up_b = w_up.astype(BF16)
    w_dn_b = w_down.astype(BF16)
    pool_w_b = pool_w[0].astype(BF16)

    n_c = nbp + nbs
    c_rows = -(-n_c // SUBLANES) * SUBLANES
    c_all = jnp.pad(jnp.concatenate([c_prompt, c_sample], axis=0), ((0, c_rows - n_c), (0, 0)))
    mod, mod_tok = _mod_call(c_all, w_mod, b_mod, nbp, nbs, t_new)

    def mods_prompt(layer):
        m = mod[layer, :nbp].reshape(nbp, 1, 6, D_MODEL)
        return [_mod_per_sequence(m[:, :, j]) for j in range(6)]

    def mods_sample(layer):
        return [_mod_per_token(mod_tok, layer, j) for j in range(6)]

    sh1, sc1, g1, sh2, sc2, g2 = mods_prompt(0)
    k_p, v_p, ki_p, q, qi, kiw, kb, vb, kib = _proj_call(x_prompt, sh1, sc1, norm_mix_g[0], w_in_b, 512)
    o = _dsa_prompt_call(q, qi, kiw, kb, vb, kib)
    x1 = _post0_call(x_prompt, o, g1, sh2, sc2, g2, norm_ffn_g[0], w_o_b, w_up_b[0], w_dn_b[0], 512)
    y_prompt, hlast = _layer1_prompt_call(x1, mods_prompt(1), norm_mix_g[1], norm_ffn_g[1], final_norm_g,
                                          pool_w_b, pool_scale[0], w_up_b[1], w_dn_b[1], 512)
    pool_p = hlast[:, HALO - POOL_STATE:]

    sh1, sc1, g1, sh2, sc2, g2 = mods_sample(0)
    xs = x_sample.reshape(1, n_tok_s, D_MODEL)
    tms = 256
    k_s, v_s, ki_s, q, qi, kiw, _, _, _ = _proj_call(xs, sh1, sc1, norm_mix_g[0], w_in_b, tms)

    pad_q = lambda a: jnp.pad(a, [(0, 0)] * (a.ndim - 2) + [(0, Q8 - t_new), (0, 0)])
    qis = pad_q(qi.reshape(nbs, t_new, IDX_HEADS, IDX_DIM).transpose(0, 2, 1, 3)).reshape(nbs, IDX_HEADS * Q8, IDX_DIM)
    wi = kiw.reshape(nbs, t_new, KIW_W)[:, :, IDX_DIM:IDX_DIM + IDX_HEADS] * IDX_HEAD_SCALE
    ws = pad_q(wi.transpose(0, 2, 1)[..., None]).reshape(nbs, IDX_HEADS * Q8, 1)
    knew8 = pad_q(k_s.reshape(nbs, t_new, KV_W))
    vnew8 = pad_q(v_s.reshape(nbs, t_new, KV_W))
    kinew_t = jnp.pad(ki_s.reshape(nbs, t_new, IDX_DIM).transpose(0, 2, 1), ((0, 0), (0, 0), (0, PAGE_SIZE - t_new)))
    scores = _idx_sample_call(page_table, qis, ws, kinew_t, cache_kidx[0].transpose(0, 2, 1))
    bias = _select_sample_call(scores.reshape(nbs * Q8, L_SAMPLE)).reshape(nbs, Q8, L_SAMPLE)
    qg = pad_q(q.reshape(nbs, t_new, N_KV_HEADS, GROUP, HEAD_DIM).transpose(0, 2, 3, 1, 4))
    eye = jnp.eye(N_KV_HEADS, dtype=BF16)
    qblk = (qg[:, :, :, :, None, :] * eye[None, :, None, None, :, None]).reshape(nbs, N_KV_HEADS * GROUP * Q8, KV_W)
    n_pool = cache_k.shape[1]
    page_rows = lambda c: c[0].reshape(n_pool, PAGE_SIZE * N_KV_HEADS, HEAD_DIM)
    o_blk = _attn_sample_call(page_table, qblk, bias, knew8, vnew8, page_rows(cache_k), page_rows(cache_v))
    o_blk = o_blk.reshape(nbs, N_KV_HEADS, GROUP, Q8, N_KV_HEADS, HEAD_DIM)
    o_s = jnp.stack([o_blk[:, g, :, :t_new, g] for g in range(N_KV_HEADS)], axis=1)
    o_s = o_s.transpose(0, 3, 1, 2, 4).reshape(1, n_tok_s, Q_W).astype(BF16)
    x1s = _post0_call(xs, o_s, g1, sh2, sc2, g2, norm_ffn_g[0], w_o_b, w_up_b[0], w_dn_b[0], tms)
    prev_t = state_pool[0].transpose(1, 0, 2)
    y_s, h1s = _layer1_sample_call(x1s, prev_t, mods_sample(1), norm_mix_g[1], norm_ffn_g[1], final_norm_g,
                                   pool_w_b, pool_scale[0], w_up_b[1], w_dn_b[1], tms, t_new)
    y_sample = y_s.reshape(nbs, t_new, D_MODEL)
    pool_s = jnp.concatenate([state_pool[0][:, t_new:], h1s.reshape(nbs, t_new, D_MODEL)], axis=1)

    return (
        y_prompt, y_sample,
        k_p[None], v_p[None], ki_p[None], pool_p[None],
        k_s.reshape(1, nbs, t_new, N_KV_HEADS, HEAD_DIM), v_s.reshape(1, nbs, t_new, N_KV_HEADS, HEAD_DIM),
        ki_s.reshape(1, nbs, t_new, IDX_DIM), pool_s[None],
    )
```

```python
import functools
from typing import Callable, NamedTuple

import jax
import jax.numpy as jnp
from jax import lax
from jax.experimental import pallas as pl
from jax.experimental.pallas import tpu as pltpu

F32 = jnp.float32
BF16 = jnp.bfloat16
I32 = jnp.int32

D_MODEL = 1024
DEPTH = 2
PAST_LEN = 2048
PAGE_SIZE = 128
N_PAGES = PAST_LEN // PAGE_SIZE
N_HEADS = 8
HEAD_DIM = 128
N_KV_HEADS = 2
GROUP = N_HEADS // N_KV_HEADS
IDX_HEADS = 8
IDX_DIM = 64
TOPK = 256
POOL_WINDOWS = (2, 4, 8, 16)
POOL_GROUP_DIM = D_MODEL // len(POOL_WINDOWS)
POOL_STATE = max(POOL_WINDOWS) - 1
D_FF = 4 * D_MODEL
RMS_EPS = 1e-6
Q_W = N_HEADS * HEAD_DIM
KV_W = N_KV_HEADS * HEAD_DIM
QI_W = IDX_HEADS * IDX_DIM
D_PROJ = Q_W + 2 * KV_W + QI_W + IDX_DIM + IDX_HEADS

LANES = 128
SUBLANES = 8
VMEM_LIMIT_BYTES = 56 * 1024 * 1024

KIW_W = LANES
D_PROJ_PAD = Q_W + 2 * KV_W + QI_W + KIW_W
OFF_K = Q_W
OFF_V = OFF_K + KV_W
OFF_QI = OFF_V + KV_W
OFF_KIW = OFF_QI + QI_W

LOG2_E = 1.4426950408889634
ATT_SCALE = HEAD_DIM ** -0.5 * LOG2_E
IDX_SCALE = IDX_DIM ** -0.5
IDX_HEAD_SCALE = IDX_HEADS ** -0.5

INT_MIN = -(2 ** 31)
INT_MAX = 2 ** 31 - 1
F32_MAX = 3.4028234663852886e38
NEG_INF_KEY = INT_MIN + (1 << 23)
NEG_BIG = -1e30

TQ = 128
TK = 512
FF_CHUNK = 512
HALO = 16
SEL_ROWS = 128
L_SAMPLE = PAST_LEN + PAGE_SIZE
T_NEW = 4
SCORE_SEQS = 8
ATTN_SEQS = 4
Q8 = SUBLANES


def _params(sem):
    return pltpu.CompilerParams(dimension_semantics=sem, vmem_limit_bytes=VMEM_LIMIT_BYTES)


def _dot(a, b):
    return jnp.dot(a, b, preferred_element_type=F32)


def _dot_nt(a, b):
    return lax.dot_general(a, b, (((1,), (1,)), ((), ())), preferred_element_type=F32)


def _rms_mod(x, g, shift, scale):
    ms = jnp.mean(x * x, axis=-1, keepdims=True)
    y = x * lax.rsqrt(ms + RMS_EPS) * g
    return y * (1.0 + scale) + shift


def _rmsnorm(x, g):
    ms = jnp.mean(x * x, axis=-1, keepdims=True)
    return x * lax.rsqrt(ms + RMS_EPS) * g


def _split3(x):
    hi = x.astype(BF16)
    r1 = x - hi.astype(F32)
    mid = r1.astype(BF16)
    lo = (r1 - mid.astype(F32)).astype(BF16)
    return hi, mid, lo


def _mod_kernel(c_ref, w_ref, b_ref, o_ref, tok_ref, *, n_prompt, t_new):
    c = c_ref[...]
    a = c * (1.0 / (1.0 + jnp.exp(-c)))
    w = w_ref[0]
    a_hi = a.astype(BF16)
    a_lo = (a - a_hi.astype(F32)).astype(BF16)
    w_hi = w.astype(BF16)
    w_lo = (w - w_hi.astype(F32)).astype(BF16)
    mod = _dot(a_hi, w_hi) + _dot(a_lo, w_hi) + _dot(a_hi, w_lo) + b_ref[0]
    o_ref[0] = mod
    n_tok, rows = tok_ref.shape[1], mod.shape[0]
    r = lax.broadcasted_iota(I32, (n_tok, rows), 0)
    s = (lax.broadcasted_iota(I32, (n_tok, rows), 1) - n_prompt) * t_new
    pick = jnp.where((r >= s) & (r < s + t_new), 1.0, 0.0).astype(BF16)
    hi, mid, lo = _split3(mod)
    tok_ref[0] = (_dot(pick, hi) + _dot(pick, mid)) + _dot(pick, lo)


def _mod_call(c_all, w_mod, b_mod, n_prompt, n_sample, t_new):
    rows = c_all.shape[0]
    tn = 1536
    return pl.pallas_call(
        functools.partial(_mod_kernel, n_prompt=n_prompt, t_new=t_new),
        grid=(DEPTH, 6 * D_MODEL // tn),
        in_specs=[
            pl.BlockSpec((rows, D_MODEL), lambda i, j: (0, 0)),
            pl.BlockSpec((1, D_MODEL, tn), lambda i, j: (i, 0, j)),
            pl.BlockSpec((1, 1, tn), lambda i, j: (i, 0, j)),
        ],
        out_specs=[pl.BlockSpec((1, rows, tn), lambda i, j: (i, 0, j)),
                   pl.BlockSpec((1, n_sample * t_new, tn), lambda i, j: (i, 0, j))],
        out_shape=[jax.ShapeDtypeStruct((DEPTH, rows, 6 * D_MODEL), F32),
                   jax.ShapeDtypeStruct((DEPTH, n_sample * t_new, 6 * D_MODEL), F32)],
        compiler_params=_params(("arbitrary", "arbitrary")),
        name="mod_vectors",
    )(c_all, w_mod, b_mod.reshape(DEPTH, 1, 6 * D_MODEL))


def _proj_kernel(x_ref, sh_ref, sc_ref, g_ref, w_ref,
                 k_ref, v_ref, ki_ref, q_ref, qi_ref, kiw_ref, kb_ref, vb_ref, kib_ref):
    h = _rms_mod(x_ref[0], g_ref[...], sh_ref[0], sc_ref[0]).astype(BF16)
    q_ref[0] = (_dot(h, w_ref[:, 0:Q_W]) * ATT_SCALE).astype(BF16)
    k = _dot(h, w_ref[:, OFF_K:OFF_K + KV_W])
    v = _dot(h, w_ref[:, OFF_V:OFF_V + KV_W])
    for g in range(N_KV_HEADS):
        k_ref[0, :, g, :] = k[:, g * HEAD_DIM:(g + 1) * HEAD_DIM]
        v_ref[0, :, g, :] = v[:, g * HEAD_DIM:(g + 1) * HEAD_DIM]
    kb_ref[0] = k.astype(BF16)
    vb_ref[0] = v.astype(BF16)
    qi_ref[0] = (_dot(h, w_ref[:, OFF_QI:OFF_QI + QI_W]) * IDX_SCALE).astype(BF16)
    kiw = _dot(h, w_ref[:, OFF_KIW:OFF_KIW + KIW_W])
    kiw_ref[0] = kiw
    ki = kiw[:, 0:IDX_DIM]
    ki_ref[0] = ki
    kib_ref[0] = ki.astype(BF16)


class _Mod(NamedTuple):
    arr: jax.Array
    spec: Callable[[int], pl.BlockSpec]


def _mod_per_sequence(arr):
    return _Mod(arr, lambda tm: pl.BlockSpec((1, 1, D_MODEL), lambda b, i: (b, 0, 0)))


def _mod_per_token(mod_tok, layer, j):
    return _Mod(mod_tok, lambda tm: pl.BlockSpec((1, tm, D_MODEL), lambda b, i: (layer, i, j)))


def _proj_call(x, shift, scale, g, w_in_b, tm):
    nb, t, _ = x.shape
    tok = lambda w: pl.BlockSpec((1, tm, w), lambda b, i: (b, i, 0))
    shp = lambda w, dt: jax.ShapeDtypeStruct((nb, t, w), dt)
    heads = pl.BlockSpec((1, tm, N_KV_HEADS, HEAD_DIM), lambda b, i: (b, i, 0, 0))
    heads_shape = jax.ShapeDtypeStruct((nb, t, N_KV_HEADS, HEAD_DIM), F32)
    return pl.pallas_call(
        _proj_kernel,
        grid=(nb, t // tm),
        in_specs=[
            tok(D_MODEL), shift.spec(tm), scale.spec(tm),
            pl.BlockSpec((1, D_MODEL), lambda b, i: (0, 0)),
            pl.BlockSpec((D_MODEL, D_PROJ_PAD), lambda b, i: (0, 0)),
        ],
        out_specs=[heads, heads, tok(IDX_DIM), tok(Q_W), tok(QI_W), tok(KIW_W),
                   tok(KV_W), tok(KV_W), tok(IDX_DIM)],
        out_shape=[heads_shape, heads_shape, shp(IDX_DIM, F32), shp(Q_W, BF16), shp(QI_W, BF16),
                   shp(KIW_W, F32), shp(KV_W, BF16), shp(KV_W, BF16), shp(IDX_DIM, BF16)],
        compiler_params=_params(("arbitrary", "arbitrary")),
        name="dsa_proj",
    )(x, shift.arr, scale.arr, g.reshape(1, D_MODEL), w_in_b)


def _key_float(k):
    return lax.bitcast_convert_type(jnp.where(k >= 0, k, (k - 1) ^ INT_MAX), F32)


def _select_threshold(score_ref, ithr_ref, nblk, blk):
    nsub = blk // SUBLANES
    nacc = 4
    row8 = lax.broadcasted_iota(I32, (SUBLANES, LANES), 0)

    def count(pred):
        def sweep(c0, width, accs):
            accs = list(accs)
            rows = score_ref[pl.ds(c0, width), :]
            for j in range(width // SUBLANES):
                hit = pred(rows[j * SUBLANES:(j + 1) * SUBLANES], c0 + j * SUBLANES)
                accs[j % nacc] = accs[j % nacc] + jnp.where(hit, 1, 0)
            return tuple(accs)

        nquad = nblk // 4
        accs = lax.fori_loop(0, nquad, lambda c, a: sweep(pl.multiple_of(c * (4 * blk), 4 * blk), 4 * blk, a),
                             tuple(jnp.zeros((SUBLANES, LANES), I32) for _ in range(nacc)))
        accs = lax.cond(nblk % 4 >= 2, lambda a: sweep(pl.multiple_of(nquad * (4 * blk), 2 * blk), 2 * blk, a),
                        lambda a: a, accs)
        accs = lax.cond(nblk % 2 == 1, lambda a: sweep(pl.multiple_of((nblk - 1) * blk, blk), blk, a),
                        lambda a: a, accs)
        tot = (accs[0] + accs[1]) + (accs[2] + accs[3])
        return jnp.sum(tot, axis=0, keepdims=True)

    def bcast(v):
        return jnp.broadcast_to(v, (SUBLANES, LANES))

    def count_ge(probe):
        probe_b = bcast(probe)
        return count(lambda rows, row0: rows >= probe_b)

    n = count_ge(jnp.zeros((1, LANES), F32))
    ok = n >= TOPK
    thr_k = jnp.where(ok, 0, INT_MIN).astype(I32)
    n_ge = jnp.where(ok, n, INT_MAX).astype(I32)

    def bit_body(t, carry):
        thr_k, n_ge = carry
        cand_k = thr_k | lax.shift_left(jnp.int32(1), 30 - t)
        n = count_ge(_key_float(cand_k))
        ok = n >= TOPK
        return jnp.where(ok, cand_k, thr_k), jnp.where(ok, n, n_ge)

    thr_k, n_ge = lax.fori_loop(0, 31, bit_body, (thr_k, n_ge))
    found = thr_k > NEG_INF_KEY
    thr = jnp.where(found, _key_float(thr_k), -F32_MAX)
    thr_b = bcast(thr)
    tie = found & (n_ge > TOPK)
    ithr_ref[...] = jnp.full((SUBLANES, LANES), INT_MAX, I32)

    @pl.when(jnp.max(tie.astype(I32)) > 0)
    def _():
        n_gt = count(lambda rows, row0: rows > thr_b)
        need = TOPK - n_gt

        def idx_body(t, ithr):
            cand = ithr | lax.shift_left(jnp.int32(1), 11 - t)
            cand_b = bcast(cand)
            n = count(lambda rows, row0: (rows == thr_b) & ((row0 + row8) < cand_b))
            return jnp.where(n < need, cand, ithr)

        ithr = lax.fori_loop(0, 12, idx_body, jnp.zeros((1, LANES), I32))
        ithr_ref[...] = jnp.where(bcast(tie), bcast(ithr), ithr_ref[...])

    return thr


def _bias_block(score_ref, row0, thr_q, ithr_q):
    sc = score_ref[pl.ds(row0, LANES), :]
    key_index = row0 + lax.broadcasted_iota(I32, (LANES, LANES), 0)
    sel = (sc > thr_q) | ((sc == thr_q) & (key_index <= ithr_q))
    return jnp.where(sel, 0.0, NEG_BIG).astype(F32).T


def _dsa_prompt_kernel(q_ref, qi_ref, kiw_ref, kb_ref, vb_ref, kib_ref, o_ref,
                       score_ref, ithr_ref, qir_ref, s_ref, mlane_ref, llane_ref, acc_ref):
    i = pl.program_id(1)
    nchunk = (i * TQ + TQ + TK - 1) // TK
    qpos = i * TQ + lax.broadcasted_iota(I32, (1, LANES), 1)

    def for_chunks(body):
        def quad(c, carry):
            body(pl.multiple_of(c * (4 * TK), 4 * TK), 4 * TK)
            return carry
        nquad = nchunk // 4
        lax.fori_loop(0, nquad, quad, 0)

        @pl.when(nchunk % 4 >= 2)
        def _():
            body(pl.multiple_of(nquad * (4 * TK), 2 * TK), 2 * TK)

        @pl.when(nchunk % 2 == 1)
        def _():
            body(pl.multiple_of((nchunk - 1) * TK, TK), TK)

    for h in range(IDX_HEADS):
        qir_ref[h * TQ:(h + 1) * TQ, :] = qi_ref[0, :, h * IDX_DIM:(h + 1) * IDX_DIM]
    kiw_t = kiw_ref[0].T
    w_rows = [kiw_t[IDX_DIM + h:IDX_DIM + h + 1, :] * IDX_HEAD_SCALE for h in range(IDX_HEADS)]

    def idx_body(c0, width):
        logit = _dot_nt(kib_ref[0, pl.ds(c0, width), :], qir_ref[...])
        score = jnp.zeros((width, LANES), F32)
        for h in range(IDX_HEADS):
            score = score + jnp.maximum(logit[:, h * TQ:(h + 1) * TQ], 0.0) * w_rows[h]
        krow = c0 + lax.broadcasted_iota(I32, (width, LANES), 0)
        score_ref[pl.ds(c0, width), :] = jnp.where(krow <= qpos, score, -jnp.inf)

    for_chunks(idx_body)
    thr = _select_threshold(score_ref, ithr_ref, nchunk, TK)
    thr_q = jnp.broadcast_to(thr, (LANES, LANES))
    ithr_q = jnp.broadcast_to(ithr_ref[0:1, :], (LANES, LANES))

    def q_group(g):
        return jnp.concatenate(
            [q_ref[0, :, (g * GROUP + r) * HEAD_DIM:(g * GROUP + r + 1) * HEAD_DIM] for r in range(GROUP)], axis=0)

    mlane_ref[...] = jnp.full(mlane_ref.shape, NEG_BIG, F32)

    def logits_body(c0, width):
        bias = jnp.concatenate(
            [_bias_block(score_ref, pl.multiple_of(c0 + j * LANES, LANES), thr_q, ithr_q)
             for j in range(width // LANES)], axis=1)
        bias4 = jnp.concatenate([bias] * GROUP, axis=0)
        for g in range(N_KV_HEADS):
            s = _dot_nt(q_group(g), kb_ref[0, pl.ds(c0, width), g * HEAD_DIM:(g + 1) * HEAD_DIM]) + bias4
            s_ref[g, :, pl.ds(c0, width)] = s
            m = mlane_ref[g]
            for j in range(width // LANES):
                m = jnp.maximum(m, s[:, j * LANES:(j + 1) * LANES])
            mlane_ref[g] = m

    for_chunks(logits_body)

    for g in range(N_KV_HEADS):
        mlane_ref[g] = jnp.broadcast_to(jnp.max(mlane_ref[g], axis=1, keepdims=True), mlane_ref.shape[1:])
    llane_ref[...] = jnp.zeros(llane_ref.shape, F32)
    acc_ref[...] = jnp.zeros(acc_ref.shape, F32)

    def pv_body(c0, width):
        for g in range(N_KV_HEADS):
            m = mlane_ref[g]
            l = llane_ref[g]
            ps = []
            for j in range(width // LANES):
                pj = jnp.exp2(s_ref[g, :, pl.ds(pl.multiple_of(c0 + j * LANES, LANES), LANES)] - m)
                l = l + pj
                ps.append(pj.astype(BF16))
            llane_ref[g] = l
            p = jnp.concatenate(ps, axis=1)
            acc_ref[g] += _dot(p, vb_ref[0, pl.ds(c0, width), g * HEAD_DIM:(g + 1) * HEAD_DIM])

    for_chunks(pv_body)

    for g in range(N_KV_HEADS):
        og = acc_ref[g] * (1.0 / jnp.sum(llane_ref[g], axis=1, keepdims=True))
        for r in range(GROUP):
            col = (g * GROUP + r) * HEAD_DIM
            o_ref[0, :, col:col + HEAD_DIM] = og[r * TQ:(r + 1) * TQ].astype(BF16)


def _dsa_prompt_call(q, qi, kiw, kb, vb, kib):
    nb, s, _ = q.shape
    tile = lambda w: pl.BlockSpec((1, TQ, w), lambda b, i: (b, i, 0))
    full = lambda w: pl.BlockSpec((1, s, w), lambda b, i: (b, 0, 0))
    rows = GROUP * TQ
    return pl.pallas_call(
        _dsa_prompt_kernel,
        grid=(nb, s // TQ),
        in_specs=[tile(Q_W), tile(QI_W), tile(KIW_W), full(KV_W), full(KV_W), full(IDX_DIM)],
        out_specs=tile(Q_W),
        out_shape=jax.ShapeDtypeStruct((nb, s, Q_W), BF16),
        scratch_shapes=[
            pltpu.VMEM((s, TQ), F32),
            pltpu.VMEM((SUBLANES, LANES), I32),
            pltpu.VMEM((IDX_HEADS * TQ, IDX_DIM), BF16),
            pltpu.VMEM((N_KV_HEADS, rows, s), F32),
            pltpu.VMEM((N_KV_HEADS, rows, LANES), F32),
            pltpu.VMEM((N_KV_HEADS, rows, LANES), F32),
            pltpu.VMEM((N_KV_HEADS, rows, HEAD_DIM), F32),
        ],
        compiler_params=_params(("arbitrary", "arbitrary")),
        name="dsa_prompt",
    )(q, qi, kiw, kb, vb, kib)


def _page_specs(page_shape, nseq):
    zeros = (0,) * len(page_shape)
    return [pl.BlockSpec((1,) + page_shape,
                         functools.partial(lambda i, pt, u, p: (pt[i * nseq + u, p],) + zeros, u=u, p=p))
            for u in range(nseq) for p in range(N_PAGES)]


def _kv_page(page_ref):
    heads = [page_ref[0, pl.ds(g, PAGE_SIZE, stride=N_KV_HEADS), :] for g in range(N_KV_HEADS)]
    return jnp.concatenate(heads, axis=1).astype(BF16)


def _pad_new_page(new8):
    return jnp.concatenate([new8, jnp.zeros((PAGE_SIZE - Q8, new8.shape[1]), F32)], axis=0).astype(BF16)


def _idx_sample_kernel(pt_ref, qis_ref, ws_ref, kin_ref, *rest):
    nseq = qis_ref.shape[0]
    pages = rest[:nseq * N_PAGES]
    o_ref = rest[nseq * N_PAGES]
    del pt_ref
    for u in range(nseq):
        qs = qis_ref[u]
        w = ws_ref[u]

        def page_score(kt):
            r = jnp.maximum(_dot(qs, kt.astype(BF16)), 0.0) * w
            s = r[0:Q8]
            for h in range(1, IDX_HEADS):
                s = s + r[h * Q8:(h + 1) * Q8]
            return s

        for p in range(N_PAGES):
            o_ref[u, :, p * PAGE_SIZE:(p + 1) * PAGE_SIZE] = page_score(pages[u * N_PAGES + p][0])
        o_ref[u, :, PAST_LEN:L_SAMPLE] = page_score(kin_ref[u])


def _idx_sample_call(page_table, qis, ws, kinew_t, cache_ki_t):
    n = qis.shape[0]
    rows = IDX_HEADS * Q8
    nseq = SCORE_SEQS
    grid_spec = pltpu.PrefetchScalarGridSpec(
        num_scalar_prefetch=1,
        grid=(n // nseq,),
        in_specs=[
            pl.BlockSpec((nseq, rows, IDX_DIM), lambda i, pt: (i, 0, 0)),
            pl.BlockSpec((nseq, rows, 1), lambda i, pt: (i, 0, 0)),
            pl.BlockSpec((nseq, IDX_DIM, PAGE_SIZE), lambda i, pt: (i, 0, 0)),
        ] + _page_specs((IDX_DIM, PAGE_SIZE), nseq),
        out_specs=pl.BlockSpec((nseq, Q8, L_SAMPLE), lambda i, pt: (i, 0, 0)),
    )
    return pl.pallas_call(
        _idx_sample_kernel,
        grid_spec=grid_spec,
        out_shape=jax.ShapeDtypeStruct((n, Q8, L_SAMPLE), F32),
        compiler_params=_params(("arbitrary",)),
        name="dsa_sample_scores",
    )(page_table, qis, ws, kinew_t, *([cache_ki_t] * (nseq * N_PAGES)))


def _select_sample_kernel(s_ref, bias_ref, score_ref, ithr_ref):
    krow = lax.broadcasted_iota(I32, (LANES, SEL_ROWS), 0)
    qrow = lax.broadcasted_iota(I32, (LANES, SEL_ROWS), 1) % Q8
    real = qrow < T_NEW
    for j in range(L_SAMPLE // LANES):
        kidx = j * LANES + krow
        valid = ((kidx < PAST_LEN) | ((kidx - PAST_LEN) <= qrow)) & real
        score_ref[j * LANES:(j + 1) * LANES, :] = jnp.where(valid, s_ref[:, j * LANES:(j + 1) * LANES].T, -jnp.inf)
    thr = _select_threshold(score_ref, ithr_ref, L_SAMPLE // LANES, LANES)
    thr_q = jnp.broadcast_to(thr, (LANES, SEL_ROWS))
    ithr_q = jnp.broadcast_to(ithr_ref[0:1, :], (LANES, SEL_ROWS))
    for j in range(L_SAMPLE // LANES):
        bias_ref[:, j * LANES:(j + 1) * LANES] = _bias_block(score_ref, j * LANES, thr_q, ithr_q)


def _select_sample_call(scores):
    rows = scores.shape[0]
    return pl.pallas_call(
        _select_sample_kernel,
        grid=(rows // SEL_ROWS,),
        in_specs=[pl.BlockSpec((SEL_ROWS, L_SAMPLE), lambda i: (i, 0))],
        out_specs=pl.BlockSpec((SEL_ROWS, L_SAMPLE), lambda i: (i, 0)),
        out_shape=jax.ShapeDtypeStruct((rows, L_SAMPLE), F32),
        scratch_shapes=[pltpu.VMEM((L_SAMPLE, SEL_ROWS), F32), pltpu.VMEM((SUBLANES, LANES), I32)],
        compiler_params=_params(("arbitrary",)),
        name="dsa_sample_select",
    )(scores)


def _attn_sample_kernel(pt_ref, qb_ref, bias_ref, kn_ref, vn_ref, *rest):
    nseq = qb_ref.shape[0]
    kpages = rest[:nseq * N_PAGES]
    vpages = rest[nseq * N_PAGES:2 * nseq * N_PAGES]
    o_ref = rest[2 * nseq * N_PAGES]
    s_ref = rest[2 * nseq * N_PAGES + 1]
    del pt_ref
    for u in range(nseq):
        qb = qb_ref[u]
        reps = qb.shape[0] // Q8

        def page_logits(p, kp):
            bias8 = bias_ref[u, :, p * PAGE_SIZE:(p + 1) * PAGE_SIZE]
            s_ref[u, :, p * PAGE_SIZE:(p + 1) * PAGE_SIZE] = (
                _dot_nt(qb, kp) + jnp.concatenate([bias8] * reps, axis=0))

        for p in range(N_PAGES):
            page_logits(p, _kv_page(kpages[u * N_PAGES + p]))
        page_logits(N_PAGES, _pad_new_page(kn_ref[u]))

        s = s_ref[u]
        m = jnp.max(s, axis=1, keepdims=True)
        e = jnp.exp2(s - m)
        l = jnp.sum(e, axis=1, keepdims=True)
        eb = e.astype(BF16)
        acc = _dot(eb[:, PAST_LEN:L_SAMPLE], _pad_new_page(vn_ref[u]))
        for p in range(N_PAGES):
            acc = acc + _dot(eb[:, p * PAGE_SIZE:(p + 1) * PAGE_SIZE], _kv_page(vpages[u * N_PAGES + p]))
        o_ref[u] = acc * (1.0 / l)


def _attn_sample_call(page_table, qblk, bias, knew8, vnew8, cache_k, cache_v):
    n, rows, _ = qblk.shape
    nseq = ATTN_SEQS
    page = (PAGE_SIZE * N_KV_HEADS, HEAD_DIM)
    grid_spec = pltpu.PrefetchScalarGridSpec(
        num_scalar_prefetch=1,
        grid=(n // nseq,),
        in_specs=[
            pl.BlockSpec((nseq, rows, KV_W), lambda i, pt: (i, 0, 0)),
            pl.BlockSpec((nseq, Q8, L_SAMPLE), lambda i, pt: (i, 0, 0)),
            pl.BlockSpec((nseq, Q8, KV_W), lambda i, pt: (i, 0, 0)),
            pl.BlockSpec((nseq, Q8, KV_W), lambda i, pt: (i, 0, 0)),
        ] + _page_specs(page, nseq) + _page_specs(page, nseq),
        out_specs=pl.BlockSpec((nseq, rows, KV_W), lambda i, pt: (i, 0, 0)),
        scratch_shapes=[pltpu.VMEM((nseq, rows, L_SAMPLE), F32)],
    )
    return pl.pallas_call(
        _attn_sample_kernel,
        grid_spec=grid_spec,
        out_shape=jax.ShapeDtypeStruct((n, rows, KV_W), F32),
        compiler_params=_params(("arbitrary",)),
        name="dsa_sample_attn",
    )(page_table, qblk, bias, knew8, vnew8, *([cache_k] * (nseq * N_PAGES)), *([cache_v] * (nseq * N_PAGES)))


def _mlp_resid(x1, g_ffn, shift, scale, gate, wup_ref, wdn_ref, acc_ref):
    h = _rms_mod(x1, g_ffn, shift, scale).astype(BF16)
    for c in range(D_FF // FF_CHUNK):
        u = _dot(h, wup_ref[:, c * FF_CHUNK:(c + 1) * FF_CHUNK])
        u = jnp.square(jnp.maximum(u, 0.0)).astype(BF16)
        d = _dot(u, wdn_ref[c * FF_CHUNK:(c + 1) * FF_CHUNK, :])
        if c == 0:
            acc_ref[...] = d
        else:
            acc_ref[...] += d
    return x1 + gate * acc_ref[...]


def _post0_kernel(x_ref, o_ref, g1_ref, sh2_ref, sc2_ref, g2_ref, gffn_ref, wo_ref, wup_ref, wdn_ref,
                  out_ref, acc_ref):
    x1 = x_ref[0] + g1_ref[0] * _dot(o_ref[0], wo_ref[...])
    out_ref[0] = _mlp_resid(x1, gffn_ref[...], sh2_ref[0], sc2_ref[0], g2_ref[0], wup_ref, wdn_ref, acc_ref)


def _const_spec(shape):
    nd = len(shape)
    return pl.BlockSpec(shape, lambda b, i: (0,) * nd)


def _post0_call(x, o, g1, sh2, sc2, g2, g_ffn, w_o, w_up, w_dn, tm):
    nb, t, _ = x.shape
    tok = lambda w: pl.BlockSpec((1, tm, w), lambda b, i: (b, i, 0))
    return pl.pallas_call(
        _post0_kernel,
        grid=(nb, t // tm),
        in_specs=[tok(D_MODEL), tok(Q_W), g1.spec(tm), sh2.spec(tm), sc2.spec(tm),
                  g2.spec(tm), _const_spec((1, D_MODEL)), _const_spec((Q_W, D_MODEL)),
                  _const_spec((D_MODEL, D_FF)), _const_spec((D_FF, D_MODEL))],
        out_specs=tok(D_MODEL),
        out_shape=jax.ShapeDtypeStruct((nb, t, D_MODEL), F32),
        scratch_shapes=[pltpu.VMEM((tm, D_MODEL), F32)],
        compiler_params=_params(("arbitrary", "arbitrary")),
        name="layer0_out",
    )(x, o, g1.arr, sh2.arr, sc2.arr, g2.arr, g_ffn.reshape(1, D_MODEL), w_o, w_up, w_dn)


def _pool_project(pooled, wp_ref, ps_ref):
    parts = []
    for g in range(len(POOL_WINDOWS)):
        lo = g * POOL_GROUP_DIM
        parts.append(_dot(pooled[:, lo:lo + POOL_GROUP_DIM].astype(BF16), wp_ref[g]))
    return jnp.concatenate(parts, axis=1) * ps_ref[...]


def _layer1_prompt_kernel(x_ref, xh_ref, sh1_ref, sc1_ref, g1_ref, sh2_ref, sc2_ref, g2_ref,
                          gmix_ref, gffn_ref, gfin_ref, wp_ref, ps_ref, wup_ref, wdn_ref,
                          y_ref, hlast_ref, acc_ref):
    i = pl.program_id(1)
    tm = x_ref.shape[1]
    x = x_ref[0]
    sh1, sc1 = sh1_ref[0], sc1_ref[0]
    h = _rms_mod(x, gmix_ref[...], sh1, sc1)
    h_halo = _rms_mod(xh_ref[0], gmix_ref[...], sh1, sc1)
    h_halo = jnp.where(i > 0, h_halo, 0.0)
    ext = jnp.concatenate([h_halo, h], axis=0)
    pos = i * tm + lax.broadcasted_iota(I32, (tm, 1), 0)
    parts = []
    for g, w in enumerate(POOL_WINDOWS):
        lo = g * POOL_GROUP_DIM
        s = ext[:, lo:lo + POOL_GROUP_DIM]
        step = 1
        while step < w:
            s = s + pltpu.roll(s, step, axis=0)
            step *= 2
        cnt = jnp.minimum(w, pos + 1).astype(F32)
        parts.append(s[HALO:, :] / cnt - h[:, lo:lo + POOL_GROUP_DIM])
    pooled = jnp.concatenate(parts, axis=1)
    x1 = x + g1_ref[0] * _pool_project(pooled, wp_ref, ps_ref)
    x2 = _mlp_resid(x1, gffn_ref[...], sh2_ref[0], sc2_ref[0], g2_ref[0], wup_ref, wdn_ref, acc_ref)
    y_ref[0] = _rmsnorm(x2, gfin_ref[...])

    @pl.when(i == pl.num_programs(1) - 1)
    def _():
        hlast_ref[0] = h[tm - HALO:, :]


def _layer1_prompt_call(x, mods, g_mix, g_ffn, g_fin, pool_w, pool_scale, w_up, w_dn, tm):
    nb, t, _ = x.shape
    tok = pl.BlockSpec((1, tm, D_MODEL), lambda b, i: (b, i, 0))
    halo = pl.BlockSpec((1, HALO, D_MODEL), lambda b, i: (b, jnp.maximum(i * (tm // HALO) - 1, 0), 0))
    vec = _const_spec((1, D_MODEL))
    return pl.pallas_call(
        _layer1_prompt_kernel,
        grid=(nb, t // tm),
        in_specs=[tok, halo] + [m.spec(tm) for m in mods] + [vec, vec, vec,
                  _const_spec(pool_w.shape), vec, _const_spec((D_MODEL, D_FF)), _const_spec((D_FF, D_MODEL))],
        out_specs=[tok, pl.BlockSpec((1, HALO, D_MODEL), lambda b, i: (b, 0, 0))],
        out_shape=[jax.ShapeDtypeStruct((nb, t, D_MODEL), F32), jax.ShapeDtypeStruct((nb, HALO, D_MODEL), F32)],
        scratch_shapes=[pltpu.VMEM((tm, D_MODEL), F32)],
        compiler_params=_params(("arbitrary", "arbitrary")),
        name="layer1_prompt",
    )(x, x, *[m.arr for m in mods], g_mix.reshape(1, D_MODEL), g_ffn.reshape(1, D_MODEL),
      g_fin.reshape(1, D_MODEL), pool_w, pool_scale.reshape(1, D_MODEL), w_up, w_dn)


def _layer1_sample_kernel(x_ref, prev_ref, sh1_ref, sc1_ref, g1_ref, sh2_ref, sc2_ref, g2_ref,
                          gmix_ref, gffn_ref, gfin_ref, wp_ref, ps_ref, wup_ref, wdn_ref,
                          y_ref, h_ref, hs_ref, pooled_ref, acc_ref):
    tm = x_ref.shape[1]
    t_new = tm // prev_ref.shape[1]
    nseq = prev_ref.shape[1]
    x = x_ref[0]
    h = _rms_mod(x, gmix_ref[...], sh1_ref[0], sc1_ref[0])
    h_ref[0] = h
    ncol = D_MODEL // LANES
    for c in range(ncol):
        hs_ref[c] = h[:, c * LANES:(c + 1) * LANES]
    ext = [prev_ref[j] for j in range(POOL_STATE)]
    ext += [jnp.concatenate([hs_ref[c, pl.ds(t, nseq, stride=t_new), :] for c in range(ncol)], axis=1)
            for t in range(t_new)]
    for t in range(t_new):
        parts = []
        for g, w in enumerate(POOL_WINDOWS):
            lo = g * POOL_GROUP_DIM
            s = ext[POOL_STATE + t][:, lo:lo + POOL_GROUP_DIM]
            for j in range(1, w):
                s = s + ext[POOL_STATE + t - j][:, lo:lo + POOL_GROUP_DIM]
            parts.append(s / float(w) - ext[POOL_STATE + t][:, lo:lo + POOL_GROUP_DIM])
        pooled_t = jnp.concatenate(parts, axis=1)
        for c in range(ncol):
            pooled_ref[c, pl.ds(t, nseq, stride=t_new), :] = pooled_t[:, c * LANES:(c + 1) * LANES]
    pooled = jnp.concatenate([pooled_ref[c] for c in range(ncol)], axis=1)
    x1 = x + g1_ref[0] * _pool_project(pooled, wp_ref, ps_ref)
    x2 = _mlp_resid(x1, gffn_ref[...], sh2_ref[0], sc2_ref[0], g2_ref[0], wup_ref, wdn_ref, acc_ref)
    y_ref[0] = _rmsnorm(x2, gfin_ref[...])


def _layer1_sample_call(x, prev_t, mods, g_mix, g_ffn, g_fin, pool_w, pool_scale, w_up, w_dn, tm, t_new):
    nb, t, _ = x.shape
    tok = pl.BlockSpec((1, tm, D_MODEL), lambda b, i: (b, i, 0))
    prev = pl.BlockSpec((POOL_STATE, tm // t_new, D_MODEL), lambda b, i: (0, i, 0))
    vec = _const_spec((1, D_MODEL))
    return pl.pallas_call(
        _layer1_sample_kernel,
        grid=(nb, t // tm),
        in_specs=[tok, prev] + [m.spec(tm) for m in mods] + [vec, vec, vec,
                  _const_spec(pool_w.shape), vec, _const_spec((D_MODEL, D_FF)), _const_spec((D_FF, D_MODEL))],
        out_specs=[tok, tok],
        out_shape=[jax.ShapeDtypeStruct((nb, t, D_MODEL), F32), jax.ShapeDtypeStruct((nb, t, D_MODEL), F32)],
        scratch_shapes=[pltpu.VMEM((D_MODEL // LANES, tm, LANES), F32),
                        pltpu.VMEM((D_MODEL // LANES, tm, LANES), F32),
                        pltpu.VMEM((tm, D_MODEL), F32)],
        compiler_params=_params(("arbitrary", "arbitrary")),
        name="layer1_sample",
    )(x, prev_t, *[m.arr for m in mods], g_mix.reshape(1, D_MODEL), g_ffn.reshape(1, D_MODEL),
      g_fin.reshape(1, D_MODEL), pool_w, pool_scale.reshape(1, D_MODEL), w_up, w_dn)


def kernel(x_prompt, x_sample, cache_k, cache_v, cache_kidx, state_pool, page_table, c_prompt, c_sample,
           norm_mix_g, norm_ffn_g, w_mod, b_mod, dsa_w_in, dsa_w_o, pool_w, pool_scale, w_up, w_down,
           final_norm_g):
    nbp, seq, _ = x_prompt.shape
    nbs, t_new, _ = x_sample.shape
    n_tok_s = nbs * t_new

    w_in_b = jnp.pad(dsa_w_in[0], ((0, 0), (0, D_PROJ_PAD - D_PROJ))).astype(BF16)
    w_o_b = dsa_w_o[0].astype(BF16)
    w_up_b = w_up.astype(BF16)
    w_dn_b = w_down.astype(BF16)
    pool_w_b = pool_w[0].astype(BF16)

    n_c = nbp + nbs
    c_rows = -(-n_c // SUBLANES) * SUBLANES
    c_all = jnp.pad(jnp.concatenate([c_prompt, c_sample], axis=0), ((0, c_rows - n_c), (0, 0)))
    mod, mod_tok = _mod_call(c_all, w_mod, b_mod, nbp, nbs, t_new)

    def mods_prompt(layer):
        m = mod[layer, :nbp].reshape(nbp, 1, 6, D_MODEL)
        return [_mod_per_sequence(m[:, :, j]) for j in range(6)]

    def mods_sample(layer):
        return [_mod_per_token(mod_tok, layer, j) for j in range(6)]

    sh1, sc1, g1, sh2, sc2, g2 = mods_prompt(0)
    k_p, v_p, ki_p, q, qi, kiw, kb, vb, kib = _proj_call(x_prompt, sh1, sc1, norm_mix_g[0], w_in_b, 512)
    o = _dsa_prompt_call(q, qi, kiw, kb, vb, kib)
    x1 = _post0_call(x_prompt, o, g1, sh2, sc2, g2, norm_ffn_g[0], w_o_b, w_up_b[0], w_dn_b[0], 512)
    y_prompt, hlast = _layer1_prompt_call(x1, mods_prompt(1), norm_mix_g[1], norm_ffn_g[1], final_norm_g,
                                          pool_w_b, pool_scale[0], w_up_b[1], w_dn_b[1], 512)
    pool_p = hlast[:, HALO - POOL_STATE:]

    sh1, sc1, g1, sh2, sc2, g2 = mods_sample(0)
    xs = x_sample.reshape(1, n_tok_s, D_MODEL)
    tms = 256
    k_s, v_s, ki_s, q, qi, kiw, _, _, _ = _proj_call(xs, sh1, sc1, norm_mix_g[0], w_in_b, tms)

    pad_q = lambda a: jnp.pad(a, [(0, 0)] * (a.ndim - 2) + [(0, Q8 - t_new), (0, 0)])
    qis = pad_q(qi.reshape(nbs, t_new, IDX_HEADS, IDX_DIM).transpose(0, 2, 1, 3)).reshape(nbs, IDX_HEADS * Q8, IDX_DIM)
    wi = kiw.reshape(nbs, t_new, KIW_W)[:, :, IDX_DIM:IDX_DIM + IDX_HEADS] * IDX_HEAD_SCALE
    ws = pad_q(wi.transpose(0, 2, 1)[..., None]).reshape(nbs, IDX_HEADS * Q8, 1)
    knew8 = pad_q(k_s.reshape(nbs, t_new, KV_W))
    vnew8 = pad_q(v_s.reshape(nbs, t_new, KV_W))
    kinew_t = jnp.pad(ki_s.reshape(nbs, t_new, IDX_DIM).transpose(0, 2, 1), ((0, 0), (0, 0), (0, PAGE_SIZE - t_new)))
    scores = _idx_sample_call(page_table, qis, ws, kinew_t, cache_kidx[0].transpose(0, 2, 1))
    bias = _select_sample_call(scores.reshape(nbs * Q8, L_SAMPLE)).reshape(nbs, Q8, L_SAMPLE)
    qg = pad_q(q.reshape(nbs, t_new, N_KV_HEADS, GROUP, HEAD_DIM).transpose(0, 2, 3, 1, 4))
    eye = jnp.eye(N_KV_HEADS, dtype=BF16)
    qblk = (qg[:, :, :, :, None, :] * eye[None, :, None, None, :, None]).reshape(nbs, N_KV_HEADS * GROUP * Q8, KV_W)
    n_pool = cache_k.shape[1]
    page_rows = lambda c: c[0].reshape(n_pool, PAGE_SIZE * N_KV_HEADS, HEAD_DIM)
    o_blk = _attn_sample_call(page_table, qblk, bias, knew8, vnew8, page_rows(cache_k), page_rows(cache_v))
    o_blk = o_blk.reshape(nbs, N_KV_HEADS, GROUP, Q8, N_KV_HEADS, HEAD_DIM)
    o_s = jnp.stack([o_blk[:, g, :, :t_new, g] for g in range(N_KV_HEADS)], axis=1)
    o_s = o_s.transpose(0, 3, 1, 2, 4).reshape(1, n_tok_s, Q_W).astype(BF16)
    x1s = _post0_call(xs, o_s, g1, sh2, sc2, g2, norm_ffn_g[0], w_o_b, w_up_b[0], w_dn_b[0], tms)
    prev_t = state_pool[0].transpose(1, 0, 2)
    y_s, h1s = _layer1_sample_call(x1s, prev_t, mods_sample(1), norm_mix_g[1], norm_ffn_g[1], final_norm_g,
                                   pool_w_b, pool_scale[0], w_up_b[1], w_dn_b[1], tms, t_new)
    y_sample = y_s.reshape(nbs, t_new, D_MODEL)
    pool_s = jnp.concatenate([state_pool[0][:, t_new:], h1s.reshape(nbs, t_new, D_MODEL)], axis=1)

    return (
        y_prompt, y_sample,
        k_p[None], v_p[None], ki_p[None], pool_p[None],
        k_s.reshape(1, nbs, t_new, N_KV_HEADS, HEAD_DIM), v_s.reshape(1, nbs, t_new, N_KV_HEADS, HEAD_DIM),
        ki_s.reshape(1, nbs, t_new, IDX_DIM), pool_s[None],
    )
```

```python
import functools
from typing import Callable, NamedTuple

import jax
import jax.numpy as jnp
from jax import lax
from jax.experimental import pallas as pl
from jax.experimental.pallas import tpu as pltpu

F32 = jnp.float32
BF16 = jnp.bfloat16
I32 = jnp.int32

D_MODEL = 1024
DEPTH = 2
PAST_LEN = 2048
PAGE_SIZE = 128
N_PAGES = PAST_LEN // PAGE_SIZE
N_HEADS = 8
HEAD_DIM = 128
N_KV_HEADS = 2
GROUP = N_HEADS // N_KV_HEADS
IDX_HEADS = 8
IDX_DIM = 64
TOPK = 256
POOL_WINDOWS = (2, 4, 8, 16)
POOL_GROUP_DIM = D_MODEL // len(POOL_WINDOWS)
POOL_STATE = max(POOL_WINDOWS) - 1
D_FF = 4 * D_MODEL
RMS_EPS = 1e-6
Q_W = N_HEADS * HEAD_DIM
KV_W = N_KV_HEADS * HEAD_DIM
QI_W = IDX_HEADS * IDX_DIM
D_PROJ = Q_W + 2 * KV_W + QI_W + IDX_DIM + IDX_HEADS

LANES = 128
SUBLANES = 8
VMEM_LIMIT_BYTES = 56 * 1024 * 1024

KIW_W = LANES
D_PROJ_PAD = Q_W + 2 * KV_W + QI_W + KIW_W
OFF_K = Q_W
OFF_V = OFF_K + KV_W
OFF_QI = OFF_V + KV_W
OFF_KIW = OFF_QI + QI_W

LOG2_E = 1.4426950408889634
ATT_SCALE = HEAD_DIM ** -0.5 * LOG2_E
IDX_SCALE = IDX_DIM ** -0.5
IDX_HEAD_SCALE = IDX_HEADS ** -0.5

INT_MIN = -(2 ** 31)
INT_MAX = 2 ** 31 - 1
F32_MAX = 3.4028234663852886e38
NEG_INF_KEY = INT_MIN + (1 << 23)
NEG_BIG = -1e30

TQ = 128
TK = 512
FF_CHUNK = 512
HALO = 16
SEL_ROWS = 128
L_SAMPLE = PAST_LEN + PAGE_SIZE
T_NEW = 4
SCORE_SEQS = 8
ATTN_SEQS = 4
Q8 = SUBLANES


def _params(sem):
    return pltpu.CompilerParams(dimension_semantics=sem, vmem_limit_bytes=VMEM_LIMIT_BYTES)


def _dot(a, b):
    return jnp.dot(a, b, preferred_element_type=F32)


def _dot_nt(a, b):
    return lax.dot_general(a, b, (((1,), (1,)), ((), ())), preferred_element_type=F32)


def _rms_mod(x, g, shift, scale):
    ms = jnp.mean(x * x, axis=-1, keepdims=True)
    y = x * lax.rsqrt(ms + RMS_EPS) * g
    return y * (1.0 + scale) + shift


def _rmsnorm(x, g):
    ms = jnp.mean(x * x, axis=-1, keepdims=True)
    return x * lax.rsqrt(ms + RMS_EPS) * g


def _split3(x):
    hi = x.astype(BF16)
    r1 = x - hi.astype(F32)
    mid = r1.astype(BF16)
    lo = (r1 - mid.astype(F32)).astype(BF16)
    return hi, mid, lo


def _mod_kernel(c_ref, w_ref, b_ref, o_ref, tok_ref, *, n_prompt, t_new):
    c = c_ref[...]
    a = c * (1.0 / (1.0 + jnp.exp(-c)))
    w = w_ref[0]
    a_hi = a.astype(BF16)
    a_lo = (a - a_hi.astype(F32)).astype(BF16)
    w_hi = w.astype(BF16)
    w_lo = (w - w_hi.astype(F32)).astype(BF16)
    mod = _dot(a_hi, w_hi) + _dot(a_lo, w_hi) + _dot(a_hi, w_lo) + b_ref[0]
    o_ref[0] = mod
    n_tok, rows = tok_ref.shape[1], mod.shape[0]
    r = lax.broadcasted_iota(I32, (n_tok, rows), 0)
    s = (lax.broadcasted_iota(I32, (n_tok, rows), 1) - n_prompt) * t_new
    pick = jnp.where((r >= s) & (r < s + t_new), 1.0, 0.0).astype(BF16)
    hi, mid, lo = _split3(mod)
    tok_ref[0] = (_dot(pick, hi) + _dot(pick, mid)) + _dot(pick, lo)


def _mod_call(c_all, w_mod, b_mod, n_prompt, n_sample, t_new):
    rows = c_all.shape[0]
    tn = 1536
    return pl.pallas_call(
        functools.partial(_mod_kernel, n_prompt=n_prompt, t_new=t_new),
        grid=(DEPTH, 6 * D_MODEL // tn),
        in_specs=[
            pl.BlockSpec((rows, D_MODEL), lambda i, j: (0, 0)),
            pl.BlockSpec((1, D_MODEL, tn), lambda i, j: (i, 0, j)),
            pl.BlockSpec((1, 1, tn), lambda i, j: (i, 0, j)),
        ],
        out_specs=[pl.BlockSpec((1, rows, tn), lambda i, j: (i, 0, j)),
                   pl.BlockSpec((1, n_sample * t_new, tn), lambda i, j: (i, 0, j))],
        out_shape=[jax.ShapeDtypeStruct((DEPTH, rows, 6 * D_MODEL), F32),
                   jax.ShapeDtypeStruct((DEPTH, n_sample * t_new, 6 * D_MODEL), F32)],
        compiler_params=_params(("arbitrary", "arbitrary")),
        name="mod_vectors",
    )(c_all, w_mod, b_mod.reshape(DEPTH, 1, 6 * D_MODEL))


def _proj_kernel(x_ref, sh_ref, sc_ref, g_ref, w_ref,
                 k_ref, v_ref, ki_ref, q_ref, qi_ref, kiw_ref, kb_ref, vb_ref, kib_ref):
    h = _rms_mod(x_ref[0], g_ref[...], sh_ref[0], sc_ref[0]).astype(BF16)
    q_ref[0] = (_dot(h, w_ref[:, 0:Q_W]) * ATT_SCALE).astype(BF16)
    k = _dot(h, w_ref[:, OFF_K:OFF_K + KV_W])
    v = _dot(h, w_ref[:, OFF_V:OFF_V + KV_W])
    for g in range(N_KV_HEADS):
        k_ref[0, :, g, :] = k[:, g * HEAD_DIM:(g + 1) * HEAD_DIM]
        v_ref[0, :, g, :] = v[:, g * HEAD_DIM:(g + 1) * HEAD_DIM]
    kb_ref[0] = k.astype(BF16)
    vb_ref[0] = v.astype(BF16)
    qi_ref[0] = (_dot(h, w_ref[:, OFF_QI:OFF_QI + QI_W]) * IDX_SCALE).astype(BF16)
    kiw = _dot(h, w_ref[:, OFF_KIW:OFF_KIW + KIW_W])
    kiw_ref[0] = kiw
    ki = kiw[:, 0:IDX_DIM]
    ki_ref[0] = ki
    kib_ref[0] = ki.astype(BF16)


class _Mod(NamedTuple):
    arr: jax.Array
    spec: Callable[[int], pl.BlockSpec]


def _mod_per_sequence(arr):
    return _Mod(arr, lambda tm: pl.BlockSpec((1, 1, D_MODEL), lambda b, i: (b, 0, 0)))


def _mod_per_token(mod_tok, layer, j):
    return _Mod(mod_tok, lambda tm: pl.BlockSpec((1, tm, D_MODEL), lambda b, i: (layer, i, j)))


def _proj_call(x, shift, scale, g, w_in_b, tm):
    nb, t, _ = x.shape
    tok = lambda w: pl.BlockSpec((1, tm, w), lambda b, i: (b, i, 0))
    shp = lambda w, dt: jax.ShapeDtypeStruct((nb, t, w), dt)
    heads = pl.BlockSpec((1, tm, N_KV_HEADS, HEAD_DIM), lambda b, i: (b, i, 0, 0))
    heads_shape = jax.ShapeDtypeStruct((nb, t, N_KV_HEADS, HEAD_DIM), F32)
    return pl.pallas_call(
        _proj_kernel,
        grid=(nb, t // tm),
        in_specs=[
            tok(D_MODEL), shift.spec(tm), scale.spec(tm),
            pl.BlockSpec((1, D_MODEL), lambda b, i: (0, 0)),
            pl.BlockSpec((D_MODEL, D_PROJ_PAD), lambda b, i: (0, 0)),
        ],
        out_specs=[heads, heads, tok(IDX_DIM), tok(Q_W), tok(QI_W), tok(KIW_W),
                   tok(KV_W), tok(KV_W), tok(IDX_DIM)],
        out_shape=[heads_shape, heads_shape, shp(IDX_DIM, F32), shp(Q_W, BF16), shp(QI_W, BF16),
                   shp(KIW_W, F32), shp(KV_W, BF16), shp(KV_W, BF16), shp(IDX_DIM, BF16)],
        compiler_params=_params(("arbitrary", "arbitrary")),
        name="dsa_proj",
    )(x, shift.arr, scale.arr, g.reshape(1, D_MODEL), w_in_b)


def _key_float(k):
    return lax.bitcast_convert_type(jnp.where(k >= 0, k, (k - 1) ^ INT_MAX), F32)


def _select_threshold(score_ref, ithr_ref, nblk, blk):
    nsub = blk // SUBLANES
    nacc = 4
    row8 = lax.broadcasted_iota(I32, (SUBLANES, LANES), 0)

    def count(pred):
        def sweep(c0, width, accs):
            accs = list(accs)
            rows = score_ref[pl.ds(c0, width), :]
            for j in range(width // SUBLANES):
                hit = pred(rows[j * SUBLANES:(j + 1) * SUBLANES], c0 + j * SUBLANES)
                accs[j % nacc] = accs[j % nacc] + jnp.where(hit, 1, 0)
            return tuple(accs)

        nquad = nblk // 4
        accs = lax.fori_loop(0, nquad, lambda c, a: sweep(pl.multiple_of(c * (4 * blk), 4 * blk), 4 * blk, a),
                             tuple(jnp.zeros((SUBLANES, LANES), I32) for _ in range(nacc)))
        accs = lax.cond(nblk % 4 >= 2, lambda a: sweep(pl.multiple_of(nquad * (4 * blk), 2 * blk), 2 * blk, a),
                        lambda a: a, accs)
        accs = lax.cond(nblk % 2 == 1, lambda a: sweep(pl.multiple_of((nblk - 1) * blk, blk), blk, a),
                        lambda a: a, accs)
        tot = (accs[0] + accs[1]) + (accs[2] + accs[3])
        return jnp.sum(tot, axis=0, keepdims=True)

    def bcast(v):
        return jnp.broadcast_to(v, (SUBLANES, LANES))

    def count_ge(probe):
        probe_b = bcast(probe)
        return count(lambda rows, row0: rows >= probe_b)

    n = count_ge(jnp.zeros((1, LANES), F32))
    ok = n >= TOPK
    thr_k = jnp.where(ok, 0, INT_MIN).astype(I32)
    n_ge = jnp.where(ok, n, INT_MAX).astype(I32)

    def bit_body(t, carry):
        thr_k, n_ge = carry
        cand_k = thr_k | lax.shift_left(jnp.int32(1), 30 - t)
        n = count_ge(_key_float(cand_k))
        ok = n >= TOPK
        return jnp.where(ok, cand_k, thr_k), jnp.where(ok, n, n_ge)

    thr_k, n_ge = lax.fori_loop(0, 31, bit_body, (thr_k, n_ge))
    found = thr_k > NEG_INF_KEY
    thr = jnp.where(found, _key_float(thr_k), -F32_MAX)
    thr_b = bcast(thr)
    tie = found & (n_ge > TOPK)
    ithr_ref[...] = jnp.full((SUBLANES, LANES), INT_MAX, I32)

    @pl.when(jnp.max(tie.astype(I32)) > 0)
    def _():
        n_gt = count(lambda rows, row0: rows > thr_b)
        need = TOPK - n_gt

        def idx_body(t, ithr):
            cand = ithr | lax.shift_left(jnp.int32(1), 11 - t)
            cand_b = bcast(cand)
            n = count(lambda rows, row0: (rows == thr_b) & ((row0 + row8) < cand_b))
            return jnp.where(n < need, cand, ithr)

        ithr = lax.fori_loop(0, 12, idx_body, jnp.zeros((1, LANES), I32))
        ithr_ref[...] = jnp.where(bcast(tie), bcast(ithr), ithr_ref[...])

    return thr


def _bias_block(score_ref, row0, thr_q, ithr_q):
    sc = score_ref[pl.ds(row0, LANES), :]
    key_index = row0 + lax.broadcasted_iota(I32, (LANES, LANES), 0)
    sel = (sc > thr_q) | ((sc == thr_q) & (key_index <= ithr_q))
    return jnp.where(sel, 0.0, NEG_BIG).astype(F32).T


def _dsa_prompt_kernel(q_ref, qi_ref, kiw_ref, kb_ref, vb_ref, kib_ref, o_ref,
                       score_ref, ithr_ref, qir_ref, s_ref, mlane_ref, llane_ref, acc_ref):
    i = pl.program_id(1)
    nchunk = (i * TQ + TQ + TK - 1) // TK
    qpos = i * TQ + lax.broadcasted_iota(I32, (1, LANES), 1)

    def for_chunks(body):
        def quad(c, carry):
            body(pl.multiple_of(c * (4 * TK), 4 * TK), 4 * TK)
            return carry
        nquad = nchunk // 4
        lax.fori_loop(0, nquad, quad, 0)

        @pl.when(nchunk % 4 >= 2)
        def _():
            body(pl.multiple_of(nquad * (4 * TK), 2 * TK), 2 * TK)

        @pl.when(nchunk % 2 == 1)
        def _():
            body(pl.multiple_of((nchunk - 1) * TK, TK), TK)

    for h in range(IDX_HEADS):
        qir_ref[h * TQ:(h + 1) * TQ, :] = qi_ref[0, :, h * IDX_DIM:(h + 1) * IDX_DIM]
    kiw_t = kiw_ref[0].T
    w_rows = [kiw_t[IDX_DIM + h:IDX_DIM + h + 1, :] * IDX_HEAD_SCALE for h in range(IDX_HEADS)]

    def idx_body(c0, width):
        logit = _dot_nt(kib_ref[0, pl.ds(c0, width), :], qir_ref[...])
        score = jnp.zeros((width, LANES), F32)
        for h in range(IDX_HEADS):
            score = score + jnp.maximum(logit[:, h * TQ:(h + 1) * TQ], 0.0) * w_rows[h]
        krow = c0 + lax.broadcasted_iota(I32, (width, LANES), 0)
        score_ref[pl.ds(c0, width), :] = jnp.where(krow <= qpos, score, -jnp.inf)

    for_chunks(idx_body)
    thr = _select_threshold(score_ref, ithr_ref, nchunk, TK)
    thr_q = jnp.broadcast_to(thr, (LANES, LANES))
    ithr_q = jnp.broadcast_to(ithr_ref[0:1, :], (LANES, LANES))

    def q_group(g):
        return jnp.concatenate(
            [q_ref[0, :, (g * GROUP + r) * HEAD_DIM:(g * GROUP + r + 1) * HEAD_DIM] for r in range(GROUP)], axis=0)

    mlane_ref[...] = jnp.full(mlane_ref.shape, NEG_BIG, F32)

    def logits_body(c0, width):
        bias = jnp.concatenate(
            [_bias_block(score_ref, pl.multiple_of(c0 + j * LANES, LANES), thr_q, ithr_q)
             for j in range(width // LANES)], axis=1)
        bias4 = jnp.concatenate([bias] * GROUP, axis=0)
        for g in range(N_KV_HEADS):
            s = _dot_nt(q_group(g), kb_ref[0, pl.ds(c0, width), g * HEAD_DIM:(g + 1) * HEAD_DIM]) + bias4
            s_ref[g, :, pl.ds(c0, width)] = s
            m = mlane_ref[g]
            for j in range(width // LANES):
                m = jnp.maximum(m, s[:, j * LANES:(j + 1) * LANES])
            mlane_ref[g] = m

    for_chunks(logits_body)

    for g in range(N_KV_HEADS):
        mlane_ref[g] = jnp.broadcast_to(jnp.max(mlane_ref[g], axis=1, keepdims=True), mlane_ref.shape[1:])
    llane_ref[...] = jnp.zeros(llane_ref.shape, F32)
    acc_ref[...] = jnp.zeros(acc_ref.shape, F32)

    def pv_body(c0, width):
        for g in range(N_KV_HEADS):
            m = mlane_ref[g]
            l = llane_ref[g]
            ps = []
            for j in range(width // LANES):
                pj = jnp.exp2(s_ref[g, :, pl.ds(pl.multiple_of(c0 + j * LANES, LANES), LANES)] - m)
                l = l + pj
                ps.append(pj.astype(BF16))
            llane_ref[g] = l
            p = jnp.concatenate(ps, axis=1)
            acc_ref[g] += _dot(p, vb_ref[0, pl.ds(c0, width), g * HEAD_DIM:(g + 1) * HEAD_DIM])

    for_chunks(pv_body)

    for g in range(N_KV_HEADS):
        og = acc_ref[g] * (1.0 / jnp.sum(llane_ref[g], axis=1, keepdims=True))
        for r in range(GROUP):
            col = (g * GROUP + r) * HEAD_DIM
            o_ref[0, :, col:col + HEAD_DIM] = og[r * TQ:(r + 1) * TQ].astype(BF16)


def _dsa_prompt_call(q, qi, kiw, kb, vb, kib):
    nb, s, _ = q.shape
    tile = lambda w: pl.BlockSpec((1, TQ, w), lambda b, i: (b, i, 0))
    full = lambda w: pl.BlockSpec((1, s, w), lambda b, i: (b, 0, 0))
    rows = GROUP * TQ
    return pl.pallas_call(
        _dsa_prompt_kernel,
        grid=(nb, s // TQ),
        in_specs=[tile(Q_W), tile(QI_W), tile(KIW_W), full(KV_W), full(KV_W), full(IDX_DIM)],
        out_specs=tile(Q_W),
        out_shape=jax.ShapeDtypeStruct((nb, s, Q_W), BF16),
        scratch_shapes=[
            pltpu.VMEM((s, TQ), F32),
            pltpu.VMEM((SUBLANES, LANES), I32),
            pltpu.VMEM((IDX_HEADS * TQ, IDX_DIM), BF16),
            pltpu.VMEM((N_KV_HEADS, rows, s), F32),
            pltpu.VMEM((N_KV_HEADS, rows, LANES), F32),
            pltpu.VMEM((N_KV_HEADS, rows, LANES), F32),
            pltpu.VMEM((N_KV_HEADS, rows, HEAD_DIM), F32),
        ],
        compiler_params=_params(("arbitrary", "arbitrary")),
        name="dsa_prompt",
    )(q, qi, kiw, kb, vb, kib)


def _page_specs(page_shape, nseq):
    zeros = (0,) * len(page_shape)
    return [pl.BlockSpec((1,) + page_shape,
                         functools.partial(lambda i, pt, u, p: (pt[i * nseq + u, p],) + zeros, u=u, p=p))
            for u in range(nseq) for p in range(N_PAGES)]


def _kv_page(page_ref):
    heads = [page_ref[0, pl.ds(g, PAGE_SIZE, stride=N_KV_HEADS), :] for g in range(N_KV_HEADS)]
    return jnp.concatenate(heads, axis=1).astype(BF16)


def _pad_new_page(new8):
    return jnp.concatenate([new8, jnp.zeros((PAGE_SIZE - Q8, new8.shape[1]), F32)], axis=0).astype(BF16)


def _idx_sample_kernel(pt_ref, qis_ref, ws_ref, kin_ref, *rest):
    nseq = qis_ref.shape[0]
    pages = rest[:nseq * N_PAGES]
    o_ref = rest[nseq * N_PAGES]
    del pt_ref
    for u in range(nseq):
        qs = qis_ref[u]
        w = ws_ref[u]

        def page_score(kt):
            r = jnp.maximum(_dot(qs, kt.astype(BF16)), 0.0) * w
            s = r[0:Q8]
            for h in range(1, IDX_HEADS):
                s = s + r[h * Q8:(h + 1) * Q8]
            return s

        for p in range(N_PAGES):
            o_ref[u, :, p * PAGE_SIZE:(p + 1) * PAGE_SIZE] = page_score(pages[u * N_PAGES + p][0])
        o_ref[u, :, PAST_LEN:L_SAMPLE] = page_score(kin_ref[u])


def _idx_sample_call(page_table, qis, ws, kinew_t, cache_ki_t):
    n = qis.shape[0]
    rows = IDX_HEADS * Q8
    nseq = SCORE_SEQS
    grid_spec = pltpu.PrefetchScalarGridSpec(
        num_scalar_prefetch=1,
        grid=(n // nseq,),
        in_specs=[
            pl.BlockSpec((nseq, rows, IDX_DIM), lambda i, pt: (i, 0, 0)),
            pl.BlockSpec((nseq, rows, 1), lambda i, pt: (i, 0, 0)),
            pl.BlockSpec((nseq, IDX_DIM, PAGE_SIZE), lambda i, pt: (i, 0, 0)),
        ] + _page_specs((IDX_DIM, PAGE_SIZE), nseq),
        out_specs=pl.BlockSpec((nseq, Q8, L_SAMPLE), lambda i, pt: (i, 0, 0)),
    )
    return pl.pallas_call(
        _idx_sample_kernel,
        grid_spec=grid_spec,
        out_shape=jax.ShapeDtypeStruct((n, Q8, L_SAMPLE), F32),
        compiler_params=_params(("arbitrary",)),
        name="dsa_sample_scores",
    )(page_table, qis, ws, kinew_t, *([cache_ki_t] * (nseq * N_PAGES)))


def _select_sample_kernel(s_ref, bias_ref, score_ref, ithr_ref):
    krow = lax.broadcasted_iota(I32, (LANES, SEL_ROWS), 0)
    qrow = lax.broadcasted_iota(I32, (LANES, SEL_ROWS), 1) % Q8
    real = qrow < T_NEW
    for j in range(L_SAMPLE // LANES):
        kidx = j * LANES + krow
        valid = ((kidx < PAST_LEN) | ((kidx - PAST_LEN) <= qrow)) & real
        score_ref[j * LANES:(j + 1) * LANES, :] = jnp.where(valid, s_ref[:, j * LANES:(j + 1) * LANES].T, -jnp.inf)
    thr = _select_threshold(score_ref, ithr_ref, L_SAMPLE // LANES, LANES)
    thr_q = jnp.broadcast_to(thr, (LANES, SEL_ROWS))
    ithr_q = jnp.broadcast_to(ithr_ref[0:1, :], (LANES, SEL_ROWS))
    for j in range(L_SAMPLE // LANES):
        bias_ref[:, j * LANES:(j + 1) * LANES] = _bias_block(score_ref, j * LANES, thr_q, ithr_q)


def _select_sample_call(scores):
    rows = scores.shape[0]
    return pl.pallas_call(
        _select_sample_kernel,
        grid=(rows // SEL_ROWS,),
        in_specs=[pl.BlockSpec((SEL_ROWS, L_SAMPLE), lambda i: (i, 0))],
        out_specs=pl.BlockSpec((SEL_ROWS, L_SAMPLE), lambda i: (i, 0)),
        out_shape=jax.ShapeDtypeStruct((rows, L_SAMPLE), F32),
        scratch_shapes=[pltpu.VMEM((L_SAMPLE, SEL_ROWS), F32), pltpu.VMEM((SUBLANES, LANES), I32)],
        compiler_params=_params(("arbitrary",)),
        name="dsa_sample_select",
    )(scores)


def _attn_sample_kernel(pt_ref, qb_ref, bias_ref, kn_ref, vn_ref, *rest):
    nseq = qb_ref.shape[0]
    kpages = rest[:nseq * N_PAGES]
    vpages = rest[nseq * N_PAGES:2 * nseq * N_PAGES]
    o_ref = rest[2 * nseq * N_PAGES]
    s_ref = rest[2 * nseq * N_PAGES + 1]
    del pt_ref
    for u in range(nseq):
        qb = qb_ref[u]
        reps = qb.shape[0] // Q8

        def page_logits(p, kp):
            bias8 = bias_ref[u, :, p * PAGE_SIZE:(p + 1) * PAGE_SIZE]
            s_ref[u, :, p * PAGE_SIZE:(p + 1) * PAGE_SIZE] = (
                _dot_nt(qb, kp) + jnp.concatenate([bias8] * reps, axis=0))

        for p in range(N_PAGES):
            page_logits(p, _kv_page(kpages[u * N_PAGES + p]))
        page_logits(N_PAGES, _pad_new_page(kn_ref[u]))

        s = s_ref[u]
        m = jnp.max(s, axis=1, keepdims=True)
        e = jnp.exp2(s - m)
        l = jnp.sum(e, axis=1, keepdims=True)
        eb = e.astype(BF16)
        acc = _dot(eb[:, PAST_LEN:L_SAMPLE], _pad_new_page(vn_ref[u]))
        for p in range(N_PAGES):
            acc = acc + _dot(eb[:, p * PAGE_SIZE:(p + 1) * PAGE_SIZE], _kv_page(vpages[u * N_PAGES + p]))
        o_ref[u] = acc * (1.0 / l)


def _attn_sample_call(page_table, qblk, bias, knew8, vnew8, cache_k, cache_v):
    n, rows, _ = qblk.shape
    nseq = ATTN_SEQS
    page = (PAGE_SIZE * N_KV_HEADS, HEAD_DIM)
    grid_spec = pltpu.PrefetchScalarGridSpec(
        num_scalar_prefetch=1,
        grid=(n // nseq,),
        in_specs=[
            pl.BlockSpec((nseq, rows, KV_W), lambda i, pt: (i, 0, 0)),
            pl.BlockSpec((nseq, Q8, L_SAMPLE), lambda i, pt: (i, 0, 0)),
            pl.BlockSpec((nseq, Q8, KV_W), lambda i, pt: (i, 0, 0)),
            pl.BlockSpec((nseq, Q8, KV_W), lambda i, pt: (i, 0, 0)),
        ] + _page_specs(page, nseq) + _page_specs(page, nseq),
        out_specs=pl.BlockSpec((nseq, rows, KV_W), lambda i, pt: (i, 0, 0)),
        scratch_shapes=[pltpu.VMEM((nseq, rows, L_SAMPLE), F32)],
    )
    return pl.pallas_call(
        _attn_sample_kernel,
        grid_spec=grid_spec,
        out_shape=jax.ShapeDtypeStruct((n, rows, KV_W), F32),
        compiler_params=_params(("arbitrary",)),
        name="dsa_sample_attn",
    )(page_table, qblk, bias, knew8, vnew8, *([cache_k] * (nseq * N_PAGES)), *([cache_v] * (nseq * N_PAGES)))


def _mlp_resid(x1, g_ffn, shift, scale, gate, wup_ref, wdn_ref, acc_ref):
    h = _rms_mod(x1, g_ffn, shift, scale).astype(BF16)
    for c in range(D_FF // FF_CHUNK):
        u = _dot(h, wup_ref[:, c * FF_CHUNK:(c + 1) * FF_CHUNK])
        u = jnp.square(jnp.maximum(u, 0.0)).astype(BF16)
        d = _dot(u, wdn_ref[c * FF_CHUNK:(c + 1) * FF_CHUNK, :])
        if c == 0:
            acc_ref[...] = d
        else:
            acc_ref[...] += d
    return x1 + gate * acc_ref[...]


def _post0_kernel(x_ref, o_ref, g1_ref, sh2_ref, sc2_ref, g2_ref, gffn_ref, wo_ref, wup_ref, wdn_ref,
                  out_ref, acc_ref):
    x1 = x_ref[0] + g1_ref[0] * _dot(o_ref[0], wo_ref[...])
    out_ref[0] = _mlp_resid(x1, gffn_ref[...], sh2_ref[0], sc2_ref[0], g2_ref[0], wup_ref, wdn_ref, acc_ref)


def _const_spec(shape):
    nd = len(shape)
    return pl.BlockSpec(shape, lambda b, i: (0,) * nd)


def _post0_call(x, o, g1, sh2, sc2, g2, g_ffn, w_o, w_up, w_dn, tm):
    nb, t, _ = x.shape
    tok = lambda w: pl.BlockSpec((1, tm, w), lambda b, i: (b, i, 0))
    return pl.pallas_call(
        _post0_kernel,
        grid=(nb, t // tm),
        in_specs=[tok(D_MODEL), tok(Q_W), g1.spec(tm), sh2.spec(tm), sc2.spec(tm),
                  g2.spec(tm), _const_spec((1, D_MODEL)), _const_spec((Q_W, D_MODEL)),
                  _const_spec((D_MODEL, D_FF)), _const_spec((D_FF, D_MODEL))],
        out_specs=tok(D_MODEL),
        out_shape=jax.ShapeDtypeStruct((nb, t, D_MODEL), F32),
        scratch_shapes=[pltpu.VMEM((tm, D_MODEL), F32)],
        compiler_params=_params(("arbitrary", "arbitrary")),
        name="layer0_out",
    )(x, o, g1.arr, sh2.arr, sc2.arr, g2.arr, g_ffn.reshape(1, D_MODEL), w_o, w_up, w_dn)


def _pool_project(pooled, wp_ref, ps_ref):
    parts = []
    for g in range(len(POOL_WINDOWS)):
        lo = g * POOL_GROUP_DIM
        parts.append(_dot(pooled[:, lo:lo + POOL_GROUP_DIM].astype(BF16), wp_ref[g]))
    return jnp.concatenate(parts, axis=1) * ps_ref[...]


def _layers_prompt_kernel(x_ref, o_ref, g1a_ref, sh2a_ref, sc2a_ref, g2a_ref,
                          sh1_ref, sc1_ref, g1_ref, sh2_ref, sc2_ref, g2_ref,
                          gffna_ref, wo_ref, wupa_ref, wdna_ref,
                          gmix_ref, gffn_ref, gfin_ref, wp_ref, ps_ref, wup_ref, wdn_ref,
                          y_ref, hlast_ref, acc_ref, halo_ref):
    i = pl.program_id(1)
    tm = x_ref.shape[1]
    x = x_ref[0] + g1a_ref[0] * _dot(o_ref[0], wo_ref[...])
    x = _mlp_resid(x, gffna_ref[...], sh2a_ref[0], sc2a_ref[0], g2a_ref[0], wupa_ref, wdna_ref, acc_ref)
    h = _rms_mod(x, gmix_ref[...], sh1_ref[0], sc1_ref[0])

    @pl.when(i == 0)
    def _():
        halo_ref[...] = jnp.zeros(halo_ref.shape, F32)

    ext = jnp.concatenate([halo_ref[...], h], axis=0)
    halo_ref[...] = h[tm - HALO:, :]
    pos = i * tm + lax.broadcasted_iota(I32, (tm, 1), 0)
    parts = []
    for g, w in enumerate(POOL_WINDOWS):
        lo = g * POOL_GROUP_DIM
        s = ext[:, lo:lo + POOL_GROUP_DIM]
        step = 1
        while step < w:
            s = s + pltpu.roll(s, step, axis=0)
            step *= 2
        cnt = jnp.minimum(w, pos + 1).astype(F32)
        parts.append(s[HALO:, :] / cnt - h[:, lo:lo + POOL_GROUP_DIM])
    pooled = jnp.concatenate(parts, axis=1)
    x1 = x + g1_ref[0] * _pool_project(pooled, wp_ref, ps_ref)
    x2 = _mlp_resid(x1, gffn_ref[...], sh2_ref[0], sc2_ref[0], g2_ref[0], wup_ref, wdn_ref, acc_ref)
    y_ref[0] = _rmsnorm(x2, gfin_ref[...])

    @pl.when(i == pl.num_programs(1) - 1)
    def _():
        hlast_ref[0] = h[tm - HALO:, :]


def _resident_spec(shape):
    nd = len(shape)
    return pl.BlockSpec(shape, lambda b, i: (0,) * nd, pipeline_mode=pl.Buffered(1))


def _layers_prompt_call(x, o, mods0, mods1, g_ffn0, w_o, w_up0, w_dn0,
                        g_mix, g_ffn, g_fin, pool_w, pool_scale, w_up, w_dn, tm):
    nb, t, _ = x.shape
    tok = lambda w: pl.BlockSpec((1, tm, w), lambda b, i: (b, i, 0))
    vec = _const_spec((1, D_MODEL))
    up, dn = _resident_spec((D_MODEL, D_FF)), _resident_spec((D_FF, D_MODEL))
    mods = list(mods0[2:]) + list(mods1)
    return pl.pallas_call(
        _layers_prompt_kernel,
        grid=(nb, t // tm),
        in_specs=[tok(D_MODEL), tok(Q_W)] + [m.spec(tm) for m in mods]
                 + [vec, _resident_spec((Q_W, D_MODEL)), up, dn,
                    vec, vec, vec, _resident_spec(pool_w.shape), vec, up, dn],
        out_specs=[tok(D_MODEL), pl.BlockSpec((1, HALO, D_MODEL), lambda b, i: (b, 0, 0))],
        out_shape=[jax.ShapeDtypeStruct((nb, t, D_MODEL), F32), jax.ShapeDtypeStruct((nb, HALO, D_MODEL), F32)],
        scratch_shapes=[pltpu.VMEM((tm, D_MODEL), F32), pltpu.VMEM((HALO, D_MODEL), F32)],
        compiler_params=_params(("arbitrary", "arbitrary")),
        name="layers_prompt",
    )(x, o, *[m.arr for m in mods], g_ffn0.reshape(1, D_MODEL), w_o, w_up0, w_dn0,
      g_mix.reshape(1, D_MODEL), g_ffn.reshape(1, D_MODEL), g_fin.reshape(1, D_MODEL),
      pool_w, pool_scale.reshape(1, D_MODEL), w_up, w_dn)


def _layer1_sample_kernel(x_ref, prev_ref, sh1_ref, sc1_ref, g1_ref, sh2_ref, sc2_ref, g2_ref,
                          gmix_ref, gffn_ref, gfin_ref, wp_ref, ps_ref, wup_ref, wdn_ref,
                          y_ref, h_ref, hs_ref, pooled_ref, acc_ref):
    tm = x_ref.shape[1]
    t_new = tm // prev_ref.shape[1]
    nseq = prev_ref.shape[1]
    x = x_ref[0]
    h = _rms_mod(x, gmix_ref[...], sh1_ref[0], sc1_ref[0])
    h_ref[0] = h
    ncol = D_MODEL // LANES
    for c in range(ncol):
        hs_ref[c] = h[:, c * LANES:(c + 1) * LANES]
    ext = [prev_ref[j] for j in range(POOL_STATE)]
    ext += [jnp.concatenate([hs_ref[c, pl.ds(t, nseq, stride=t_new), :] for c in range(ncol)], axis=1)
            for t in range(t_new)]
    for t in range(t_new):
        parts = []
        for g, w in enumerate(POOL_WINDOWS):
            lo = g * POOL_GROUP_DIM
            s = ext[POOL_STATE + t][:, lo:lo + POOL_GROUP_DIM]
            for j in range(1, w):
                s = s + ext[POOL_STATE + t - j][:, lo:lo + POOL_GROUP_DIM]
            parts.append(s / float(w) - ext[POOL_STATE + t][:, lo:lo + POOL_GROUP_DIM])
        pooled_t = jnp.concatenate(parts, axis=1)
        for c in range(ncol):
            pooled_ref[c, pl.ds(t, nseq, stride=t_new), :] = pooled_t[:, c * LANES:(c + 1) * LANES]
    pooled = jnp.concatenate([pooled_ref[c] for c in range(ncol)], axis=1)
    x1 = x + g1_ref[0] * _pool_project(pooled, wp_ref, ps_ref)
    x2 = _mlp_resid(x1, gffn_ref[...], sh2_ref[0], sc2_ref[0], g2_ref[0], wup_ref, wdn_ref, acc_ref)
    y_ref[0] = _rmsnorm(x2, gfin_ref[...])


def _layer1_sample_call(x, prev_t, mods, g_mix, g_ffn, g_fin, pool_w, pool_scale, w_up, w_dn, tm, t_new):
    nb, t, _ = x.shape
    tok = pl.BlockSpec((1, tm, D_MODEL), lambda b, i: (b, i, 0))
    prev = pl.BlockSpec((POOL_STATE, tm // t_new, D_MODEL), lambda b, i: (0, i, 0))
    vec = _const_spec((1, D_MODEL))
    return pl.pallas_call(
        _layer1_sample_kernel,
        grid=(nb, t // tm),
        in_specs=[tok, prev] + [m.spec(tm) for m in mods] + [vec, vec, vec,
                  _const_spec(pool_w.shape), vec, _const_spec((D_MODEL, D_FF)), _const_spec((D_FF, D_MODEL))],
        out_specs=[tok, tok],
        out_shape=[jax.ShapeDtypeStruct((nb, t, D_MODEL), F32), jax.ShapeDtypeStruct((nb, t, D_MODEL), F32)],
        scratch_shapes=[pltpu.VMEM((D_MODEL // LANES, tm, LANES), F32),
                        pltpu.VMEM((D_MODEL // LANES, tm, LANES), F32),
                        pltpu.VMEM((tm, D_MODEL), F32)],
        compiler_params=_params(("arbitrary", "arbitrary")),
        name="layer1_sample",
    )(x, prev_t, *[m.arr for m in mods], g_mix.reshape(1, D_MODEL), g_ffn.reshape(1, D_MODEL),
      g_fin.reshape(1, D_MODEL), pool_w, pool_scale.reshape(1, D_MODEL), w_up, w_dn)


def kernel(x_prompt, x_sample, cache_k, cache_v, cache_kidx, state_pool, page_table, c_prompt, c_sample,
           norm_mix_g, norm_ffn_g, w_mod, b_mod, dsa_w_in, dsa_w_o, pool_w, pool_scale, w_up, w_down,
           final_norm_g):
    nbp, seq, _ = x_prompt.shape
    nbs, t_new, _ = x_sample.shape
    n_tok_s = nbs * t_new

    w_in_b = jnp.pad(dsa_w_in[0], ((0, 0), (0, D_PROJ_PAD - D_PROJ))).astype(BF16)
    w_o_b = dsa_w_o[0].astype(BF16)
    w_up_b = w_up.astype(BF16)
    w_dn_b = w_down.astype(BF16)
    pool_w_b = pool_w[0].astype(BF16)

    n_c = nbp + nbs
    c_rows = -(-n_c // SUBLANES) * SUBLANES
    c_all = jnp.pad(jnp.concatenate([c_prompt, c_sample], axis=0), ((0, c_rows - n_c), (0, 0)))
    mod, mod_tok = _mod_call(c_all, w_mod, b_mod, nbp, nbs, t_new)

    def mods_prompt(layer):
        m = mod[layer, :nbp].reshape(nbp, 1, 6, D_MODEL)
        return [_mod_per_sequence(m[:, :, j]) for j in range(6)]

    def mods_sample(layer):
        return [_mod_per_token(mod_tok, layer, j) for j in range(6)]

    sh1, sc1, g1, sh2, sc2, g2 = mods_prompt(0)
    k_p, v_p, ki_p, q, qi, kiw, kb, vb, kib = _proj_call(x_prompt, sh1, sc1, norm_mix_g[0], w_in_b, 512)
    o = _dsa_prompt_call(q, qi, kiw, kb, vb, kib)
    y_prompt, hlast = _layers_prompt_call(x_prompt, o, mods_prompt(0), mods_prompt(1), norm_ffn_g[0], w_o_b,
                                          w_up_b[0], w_dn_b[0], norm_mix_g[1], norm_ffn_g[1], final_norm_g,
                                          pool_w_b, pool_scale[0], w_up_b[1], w_dn_b[1], 512)
    pool_p = hlast[:, HALO - POOL_STATE:]

    sh1, sc1, g1, sh2, sc2, g2 = mods_sample(0)
    xs = x_sample.reshape(1, n_tok_s, D_MODEL)
    tms = 256
    k_s, v_s, ki_s, q, qi, kiw, _, _, _ = _proj_call(xs, sh1, sc1, norm_mix_g[0], w_in_b, tms)

    pad_q = lambda a: jnp.pad(a, [(0, 0)] * (a.ndim - 2) + [(0, Q8 - t_new), (0, 0)])
    qis = pad_q(qi.reshape(nbs, t_new, IDX_HEADS, IDX_DIM).transpose(0, 2, 1, 3)).reshape(nbs, IDX_HEADS * Q8, IDX_DIM)
    wi = kiw.reshape(nbs, t_new, KIW_W)[:, :, IDX_DIM:IDX_DIM + IDX_HEADS] * IDX_HEAD_SCALE
    ws = pad_q(wi.transpose(0, 2, 1)[..., None]).reshape(nbs, IDX_HEADS * Q8, 1)
    knew8 = pad_q(k_s.reshape(nbs, t_new, KV_W))
    vnew8 = pad_q(v_s.reshape(nbs, t_new, KV_W))
    kinew_t = jnp.pad(ki_s.reshape(nbs, t_new, IDX_DIM).transpose(0, 2, 1), ((0, 0), (0, 0), (0, PAGE_SIZE - t_new)))
    scores = _idx_sample_call(page_table, qis, ws, kinew_t, cache_kidx[0].transpose(0, 2, 1))
    bias = _select_sample_call(scores.reshape(nbs * Q8, L_SAMPLE)).reshape(nbs, Q8, L_SAMPLE)
    qg = pad_q(q.reshape(nbs, t_new, N_KV_HEADS, GROUP, HEAD_DIM).transpose(0, 2, 3, 1, 4))
    eye = jnp.eye(N_KV_HEADS, dtype=BF16)
    qblk = (qg[:, :, :, :, None, :] * eye[None, :, None, None, :, None]).reshape(nbs, N_KV_HEADS * GROUP * Q8, KV_W)
    n_pool = cache_k.shape[1]
    page_rows = lambda c: c[0].reshape(n_pool, PAGE_SIZE * N_KV_HEADS, HEAD_DIM)
    o_blk = _attn_sample_call(page_table, qblk, bias, knew8, vnew8, page_rows(cache_k), page_rows(cache_v))
    o_blk = o_blk.reshape(nbs, N_KV_HEADS, GROUP, Q8, N_KV_HEADS, HEAD_DIM)
    o_s = jnp.stack([o_blk[:, g, :, :t_new, g] for g in range(N_KV_HEADS)], axis=1)
    o_s = o_s.transpose(0, 3, 1, 2, 4).reshape(1, n_tok_s, Q_W).astype(BF16)
    x1s = _post0_call(xs, o_s, g1, sh2, sc2, g2, norm_ffn_g[0], w_o_b, w_up_b[0], w_dn_b[0], tms)
    prev_t = state_pool[0].transpose(1, 0, 2)
    y_s, h1s = _layer1_sample_call(x1s, prev_t, mods_sample(1), norm_mix_g[1], norm_ffn_g[1], final_norm_g,
                                   pool_w_b, pool_scale[0], w_up_b[1], w_dn_b[1], tms, t_new)
    y_sample = y_s.reshape(nbs, t_new, D_MODEL)
    pool_s = jnp.concatenate([state_pool[0][:, t_new:], h1s.reshape(nbs, t_new, D_MODEL)], axis=1)

    return (
        y_prompt, y_sample,
        k_p[None], v_p[None], ki_p[None], pool_p[None],
        k_s.reshape(1, nbs, t_new, N_KV_HEADS, HEAD_DIM), v_s.reshape(1, nbs, t_new, N_KV_HEADS, HEAD_DIM),
        ki_s.reshape(1, nbs, t_new, IDX_DIM), pool_s[None],
    )
```
